```python
import jax, jax.numpy as jnp
from jax import lax
import numpy as np

D_MODEL = 4096
BATCH = 4
SEQ = 2048
DEPTH = 2
DEC_BATCH = 8
DEC_SEQ = 4
PAST_LEN = 16384
PAGE_SIZE = 128

N_MIXERS = 2
HEAD_DIM = 128
N_HEADS = (3 * D_MODEL) // (4 * HEAD_DIM)
N_KV = N_HEADS // 3
GROUP = N_HEADS // N_KV
ROT_DIM = HEAD_DIM // 4
ROPE_THETA = 500000.0
MEM_LEN = 256
MEM_HEADS = 4
MEM_HEAD_DIM = D_MODEL // (4 * MEM_HEADS)
IDX_HEADS = D_MODEL // 128
IDX_DIM = 128
IDX_ROT = IDX_DIM // 4
DSA_TOPK = 256
DSA_QBLOCK = 32
MOBA_BLOCK = 256
MOBA_TOPB = 3
MOBA_QCHUNK = 4
D_FF = 4 * D_MODEL
EPS = 1e-6
Q_W = N_HEADS * HEAD_DIM
KV_W = N_KV * HEAD_DIM
MEM_W = MEM_HEADS * MEM_HEAD_DIM
MIX_W = Q_W + MEM_W
DSA_SPLITS = (Q_W, KV_W, KV_W, IDX_HEADS * IDX_DIM, IDX_HEADS, IDX_DIM, MEM_W)
MOBA_SPLITS = (Q_W, KV_W, KV_W, MEM_W)
DSA_IN = sum(DSA_SPLITS)
MOBA_IN = sum(MOBA_SPLITS)
N_DSA = (DEPTH + 1) // 2
N_MOBA = DEPTH // 2

kernel_name = 'dsa_moba_memxattn_hybrid_step'


def rmsnorm(x, g):
    xf = x.astype(jnp.float32)
    y = xf * lax.rsqrt(jnp.mean(xf * xf, axis=-1, keepdims=True) + EPS)
    return (y * g.astype(jnp.float32)).astype(x.dtype)


def rope(x, pos, rot_dim):
    half = rot_dim // 2
    inv = ROPE_THETA ** (-jnp.arange(half, dtype=jnp.float32) / half)
    ang = pos.astype(jnp.float32)[:, None] * inv[None, :]
    shape = (pos.shape[0],) + (1,) * (x.ndim - 3) + (half,)
    cos = jnp.cos(ang).reshape(shape).astype(x.dtype)
    sin = jnp.sin(ang).reshape(shape).astype(x.dtype)
    x1 = x[..., :half]
    x2 = x[..., half:rot_dim]
    return jnp.concatenate([x1 * cos - x2 * sin, x2 * cos + x1 * sin, x[..., rot_dim:]], axis=-1)


def split_cols(h, widths):
    cuts = [int(c) for c in np.cumsum(widths)[:-1]]
    return jnp.split(h, cuts, axis=-1)


def gather_pages(pool, page_table):
    g = pool[page_table]
    return g.reshape((g.shape[0], g.shape[1] * g.shape[2]) + g.shape[3:])


def memory_kv(mem, g, w):
    B, M, _ = mem.shape
    k, v = jnp.split(rmsnorm(mem, g) @ w, 2, axis=-1)
    return (k.reshape(B, M, MEM_HEADS, MEM_HEAD_DIM), v.reshape(B, M, MEM_HEADS, MEM_HEAD_DIM))


def memory_attention(qm, mk, mv):
    B, T = qm.shape[:2]
    logits = jnp.einsum('bthd,bmhd->bthm', qm, mk, preferred_element_type=jnp.float32) * MEM_HEAD_DIM ** -0.5
    p = jax.nn.softmax(logits, axis=-1).astype(mv.dtype)
    return jnp.einsum('bthm,bmhd->bthd', p, mv).reshape(B, T, MEM_W)


def dsa_project(hn, w_in, pos):
    B, T, _ = hn.shape
    q, k, v, qi, wi, ki, qm = split_cols(hn @ w_in, DSA_SPLITS)
    q = rope(q.reshape(B, T, N_HEADS, HEAD_DIM), pos, ROT_DIM)
    k = rope(k.reshape(B, T, N_KV, HEAD_DIM), pos, ROT_DIM)
    v = v.reshape(B, T, N_KV, HEAD_DIM)
    qi = rope(qi.reshape(B, T, IDX_HEADS, IDX_DIM), pos, IDX_ROT)
    ki = rope(ki, pos, IDX_ROT)
    wi = wi * (IDX_HEADS * IDX_DIM) ** -0.5
    qm = qm.reshape(B, T, MEM_HEADS, MEM_HEAD_DIM)
    return q, k, v, qi, wi, ki, qm


def dsa_block(q, qi, wi, q_pos, k_all, v_all, ki_all, k_pos, topk):
    B, Tq = q.shape[:2]
    s = jnp.einsum('bthd,bsd->bths', qi, ki_all, preferred_element_type=jnp.float32)
    score = jnp.einsum('bths,bth->bts', jax.nn.relu(s), wi.astype(jnp.float32))
    causal = k_pos[None, :] <= q_pos[:, None]
    score = jnp.where(causal[None], score, -jnp.inf)
    _, sel = lax.top_k(score, topk)
    valid = k_pos[sel] <= q_pos[None, :, None]
    bidx = jnp.arange(B)[:, None, None]
    k_sel = k_all[bidx, sel]
    v_sel = v_all[bidx, sel]
    qg = q.reshape(B, Tq, N_KV, GROUP, HEAD_DIM)
    logits = jnp.einsum('btngd,btknd->btngk', qg, k_sel, preferred_element_type=jnp.float32) * HEAD_DIM ** -0.5
    logits = jnp.where(valid[:, :, None, None, :], logits, -jnp.inf)
    p = jax.nn.softmax(logits, axis=-1).astype(v_sel.dtype)
    o = jnp.einsum('btngk,btknd->btngd', p, v_sel)
    return o.reshape(B, Tq, Q_W)


def dsa_mixer(hn, pos, w_in, past):
    q, k, v, qi, wi, ki, qm = dsa_project(hn, w_in, pos)
    B, T = q.shape[:2]
    if past is None:
        topk = min(DSA_TOPK, T // 4)
        nb = T // DSA_QBLOCK
        def to_blocks(a):
            return jnp.moveaxis(a.reshape((B, nb, DSA_QBLOCK) + a.shape[2:]), 1, 0)
        xs = (to_blocks(q), to_blocks(qi), to_blocks(wi), pos.reshape(nb, DSA_QBLOCK))
        out = lax.map(lambda a: dsa_block(a[0], a[1], a[2], a[3], k, v, ki, pos, topk), xs)
        mix = jnp.moveaxis(out, 0, 1).reshape(B, T, Q_W)
    else:
        k_all = jnp.concatenate([past[0], k], axis=1)
        v_all = jnp.concatenate([past[1], v], axis=1)
        ki_all = jnp.concatenate([past[2], ki], axis=1)
        L = k_all.shape[1]
        k_pos = jnp.arange(L, dtype=jnp.int32)
        mix = dsa_block(q, qi, wi, pos, k_all, v_all, ki_all, k_pos, min(DSA_TOPK, L // 4))
    return mix, qm, (k, v, ki)


def moba_project(hn, w_in, pos):
    B, T, _ = hn.shape
    q, k, v, qm = split_cols(hn @ w_in, MOBA_SPLITS)
    q = rope(q.reshape(B, T, N_HEADS, HEAD_DIM), pos, ROT_DIM)
    k = rope(k.reshape(B, T, N_KV, HEAD_DIM), pos, ROT_DIM)
    v = v.reshape(B, T, N_KV, HEAD_DIM)
    qm = qm.reshape(B, T, MEM_HEADS, MEM_HEAD_DIM)
    return q, k, v, qm


def moba_chunk(q, q_pos, kb, vb, kmean, kb_pos, n_sel):
    B, C = q.shape[:2]
    qblk = q_pos // MOBA_BLOCK
    own = jnp.broadcast_to(qblk[None, :, None, None], (B, C, N_HEADS, 1))
    if n_sel > 0:
        qg = q.reshape(B, C, N_KV, GROUP, HEAD_DIM)
        gate = jnp.einsum('bcngd,bmnd->bcngm', qg.astype(jnp.float32), kmean).reshape(B, C, N_HEADS, -1)
        past_ok = jnp.arange(kmean.shape[1])[None, :] < qblk[:, None]
        gate = jnp.where(past_ok[None, :, None, :], gate, -jnp.inf)
        _, gsel = lax.top_k(gate, n_sel)
        slot_ok = gsel < qblk[None, :, None, None]
        sel = jnp.concatenate([gsel, own], axis=-1)
        ok = jnp.concatenate([slot_ok, jnp.ones(own.shape, dtype=bool)], axis=-1)
    else:
        sel = own
        ok = jnp.ones(own.shape, dtype=bool)
    bidx = jnp.arange(B)[:, None, None, None]
    hkv = (jnp.arange(N_HEADS) // GROUP)[None, None, :, None]
    k_sel = kb[bidx, sel, hkv]
    v_sel = vb[bidx, sel, hkv]
    pos_sel = kb_pos[sel]
    mask = ok[..., None] & (pos_sel <= q_pos[None, :, None, None, None])
    logits = jnp.einsum('bchd,bchskd->bchsk', q, k_sel, preferred_element_type=jnp.float32) * HEAD_DIM ** -0.5
    logits = jnp.where(mask, logits, -jnp.inf)
    p = jax.nn.softmax(logits, axis=(-2, -1)).astype(v_sel.dtype)
    o = jnp.einsum('bchsk,bchskd->bchd', p, v_sel)
    return o.reshape(B, C, Q_W)


def moba_attention(q, q_pos, k_all, v_all, chunk):
    B, T = q.shape[:2]
    L = k_all.shape[1]
    nb = -(-L // MOBA_BLOCK)
    nbf = L // MOBA_BLOCK
    pad = nb * MOBA_BLOCK - L
    def blocks(a):
        a = jnp.pad(a, ((0, 0), (0, pad), (0, 0), (0, 0)))
        return a.reshape(B, nb, MOBA_BLOCK, N_KV, HEAD_DIM).transpose(0, 1, 3, 2, 4)
    kb = blocks(k_all)
    vb = blocks(v_all)
    n_sel = min(MOBA_TOPB, nbf)
    kmean = jnp.mean(kb[:, :nbf].astype(jnp.float32), axis=3) if n_sel > 0 else None
    kb_pos = jnp.arange(nb * MOBA_BLOCK, dtype=jnp.int32).reshape(nb, MOBA_BLOCK)
    nc = T // chunk
    qs = jnp.moveaxis(q.reshape(B, nc, chunk, N_HEADS, HEAD_DIM), 1, 0)
    ps = q_pos.reshape(nc, chunk)
    out = lax.map(lambda a: moba_chunk(a[0], a[1], kb, vb, kmean, kb_pos, n_sel), (qs, ps))
    return jnp.moveaxis(out, 0, 1).reshape(B, T, Q_W)


def moba_mixer(hn, pos, w_in, past):
    q, k, v, qm = moba_project(hn, w_in, pos)
    if past is None:
        mix = moba_attention(q, pos, k, v, MOBA_QCHUNK)
    else:
        mix = moba_attention(q, pos, jnp.concatenate([past[0], k], axis=1), jnp.concatenate([past[1], v], axis=1), 1)
    return mix, qm, (k, v)


def finish_layer(h, mix, qm, mem_k, mem_v, w_o, g_ffn, w_up, w_down):
    merged = jnp.concatenate([mix, memory_attention(qm, mem_k, mem_v)], axis=-1)
    h = h + merged @ w_o
    u = rmsnorm(h, g_ffn) @ w_up
    return h + jnp.square(jax.nn.relu(u)) @ w_down


def setup_inputs(seed: int = 0) -> dict:
    key = jax.random.key(seed)
    ks = jax.random.split(key, 24)
    f32 = jnp.float32
    n_pages = PAST_LEN // PAGE_SIZE
    n_used = DEC_BATCH * n_pages
    n_phys = n_used + n_used // 4
    def nrm(k, shape, scale=1.0):
        return jax.random.normal(k, shape, f32) * scale
    page_table = jax.random.permutation(ks[0], n_phys)[:n_used].reshape(DEC_BATCH, n_pages).astype(jnp.int32)
    return {
        'x_prompt': nrm(ks[1], (BATCH, SEQ, D_MODEL)),
        'x_sample': nrm(ks[2], (DEC_BATCH, DEC_SEQ, D_MODEL)),
        'cache_dsa_k': nrm(ks[3], (N_DSA, n_phys, PAGE_SIZE, N_KV, HEAD_DIM)),
        'cache_dsa_v': nrm(ks[4], (N_DSA, n_phys, PAGE_SIZE, N_KV, HEAD_DIM)),
        'cache_dsa_kidx': nrm(ks[5], (N_DSA, n_phys, PAGE_SIZE, IDX_DIM)),
        'cache_moba_k': nrm(ks[6], (N_MOBA, n_phys, PAGE_SIZE, N_KV, HEAD_DIM)),
        'cache_moba_v': nrm(ks[7], (N_MOBA, n_phys, PAGE_SIZE, N_KV, HEAD_DIM)),
        'cache_mem_k': nrm(ks[8], (DEPTH, DEC_BATCH, MEM_LEN, MEM_HEADS, MEM_HEAD_DIM)),
        'cache_mem_v': nrm(ks[9], (DEPTH, DEC_BATCH, MEM_LEN, MEM_HEADS, MEM_HEAD_DIM)),
        'page_table': page_table,
        'mem_prompt': nrm(ks[10], (BATCH, MEM_LEN, D_MODEL)),
        'norm_mix': 1.0 + nrm(ks[11], (DEPTH, D_MODEL), 0.02),
        'norm_mem': 1.0 + nrm(ks[12], (DEPTH, D_MODEL), 0.02),
        'w_in_dsa': nrm(ks[13], (N_DSA, D_MODEL, DSA_IN), D_MODEL ** -0.5),
        'w_in_moba': nrm(ks[14], (N_MOBA, D_MODEL, MOBA_IN), D_MODEL ** -0.5),
        'w_mem_kv': nrm(ks[15], (DEPTH, D_MODEL, 2 * MEM_W), D_MODEL ** -0.5),
        'w_out': nrm(ks[16], (DEPTH, MIX_W, D_MODEL), MIX_W ** -0.5),
        'norm_ffn': 1.0 + nrm(ks[17], (DEPTH, D_MODEL), 0.02),
        'w_up': nrm(ks[18], (DEPTH, D_MODEL, D_FF), D_MODEL ** -0.5),
        'w_down': nrm(ks[19], (DEPTH, D_FF, D_MODEL), D_FF ** -0.5),
        'norm_final': 1.0 + nrm(ks[20], (D_MODEL,), 0.02),
    }


def reference(x_prompt, x_sample, cache_dsa_k, cache_dsa_v, cache_dsa_kidx, cache_moba_k, cache_moba_v, cache_mem_k, cache_mem_v, page_table, mem_prompt, norm_mix, norm_mem, w_in_dsa, w_in_moba, w_mem_kv, w_out, norm_ffn, w_up, w_down, norm_final):
    pos_p = jnp.arange(SEQ, dtype=jnp.int32)
    pos_s = PAST_LEN + jnp.arange(DEC_SEQ, dtype=jnp.int32)
    hp, hs = x_prompt, x_sample
    p_dsa_k, p_dsa_v, p_dsa_ki, s_dsa_k, s_dsa_v, s_dsa_ki = [], [], [], [], [], []
    p_moba_k, p_moba_v, s_moba_k, s_moba_v = [], [], [], []
    p_mem_k, p_mem_v = [], []
    for i in range(DEPTH):
        j = i // N_MIXERS
        mk_p, mv_p = memory_kv(mem_prompt, norm_mem[i], w_mem_kv[i])
        p_mem_k.append(mk_p)
        p_mem_v.append(mv_p)
        hn_p = rmsnorm(hp, norm_mix[i])
        hn_s = rmsnorm(hs, norm_mix[i])
        if i % N_MIXERS == 0:
            mix_p, qm_p, (k, v, ki) = dsa_mixer(hn_p, pos_p, w_in_dsa[j], None)
            p_dsa_k.append(k)
            p_dsa_v.append(v)
            p_dsa_ki.append(ki)
            past = (gather_pages(cache_dsa_k[j], page_table), gather_pages(cache_dsa_v[j], page_table), gather_pages(cache_dsa_kidx[j], page_table))
            mix_s, qm_s, (k, v, ki) = dsa_mixer(hn_s, pos_s, w_in_dsa[j], past)
            s_dsa_k.append(k)
            s_dsa_v.append(v)
            s_dsa_ki.append(ki)
        else:
            mix_p, qm_p, (k, v) = moba_mixer(hn_p, pos_p, w_in_moba[j], None)
            p_moba_k.append(k)
            p_moba_v.append(v)
            past = (gather_pages(cache_moba_k[j], page_table), gather_pages(cache_moba_v[j], page_table))
            mix_s, qm_s, (k, v) = moba_mixer(hn_s, pos_s, w_in_moba[j], past)
            s_moba_k.append(k)
            s_moba_v.append(v)
        hp = finish_layer(hp, mix_p, qm_p, mk_p, mv_p, w_out[i], norm_ffn[i], w_up[i], w_down[i])
        hs = finish_layer(hs, mix_s, qm_s, cache_mem_k[i], cache_mem_v[i], w_out[i], norm_ffn[i], w_up[i], w_down[i])
    y_prompt = rmsnorm(hp, norm_final)
    y_sample = rmsnorm(hs, norm_final)
    return (y_prompt, y_sample, jnp.stack(p_dsa_k), jnp.stack(p_dsa_v), jnp.stack(p_dsa_ki), jnp.stack(p_moba_k), jnp.stack(p_moba_v), jnp.stack(p_mem_k), jnp.stack(p_mem_v), jnp.stack(s_dsa_k), jnp.stack(s_dsa_v), jnp.stack(s_dsa_ki), jnp.stack(s_moba_k), jnp.stack(s_moba_v))
```

```python
import functools

import jax
import jax.numpy as jnp
from jax import lax
from jax.experimental import pallas as pl
from jax.experimental.pallas import tpu as pltpu

F32 = jnp.float32
BF16 = jnp.bfloat16
I32 = jnp.int32

HEAD_DIM = 128
GROUP = 3
ROT_DIM = 32
ROPE_THETA = 500000.0
IDX_DIM = 128
MEM_HEADS = 4
DSA_TOPK = 256
MOBA_BLOCK = 256
MOBA_TOPB = 3
EPS = 1e-6
NEG = -1e30
QTILE = 256
SAMPLE_TPAD = 8
VMEM_LIMIT = 56 * 1024 * 1024

_CONTRACT_LAST = (((1,), (1,)), ((), ()))


def _cparams(*sem):
    return pltpu.CompilerParams(dimension_semantics=sem, vmem_limit_bytes=VMEM_LIMIT)


def _pick_block(n, pref, align=128):
    if n <= pref:
        return n
    b = (pref // align) * align
    while b >= align:
        if n % b == 0:
            return b
        b -= align
    return n


def _rmsnorm_kernel(x_ref, g_ref, o_ref):
    x = x_ref[...]
    ms = jnp.mean(x * x, axis=-1, keepdims=True)
    o_ref[...] = ((x * lax.rsqrt(ms + EPS)) * g_ref[...]).astype(o_ref.dtype)


def _rmsnorm(x, g, out_dtype):
    m, d = x.shape
    bm = _pick_block(m, 256, 8)
    return pl.pallas_call(
        _rmsnorm_kernel,
        grid=(m // bm,),
        in_specs=[pl.BlockSpec((bm, d), lambda i: (i, 0)), pl.BlockSpec((1, d), lambda i: (0, 0))],
        out_specs=pl.BlockSpec((bm, d), lambda i: (i, 0)),
        out_shape=jax.ShapeDtypeStruct((m, d), out_dtype),
        compiler_params=_cparams("parallel"),
        name="rmsnorm",
    )(x, g.reshape(1, d).astype(F32))


def _mm_kernel(*refs, nk, epilogue):
    if epilogue == "residual":
        a_ref, w_ref, r_ref, o_ref, acc_ref = refs
    else:
        a_ref, w_ref, o_ref, acc_ref = refs
        r_ref = None
    k = pl.program_id(2)
    part = jnp.dot(a_ref[...], w_ref[...], preferred_element_type=F32)

    def finish(acc):
        if epilogue == "relu2":
            r = jnp.maximum(acc, 0.0)
            acc = r * r
        elif epilogue == "residual":
            acc = r_ref[...] + acc
        o_ref[...] = acc.astype(o_ref.dtype)

    if nk == 1:
        finish(part)
        return

    @pl.when(k == 0)
    def _():
        acc_ref[...] = part

    @pl.when(jnp.logical_and(k > 0, k < nk - 1))
    def _():
        acc_ref[...] += part

    @pl.when(k == nk - 1)
    def _():
        finish(acc_ref[...] + part)


def _matmul(a, w, out_dtype, epilogue=None, residual=None, bm_pref=512, bn_pref=1024, bk_pref=4096):
    m, kdim = a.shape
    _, n = w.shape
    bm = _pick_block(m, bm_pref, 8)
    bn = _pick_block(n, bn_pref)
    bk = _pick_block(kdim, bk_pref)
    nk = kdim // bk
    in_specs = [pl.BlockSpec((bm, bk), lambda j, i, k: (i, k)), pl.BlockSpec((bk, bn), lambda j, i, k: (k, j))]
    args = [a, w]
    if epilogue == "residual":
        in_specs.append(pl.BlockSpec((bm, bn), lambda j, i, k: (i, j)))
        args.append(residual)
    return pl.pallas_call(
        functools.partial(_mm_kernel, nk=nk, epilogue=epilogue),
        grid=(n // bn, m // bm, nk),
        in_specs=in_specs,
        out_specs=pl.BlockSpec((bm, bn), lambda j, i, k: (i, j)),
        out_shape=jax.ShapeDtypeStruct((m, n), out_dtype),
        scratch_shapes=[pltpu.VMEM((bm, bn), F32)],
        compiler_params=_cparams("parallel", "parallel", "arbitrary"),
        name="matmul",
    )(*args)


def _rope_tables(pos):
    half = ROT_DIM // 2
    inv = ROPE_THETA ** (-jnp.arange(half, dtype=F32) / half)
    ang = pos.astype(F32)[:, None] * inv[None, :]
    cos, sin = jnp.cos(ang), jnp.sin(ang)
    r = pos.shape[0]
    z16 = jnp.zeros((r, half), F32)
    zrest = jnp.zeros((r, HEAD_DIM - ROT_DIM), F32)
    c = jnp.concatenate([cos, cos, jnp.ones((r, HEAD_DIM - ROT_DIM), F32)], axis=1)
    sm = jnp.concatenate([-sin, z16, zrest], axis=1)
    sp = jnp.concatenate([z16, sin, zrest], axis=1)
    return c, sm, sp


def _rope_head(x, c, sm, sp):
    half = ROT_DIM // 2
    return x * c + pltpu.roll(x, HEAD_DIM - half, 1) * sm + pltpu.roll(x, half, 1) * sp


def _dsa_split_kernel(pa_ref, pb_ref, c_ref, sm_ref, sp_ref,
                      q_ref, kf_ref, kb_ref, vf_ref, vb_ref, qi_ref, kif_ref, kib_ref, wi_ref, qm_ref,
                      *, nh, nkv, ih, memw):
    c, sm, sp = c_ref[...], sm_ref[...], sp_ref[...]
    qw, kvw = nh * HEAD_DIM, nkv * HEAD_DIM
    for h in range(nh):
        sl = slice(h * HEAD_DIM, (h + 1) * HEAD_DIM)
        q_ref[:, sl] = _rope_head(pa_ref[:, sl], c, sm, sp).astype(q_ref.dtype)
    for h in range(nkv):
        sl = slice(h * HEAD_DIM, (h + 1) * HEAD_DIM)
        kr = _rope_head(pa_ref[:, qw + h * HEAD_DIM: qw + (h + 1) * HEAD_DIM], c, sm, sp)
        kf_ref[:, sl] = kr
        kb_ref[:, sl] = kr.astype(kb_ref.dtype)
    v = pa_ref[:, qw + kvw: qw + 2 * kvw]
    vf_ref[...] = v
    vb_ref[...] = v.astype(vb_ref.dtype)
    base = qw + 2 * kvw
    for h in range(ih):
        sl = slice(h * IDX_DIM, (h + 1) * IDX_DIM)
        qi_ref[:, sl] = _rope_head(pa_ref[:, base + h * IDX_DIM: base + (h + 1) * IDX_DIM], c, sm, sp).astype(qi_ref.dtype)
    ki = _rope_head(pb_ref[:, 0:IDX_DIM], c, sm, sp)
    kif_ref[...] = ki
    kib_ref[...] = ki.astype(kib_ref.dtype)
    qm_ref[...] = pb_ref[:, IDX_DIM:IDX_DIM + memw].astype(qm_ref.dtype)
    wi_ref[...] = pb_ref[:, IDX_DIM + memw:IDX_DIM + memw + 128] * ((ih * IDX_DIM) ** -0.5)


def _dsa_split(pa, pb, tabs, rows_per_seq, nh, nkv, ih, memw, act_dtype):
    m = pa.shape[0]
    bm = _pick_block(rows_per_seq, 128, 8)
    nt = rows_per_seq // bm
    qw, kvw = nh * HEAD_DIM, nkv * HEAD_DIM
    row = lambda w: pl.BlockSpec((bm, w), lambda i: (i, 0))
    tab = pl.BlockSpec((bm, HEAD_DIM), lambda i: (i % nt, 0))
    shapes = [(qw, act_dtype), (kvw, F32), (kvw, act_dtype), (kvw, F32), (kvw, act_dtype),
              (ih * IDX_DIM, act_dtype), (IDX_DIM, F32), (IDX_DIM, act_dtype), (128, F32), (memw, act_dtype)]
    return pl.pallas_call(
        functools.partial(_dsa_split_kernel, nh=nh, nkv=nkv, ih=ih, memw=memw),
        grid=(m // bm,),
        in_specs=[row(pa.shape[1]), row(pb.shape[1]), tab, tab, tab],
        out_specs=[row(w) for w, _ in shapes],
        out_shape=[jax.ShapeDtypeStruct((m, w), dt) for w, dt in shapes],
        compiler_params=_cparams("parallel"),
        name="dsa_split",
    )(pa, pb, *tabs)


def _moba_split_kernel(p_ref, c_ref, sm_ref, sp_ref, q_ref, kf_ref, kb_ref, vf_ref, vb_ref, qm_ref, km_ref,
                       *, nh, nkv, memw):
    c, sm, sp = c_ref[...], sm_ref[...], sp_ref[...]
    qw, kvw = nh * HEAD_DIM, nkv * HEAD_DIM
    for h in range(nh):
        sl = slice(h * HEAD_DIM, (h + 1) * HEAD_DIM)
        q_ref[:, sl] = _rope_head(p_ref[:, sl], c, sm, sp).astype(q_ref.dtype)
    rows = p_ref.shape[0]
    for h in range(nkv):
        sl = slice(h * HEAD_DIM, (h + 1) * HEAD_DIM)
        kr = _rope_head(p_ref[:, qw + h * HEAD_DIM: qw + (h + 1) * HEAD_DIM], c, sm, sp)
        kf_ref[:, sl] = kr
        kb_ref[:, sl] = kr.astype(kb_ref.dtype)
        km_ref[0, :, sl] = jnp.sum(kr, axis=0, keepdims=True) * (1.0 / rows)
    v = p_ref[:, qw + kvw: qw + 2 * kvw]
    vf_ref[...] = v
    vb_ref[...] = v.astype(vb_ref.dtype)
    qm_ref[...] = p_ref[:, qw + 2 * kvw: qw + 2 * kvw + memw].astype(qm_ref.dtype)


def _moba_split(p, tabs, rows_per_seq, nh, nkv, memw, act_dtype):
    m = p.shape[0]
    bm = _pick_block(rows_per_seq, MOBA_BLOCK, 8)
    nt = rows_per_seq // bm
    qw, kvw = nh * HEAD_DIM, nkv * HEAD_DIM
    row = lambda w: pl.BlockSpec((bm, w), lambda i: (i, 0))
    tab = pl.BlockSpec((bm, HEAD_DIM), lambda i: (i % nt, 0))
    shapes = [(qw, act_dtype), (kvw, F32), (kvw, act_dtype), (kvw, F32), (kvw, act_dtype), (memw, act_dtype)]
    return pl.pallas_call(
        functools.partial(_moba_split_kernel, nh=nh, nkv=nkv, memw=memw),
        grid=(m // bm,),
        in_specs=[row(p.shape[1]), tab, tab, tab],
        out_specs=[row(w) for w, _ in shapes] + [pl.BlockSpec((1, 1, kvw), lambda i: (i, 0, 0))],
        out_shape=[jax.ShapeDtypeStruct((m, w), dt) for w, dt in shapes]
        + [jax.ShapeDtypeStruct((m // bm, 1, kvw), F32)],
        compiler_params=_cparams("parallel"),
        name="moba_split",
    )(p, *tabs)


def _sortable_key(x):
    bits = pltpu.bitcast(x, I32)
    return jnp.where(bits < 0, bits ^ jnp.int32(0x7FFFFFFF), bits)


def _kth_largest_key(count_ge, rows, k):
    imin = jnp.int32(-2 ** 31)
    c0 = count_ge(jnp.zeros((rows, 1), I32))
    thr = jnp.where(c0 >= k, jnp.int32(0), imin)

    def bit_body(it, thr):
        cand = thr + jnp.left_shift(jnp.int32(1), jnp.int32(30) - it)
        return jnp.where(count_ge(cand) >= k, cand, thr)

    return lax.fori_loop(0, 31, bit_body, thr)


def _flash_step(carry, q, kb, vb, bias, scale):
    m, l, acc = carry
    s = lax.dot_general(q, kb, _CONTRACT_LAST, preferred_element_type=F32) * scale + bias
    m_new = jnp.maximum(m, jnp.max(s, axis=1, keepdims=True))
    alpha = jnp.exp(m - m_new)
    p = jnp.exp(s - m_new)
    l = alpha * l + jnp.sum(p, axis=1, keepdims=True)
    acc = alpha * acc + jnp.dot(p.astype(BF16), vb, preferred_element_type=F32)
    return m_new, l, acc


def _flash_init(rows):
    return (jnp.full((rows, 1), NEG, F32), jnp.zeros((rows, 1), F32), jnp.zeros((rows, HEAD_DIM), F32))


def _dsa_prompt_kernel(qi_ref, wi_ref, ki_ref, q_ref, k_ref, v_ref, o_ref, key_ref, bias_ref, *, topk, nkv, ih):
    i = pl.program_id(1)
    tq = ck = QTILE
    row = lax.broadcasted_iota(I32, (tq, ck), 0)
    col = lax.broadcasted_iota(I32, (tq, ck), 1)

    def causal_ok(kc):
        return col <= row + jnp.where(kc < i, ck, 0)

    def idx_body(kc, _):
        off = pl.multiple_of(kc * ck, ck)
        kic = ki_ref[0, pl.ds(off, ck), :]
        acc = jnp.zeros((tq, ck), F32)
        for h in range(ih):
            s = lax.dot_general(qi_ref[:, h * IDX_DIM:(h + 1) * IDX_DIM], kic, _CONTRACT_LAST,
                                preferred_element_type=F32)
            acc = acc + jnp.maximum(s, 0.0) * wi_ref[:, h:h + 1]
        key_ref[kc] = _sortable_key(jnp.where(causal_ok(kc), acc, -jnp.inf))
        return 0

    lax.fori_loop(0, i + 1, idx_body, 0)

    def count_ge(cand):
        def body(kc, acc):
            return acc + (key_ref[kc] >= cand).astype(F32)
        acc = lax.fori_loop(0, i + 1, body, jnp.zeros((tq, ck), F32))
        return jnp.sum(acc, axis=1, keepdims=True)

    thr = _kth_largest_key(count_ge, tq, topk)

    def bias_body(kc, _):
        sel = jnp.logical_and(key_ref[kc] >= thr, causal_ok(kc))
        bias_ref[kc] = jnp.where(sel, 0.0, NEG)
        return 0

    lax.fori_loop(0, i + 1, bias_body, 0)

    scale = HEAD_DIM ** -0.5
    for n in range(nkv):
        q3 = jnp.concatenate(
            [q_ref[:, (n * GROUP + g) * HEAD_DIM:(n * GROUP + g + 1) * HEAD_DIM] for g in range(GROUP)], axis=0)

        def att_body(kc, carry, n=n, q3=q3):
            off = pl.multiple_of(kc * ck, ck)
            kb = k_ref[0, pl.ds(off, ck), n * HEAD_DIM:(n + 1) * HEAD_DIM]
            vb = v_ref[0, pl.ds(off, ck), n * HEAD_DIM:(n + 1) * HEAD_DIM]
            b = bias_ref[kc]
            bias = jnp.concatenate([b] * GROUP, axis=0)
            return _flash_step(carry, q3, kb, vb, bias, scale)

        m, l, acc = lax.fori_loop(0, i + 1, att_body, _flash_init(GROUP * tq))
        out = acc / l
        for g in range(GROUP):
            o_ref[:, (n * GROUP + g) * HEAD_DIM:(n * GROUP + g + 1) * HEAD_DIM] = (
                out[g * tq:(g + 1) * tq].astype(o_ref.dtype))


def _dsa_prompt(qi, wi, ki, q, k, v, bsz, t, nkv, ih, topk):
    nt = t // QTILE
    qw = nkv * GROUP * HEAD_DIM
    kvw = nkv * HEAD_DIM
    rows = lambda w: pl.BlockSpec((QTILE, w), lambda b, i: (b * nt + i, 0))
    seq = lambda w: pl.BlockSpec((1, t, w), lambda b, i: (b, 0, 0))
    return pl.pallas_call(
        functools.partial(_dsa_prompt_kernel, topk=topk, nkv=nkv, ih=ih),
        grid=(bsz, nt),
        in_specs=[rows(ih * IDX_DIM), rows(128), seq(IDX_DIM), rows(qw), seq(kvw), seq(kvw)],
        out_specs=rows(qw),
        out_shape=jax.ShapeDtypeStruct((bsz * t, qw), BF16),
        scratch_shapes=[pltpu.VMEM((nt, QTILE, QTILE), I32), pltpu.VMEM((nt, QTILE, QTILE), F32)],
        compiler_params=_cparams("parallel", "arbitrary"),
        name="dsa_prompt_attn",
    )(qi, wi, ki.reshape(bsz, t, IDX_DIM), q, k.reshape(bsz, t, kvw), v.reshape(bsz, t, kvw))


def _topb_select_bias(gate, valid, blk_iota, nblk):
    g = jnp.where(valid, gate, -jnp.inf)
    rank = jnp.zeros(g.shape, I32)
    for m in range(nblk):
        gm = g[:, m:m + 1]
        beats = jnp.logical_or(gm > g, jnp.logical_and(gm == g, blk_iota > m))
        rank = rank + beats.astype(I32)
    sel = jnp.logical_and(valid, rank < MOBA_TOPB)
    return jnp.where(sel, 0.0, NEG)


def _moba_prompt_kernel(q_ref, k_ref, v_ref, km_ref, o_ref, sb_ref, *, nkv, nblk):
    j = pl.program_id(1)
    tq = blk = QTILE
    rows = GROUP * tq
    row = lax.broadcasted_iota(I32, (tq, blk), 0)
    col = lax.broadcasted_iota(I32, (tq, blk), 1)
    cb = jnp.where(col <= row, 0.0, NEG)
    causal_bias = jnp.concatenate([cb] * GROUP, axis=0)
    blk_iota = lax.broadcasted_iota(I32, (rows, nblk), 1)
    valid = blk_iota < j
    scale = HEAD_DIM ** -0.5
    own = pl.multiple_of(j * blk, blk)
    for n in range(nkv):
        lanes = slice(n * HEAD_DIM, (n + 1) * HEAD_DIM)
        q3 = jnp.concatenate(
            [q_ref[:, (n * GROUP + g) * HEAD_DIM:(n * GROUP + g + 1) * HEAD_DIM] for g in range(GROUP)], axis=0)
        km = km_ref[0, :, lanes].astype(BF16)
        gate = lax.dot_general(q3, km, _CONTRACT_LAST, preferred_element_type=F32)
        selb = _topb_select_bias(gate, valid, blk_iota, nblk)
        for m in range(nblk):
            sb_ref[m] = jnp.broadcast_to(selb[:, m:m + 1], (rows, HEAD_DIM))

        carry = _flash_step(_flash_init(rows), q3, k_ref[0, pl.ds(own, blk), lanes],
                            v_ref[0, pl.ds(own, blk), lanes], causal_bias, scale)

        def past_body(m, carry, lanes=lanes, q3=q3):
            off = pl.multiple_of(m * blk, blk)
            b = sb_ref[m]
            bias = jnp.concatenate([b] * (blk // HEAD_DIM), axis=1)
            return _flash_step(carry, q3, k_ref[0, pl.ds(off, blk), lanes], v_ref[0, pl.ds(off, blk), lanes],
                               bias, scale)

        m_, l, acc = lax.fori_loop(0, j, past_body, carry)
        out = acc / l
        for g in range(GROUP):
            o_ref[:, (n * GROUP + g) * HEAD_DIM:(n * GROUP + g + 1) * HEAD_DIM] = (
                out[g * tq:(g + 1) * tq].astype(o_ref.dtype))


def _moba_prompt(q, k, v, kmean, bsz, t, nkv):
    nt = t // QTILE
    qw = nkv * GROUP * HEAD_DIM
    kvw = nkv * HEAD_DIM
    rows = lambda w: pl.BlockSpec((QTILE, w), lambda b, i: (b * nt + i, 0))
    seq = lambda w: pl.BlockSpec((1, t, w), lambda b, i: (b, 0, 0))
    return pl.pallas_call(
        functools.partial(_moba_prompt_kernel, nkv=nkv, nblk=nt),
        grid=(bsz, nt),
        in_specs=[rows(qw), seq(kvw), seq(kvw), pl.BlockSpec((1, nt, kvw), lambda b, i: (b, 0, 0))],
        out_specs=rows(qw),
        out_shape=jax.ShapeDtypeStruct((bsz * t, qw), BF16),
        scratch_shapes=[pltpu.VMEM((nt, GROUP * QTILE, HEAD_DIM), F32)],
        compiler_params=_cparams("parallel", "arbitrary"),
        name="moba_prompt_attn",
    )(q, k.reshape(bsz, t, kvw), v.reshape(bsz, t, kvw), kmean.reshape(bsz, nt, kvw))


def _mem_attn_kernel(q_ref, mk_ref, mv_ref, o_ref, *, hd):
    scale = hd ** -0.5
    for c in range(MEM_HEADS):
        lanes = slice(c * hd, (c + 1) * hd)
        q = q_ref[0, :, lanes].astype(BF16)
        mk = mk_ref[0, :, lanes].astype(BF16)
        mv = mv_ref[0, :, lanes].astype(BF16)
        s = lax.dot_general(q, mk, _CONTRACT_LAST, preferred_element_type=F32) * scale
        p = jnp.exp(s - jnp.max(s, axis=1, keepdims=True))
        l = jnp.sum(p, axis=1, keepdims=True)
        o = jnp.dot(p.astype(BF16), mv, preferred_element_type=F32) / l
        o_ref[0, :, lanes] = o.astype(o_ref.dtype)


def _mem_attn(qm, mk, mv, out_dtype):
    bsz, t, w = qm.shape
    mlen = mk.shape[1]
    tq = _pick_block(t, 512, 8)
    return pl.pallas_call(
        functools.partial(_mem_attn_kernel, hd=w // MEM_HEADS),
        grid=(bsz, t // tq),
        in_specs=[pl.BlockSpec((1, tq, w), lambda b, i: (b, i, 0)),
                  pl.BlockSpec((1, mlen, w), lambda b, i: (b, 0, 0)),
                  pl.BlockSpec((1, mlen, w), lambda b, i: (b, 0, 0))],
        out_specs=pl.BlockSpec((1, tq, w), lambda b, i: (b, i, 0)),
        out_shape=jax.ShapeDtypeStruct((bsz, t, w), out_dtype),
        compiler_params=_cparams("parallel", "parallel"),
        name="mem_attn",
    )(qm, mk, mv)


SAMPLE_PP = 4


def _page_specs(shape_tail, pp):
    nd = len(shape_tail)
    return [pl.BlockSpec((None,) + shape_tail, lambda b, p, pt, c=c: (pt[b, p * pp + c],) + (0,) * nd)
            for c in range(pp)]


def _head_sum(w, ih):
    acc = w[0:SAMPLE_TPAD]
    for h in range(1, ih):
        acc = acc + w[h * SAMPLE_TPAD:(h + 1) * SAMPLE_TPAD]
    return acc


def _dsa_sample_index_kernel(pt_ref, qi_ref, wi_ref, *refs, pp, ih, page):
    page_refs, o_ref = refs[:pp], refs[pp]
    qi = qi_ref[0].astype(BF16)
    wi = wi_ref[0]
    for c in range(pp):
        kp = page_refs[c][...].astype(BF16)
        s = lax.dot_general(qi, kp, _CONTRACT_LAST, preferred_element_type=F32)
        o_ref[0, :, c * page:(c + 1) * page] = _head_sum(jnp.maximum(s, 0.0) * wi, ih)


def _dsa_sample_index(page_table, qi_ht, wi_ht, kidx_pool, ih):
    bs, npg = page_table.shape
    page = kidx_pool.shape[1]
    pp = SAMPLE_PP
    r = ih * SAMPLE_TPAD
    grid_spec = pltpu.PrefetchScalarGridSpec(
        num_scalar_prefetch=1,
        grid=(bs, npg // pp),
        in_specs=[pl.BlockSpec((1, r, IDX_DIM), lambda b, p, pt: (b, 0, 0)),
                  pl.BlockSpec((1, r, page), lambda b, p, pt: (b, 0, 0))]
        + _page_specs((page, IDX_DIM), pp),
        out_specs=pl.BlockSpec((1, SAMPLE_TPAD, page * pp), lambda b, p, pt: (b, 0, p)),
    )
    return pl.pallas_call(
        functools.partial(_dsa_sample_index_kernel, pp=pp, ih=ih, page=page),
        grid_spec=grid_spec,
        out_shape=jax.ShapeDtypeStruct((bs, SAMPLE_TPAD, npg * page), F32),
        compiler_params=_cparams("parallel", "arbitrary"),
        name="dsa_sample_index",
    )(page_table, qi_ht, wi_ht, *([kidx_pool] * pp))


def _dsa_sample_select_kernel(sc_ref, qi_ref, wi_ref, kin_ref, o_ref, *, topk, ih, ts, past):
    tail_w = kin_ref.shape[1]
    s = lax.dot_general(qi_ref[0].astype(BF16), kin_ref[0].astype(BF16), _CONTRACT_LAST,
                        preferred_element_type=F32)
    tail = _head_sum(jnp.maximum(s, 0.0) * wi_ref[0][:, :tail_w], ih)
    row = lax.broadcasted_iota(I32, (SAMPLE_TPAD, tail_w), 0)
    col = lax.broadcasted_iota(I32, (SAMPLE_TPAD, tail_w), 1)
    tail_ok = jnp.logical_and(col <= row, col < ts)
    full = jnp.concatenate([sc_ref[0], jnp.where(tail_ok, tail, -jnp.inf)], axis=1)
    keys = _sortable_key(full)
    width = past + tail_w
    col_f = lax.broadcasted_iota(I32, (SAMPLE_TPAD, width), 1)
    row_f = lax.broadcasted_iota(I32, (SAMPLE_TPAD, width), 0)
    visible = jnp.logical_or(col_f < past, jnp.logical_and(col_f - past <= row_f, col_f - past < ts))

    def count_ge(cand):
        return jnp.sum((keys >= cand).astype(F32), axis=1, keepdims=True)

    thr = _kth_largest_key(count_ge, SAMPLE_TPAD, topk)
    o_ref[0] = jnp.where(jnp.logical_and(keys >= thr, visible), 0.0, NEG)


def _dsa_sample_select(scores, qi_ht, wi_ht, ki_new, ih, ts, topk):
    bs, _, past = scores.shape
    tail_w = ki_new.shape[1]
    r = ih * SAMPLE_TPAD
    return pl.pallas_call(
        functools.partial(_dsa_sample_select_kernel, topk=topk, ih=ih, ts=ts, past=past),
        grid=(bs,),
        in_specs=[pl.BlockSpec((1, SAMPLE_TPAD, past), lambda b: (b, 0, 0)),
                  pl.BlockSpec((1, r, IDX_DIM), lambda b: (b, 0, 0)),
                  pl.BlockSpec((1, r, wi_ht.shape[2]), lambda b: (b, 0, 0)),
                  pl.BlockSpec((1, tail_w, IDX_DIM), lambda b: (b, 0, 0))],
        out_specs=pl.BlockSpec((1, SAMPLE_TPAD, past + tail_w), lambda b: (b, 0, 0)),
        out_shape=jax.ShapeDtypeStruct((bs, SAMPLE_TPAD, past + tail_w), F32),
        compiler_params=_cparams("parallel"),
        name="dsa_sample_select",
    )(scores, qi_ht, wi_ht, ki_new)


def _sample_flash_block(q_ref, kblk, vblk, bias_fn, m_ref, l_ref, acc_ref, nkv):
    scale = HEAD_DIM ** -0.5
    for n in range(nkv):
        lanes = slice(n * HEAD_DIM, (n + 1) * HEAD_DIM)
        carry = (m_ref[n][:, 0:1], l_ref[n][:, 0:1], acc_ref[n])
        m, l, acc = _flash_step(carry, q_ref[0, n].astype(BF16), kblk[:, lanes].astype(BF16),
                                vblk[:, lanes].astype(BF16), bias_fn(n), scale)
        m_ref[n] = jnp.broadcast_to(m, m_ref.shape[1:])
        l_ref[n] = jnp.broadcast_to(l, l_ref.shape[1:])
        acc_ref[n] = acc


def _sample_flash_reset(m_ref, l_ref, acc_ref):
    m_ref[...] = jnp.full(m_ref.shape, NEG, F32)
    l_ref[...] = jnp.zeros(l_ref.shape, F32)
    acc_ref[...] = jnp.zeros(acc_ref.shape, F32)


def _dsa_sample_attn_kernel(pt_ref, q_ref, bias_ref, tbias_ref, kn_ref, vn_ref, *refs, pp, nkv, page):
    k_refs, v_refs = refs[:pp], refs[pp:2 * pp]
    o_ref, m_ref, l_ref, acc_ref = refs[2 * pp:]
    p = pl.program_id(1)

    @pl.when(p == 0)
    def _():
        _sample_flash_reset(m_ref, l_ref, acc_ref)

    for c in range(pp):
        b8 = bias_ref[0, :, c * page:(c + 1) * page]
        bias = jnp.concatenate([b8] * GROUP, axis=0)
        _sample_flash_block(q_ref, k_refs[c][...], v_refs[c][...], lambda n, bias=bias: bias,
                            m_ref, l_ref, acc_ref, nkv)

    @pl.when(p == pl.num_programs(1) - 1)
    def _():
        tb = jnp.concatenate([tbias_ref[0]] * GROUP, axis=0)
        _sample_flash_block(q_ref, kn_ref[0], vn_ref[0], lambda n: tb, m_ref, l_ref, acc_ref, nkv)
        o_ref[0] = acc_ref[...] / l_ref[...]


def _dsa_sample_attn(page_table, q_s, bias, k_new, v_new, k_pool, v_pool, nkv):
    bs, npg = page_table.shape
    page = k_pool.shape[1]
    kvw = nkv * HEAD_DIM
    pp = SAMPLE_PP
    rows = GROUP * SAMPLE_TPAD
    tail_w = k_new.shape[1]
    grid_spec = pltpu.PrefetchScalarGridSpec(
        num_scalar_prefetch=1,
        grid=(bs, npg // pp),
        in_specs=[pl.BlockSpec((1, nkv, rows, HEAD_DIM), lambda b, p, pt: (b, 0, 0, 0)),
                  pl.BlockSpec((1, SAMPLE_TPAD, page * pp), lambda b, p, pt: (b, 0, p)),
                  pl.BlockSpec((1, SAMPLE_TPAD, tail_w), lambda b, p, pt: (b, 0, (npg * page) // tail_w)),
                  pl.BlockSpec((1, tail_w, kvw), lambda b, p, pt: (b, 0, 0)),
                  pl.BlockSpec((1, tail_w, kvw), lambda b, p, pt: (b, 0, 0))]
        + _page_specs((page, kvw), pp) + _page_specs((page, kvw), pp),
        out_specs=pl.BlockSpec((1, nkv, rows, HEAD_DIM), lambda b, p, pt: (b, 0, 0, 0)),
        scratch_shapes=[pltpu.VMEM((nkv, rows, HEAD_DIM), F32)] * 3,
    )
    return pl.pallas_call(
        functools.partial(_dsa_sample_attn_kernel, pp=pp, nkv=nkv, page=page),
        grid_spec=grid_spec,
        out_shape=jax.ShapeDtypeStruct((bs, nkv, rows, HEAD_DIM), F32),
        compiler_params=_cparams("parallel", "arbitrary"),
        name="dsa_sample_attn",
    )(page_table, q_s, bias, bias, k_new, v_new, *([k_pool] * pp), *([v_pool] * pp))


def _moba_kmean_kernel(pt_ref, *refs, ppb):
    k_refs, o_ref = refs[:ppb], refs[ppb]
    acc = jnp.sum(k_refs[0][...], axis=0, keepdims=True)
    for c in range(1, ppb):
        acc = acc + jnp.sum(k_refs[c][...], axis=0, keepdims=True)
    o_ref[0] = acc * (1.0 / MOBA_BLOCK)


def _moba_sample_kmean(page_table, k_pool, nkv):
    bs, npg = page_table.shape
    page = k_pool.shape[1]
    ppb = MOBA_BLOCK // page
    nb = npg // ppb
    kvw = nkv * HEAD_DIM
    grid_spec = pltpu.PrefetchScalarGridSpec(
        num_scalar_prefetch=1,
        grid=(bs, nb),
        in_specs=_page_specs((page, kvw), ppb),
        out_specs=pl.BlockSpec((1, 1, kvw), lambda b, p, pt: (b * nb + p, 0, 0)),
    )
    out = pl.pallas_call(
        functools.partial(_moba_kmean_kernel, ppb=ppb),
        grid_spec=grid_spec,
        out_shape=jax.ShapeDtypeStruct((bs * nb, 1, kvw), F32),
        compiler_params=_cparams("parallel", "arbitrary"),
        name="moba_sample_kmean",
    )(page_table, *([k_pool] * ppb))
    return out.reshape(bs, nb, kvw)


def _moba_sample_attn_kernel(pt_ref, q_ref, km_ref, kn_ref, vn_ref, *refs, pp, nkv, page, nb, ts):
    k_refs, v_refs = refs[:pp], refs[pp:2 * pp]
    o_ref, m_ref, l_ref, acc_ref, sb_ref = refs[2 * pp:]
    p = pl.program_id(1)
    rows = GROUP * SAMPLE_TPAD
    ppb = MOBA_BLOCK // page
    blk_iota = lax.broadcasted_iota(I32, (rows, nb), 1)

    @pl.when(p == 0)
    def _():
        _sample_flash_reset(m_ref, l_ref, acc_ref)
        valid = blk_iota >= 0
        for n in range(nkv):
            km = km_ref[0, :, n * HEAD_DIM:(n + 1) * HEAD_DIM].astype(BF16)
            gate = lax.dot_general(q_ref[0, n].astype(BF16), km, _CONTRACT_LAST, preferred_element_type=F32)
            sb_ref[n] = _topb_select_bias(gate, valid, blk_iota, nb)

    for c in range(pp):
        blk = p * (pp // ppb) + c // ppb

        def bias_fn(n, blk=blk):
            col = jnp.sum(jnp.where(blk_iota == blk, sb_ref[n], 0.0), axis=1, keepdims=True)
            return col

        _sample_flash_block(q_ref, k_refs[c][...], v_refs[c][...], bias_fn, m_ref, l_ref, acc_ref, nkv)

    @pl.when(p == pl.num_programs(1) - 1)
    def _():
        tail_w = kn_ref.shape[1]
        row = lax.broadcasted_iota(I32, (rows, tail_w), 0) & (SAMPLE_TPAD - 1)
        col = lax.broadcasted_iota(I32, (rows, tail_w), 1)
        tb = jnp.where(jnp.logical_and(col <= row, col < ts), 0.0, NEG)
        _sample_flash_block(q_ref, kn_ref[0], vn_ref[0], lambda n: tb, m_ref, l_ref, acc_ref, nkv)
        o_ref[0] = acc_ref[...] / l_ref[...]


def _moba_sample_attn(page_table, q_s, kmean, k_new, v_new, k_pool, v_pool, nkv, ts):
    bs, npg = page_table.shape
    page = k_pool.shape[1]
    kvw = nkv * HEAD_DIM
    pp = SAMPLE_PP
    rows = GROUP * SAMPLE_TPAD
    nb = kmean.shape[1]
    tail_w = k_new.shape[1]
    grid_spec = pltpu.PrefetchScalarGridSpec(
        num_scalar_prefetch=1,
        grid=(bs, npg // pp),
        in_specs=[pl.BlockSpec((1, nkv, rows, HEAD_DIM), lambda b, p, pt: (b, 0, 0, 0)),
                  pl.BlockSpec((1, nb, kvw), lambda b, p, pt: (b, 0, 0)),
                  pl.BlockSpec((1, tail_w, kvw), lambda b, p, pt: (b, 0, 0)),
                  pl.BlockSpec((1, tail_w, kvw), lambda b, p, pt: (b, 0, 0))]
        + _page_specs((page, kvw), pp) + _page_specs((page, kvw), pp),
        out_specs=pl.BlockSpec((1, nkv, rows, HEAD_DIM), lambda b, p, pt: (b, 0, 0, 0)),
        scratch_shapes=[pltpu.VMEM((nkv, rows, HEAD_DIM), F32)] * 3 + [pltpu.VMEM((nkv, rows, nb), F32)],
    )
    return pl.pallas_call(
        functools.partial(_moba_sample_attn_kernel, pp=pp, nkv=nkv, page=page, nb=nb, ts=ts),
        grid_spec=grid_spec,
        out_shape=jax.ShapeDtypeStruct((bs, nkv, rows, HEAD_DIM), F32),
        compiler_params=_cparams("parallel", "arbitrary"),
        name="moba_sample_attn",
    )(page_table, q_s, kmean, k_new, v_new, *([k_pool] * pp), *([v_pool] * pp))


def _heads_to_kv_major(q, bs, nkv):
    x = q.reshape(bs, SAMPLE_TPAD, nkv, GROUP, HEAD_DIM)
    return x.transpose(0, 2, 3, 1, 4).reshape(bs, nkv, GROUP * SAMPLE_TPAD, HEAD_DIM)


def _kv_major_to_rows(o, bs, nkv):
    x = o.reshape(bs, nkv, GROUP, SAMPLE_TPAD, HEAD_DIM)
    return x.transpose(0, 3, 1, 2, 4).reshape(bs * SAMPLE_TPAD, nkv * GROUP * HEAD_DIM)


def _pad_rows(x, n):
    return jnp.pad(x, ((0, 0), (0, n - x.shape[1]), (0, 0)))


def _finish_layer(h, mix, qm, mk, mv, bsz, w_o, g_ffn, w_up, w_down):
    m = h.shape[0]
    ma = _mem_attn(qm.reshape(bsz, m // bsz, -1), mk, mv, qm.dtype).reshape(m, -1)
    merged = jnp.concatenate([mix.astype(BF16), ma.astype(BF16)], axis=1)
    h = _matmul(merged, w_o, F32, epilogue="residual", residual=h)
    u = _matmul(_rmsnorm(h, g_ffn, BF16), w_up, BF16, epilogue="relu2")
    return _matmul(u, w_down, F32, epilogue="residual", residual=h)


def kernel(x_prompt, x_sample, cache_dsa_k, cache_dsa_v, cache_dsa_kidx, cache_moba_k, cache_moba_v, cache_mem_k, cache_mem_v, page_table, mem_prompt, norm_mix, norm_mem, w_in_dsa, w_in_moba, w_mem_kv, w_out, norm_ffn, w_up, w_down, norm_final):
    bsz, t, d = x_prompt.shape
    bs, ts, _ = x_sample.shape
    depth = norm_mix.shape[0]
    nh = (3 * d) // (4 * HEAD_DIM)
    nkv = nh // GROUP
    qw, kvw = nh * HEAD_DIM, nkv * HEAD_DIM
    memw = d // 4
    mhd = memw // MEM_HEADS
    ih = d // 128
    npg = page_table.shape[1]
    page = cache_dsa_k.shape[2]
    past = npg * page
    mlen = mem_prompt.shape[1]
    tp = SAMPLE_TPAD

    hp = x_prompt.reshape(bsz * t, d)
    hs = jnp.pad(x_sample, ((0, 0), (0, tp - ts), (0, 0))).reshape(bs * tp, d)
    pos_p = jnp.arange(t, dtype=I32)
    pos_s = past + (jnp.arange(tp, dtype=I32) % ts)
    tabs_p = _rope_tables(pos_p)
    tabs_s = _rope_tables(jnp.tile(pos_s, bs))

    outs = {name: [] for name in ("pdk", "pdv", "pdki", "sdk", "sdv", "sdki", "pmk", "pmv", "smk", "smv", "mk", "mv")}
    mem_flat = mem_prompt.reshape(bsz * mlen, d)

    for i in range(depth):
        j = i // 2
        w_o = w_out[i].astype(BF16)
        wu = w_up[i].astype(BF16)
        wd = w_down[i].astype(BF16)
        mkv = _matmul(_rmsnorm(mem_flat, norm_mem[i], BF16), w_mem_kv[i].astype(BF16), F32)
        mk_p = mkv[:, :memw].reshape(bsz, mlen, memw)
        mv_p = mkv[:, memw:].reshape(bsz, mlen, memw)
        outs["mk"].append(mk_p.reshape(bsz, mlen, MEM_HEADS, mhd))
        outs["mv"].append(mv_p.reshape(bsz, mlen, MEM_HEADS, mhd))
        hn_p = _rmsnorm(hp, norm_mix[i], BF16)
        hn_s = _rmsnorm(hs, norm_mix[i], BF16)
        mk_s = cache_mem_k[i].reshape(bs, mlen, memw)
        mv_s = cache_mem_v[i].reshape(bs, mlen, memw)
        if i % 2 == 0:
            w = w_in_dsa[j]
            c0 = qw + 2 * kvw + ih * IDX_DIM
            wa = w[:, :c0].astype(BF16)
            wb = jnp.concatenate([w[:, c0 + ih:c0 + ih + IDX_DIM], w[:, c0 + ih + IDX_DIM:], w[:, c0:c0 + ih],
                                  jnp.zeros((d, 128 - ih), F32)], axis=1).astype(BF16)
            pa = _matmul(hn_p, wa, F32)
            pb = _matmul(hn_p, wb, F32)
            q, kf, kb, vf, vb, qi, kif, kib, wi, qm = _dsa_split(pa, pb, tabs_p, t, nh, nkv, ih, memw, BF16)
            outs["pdk"].append(kf.reshape(bsz, t, nkv, HEAD_DIM))
            outs["pdv"].append(vf.reshape(bsz, t, nkv, HEAD_DIM))
            outs["pdki"].append(kif.reshape(bsz, t, IDX_DIM))
            mix_p = _dsa_prompt(qi, wi, kib, q, kb, vb, bsz, t, nkv, ih, min(DSA_TOPK, t // 4))
            pa = _matmul(hn_s, wa, F32)
            pb = _matmul(hn_s, wb, F32)
            q_s, kf, _, vf, _, qi_s, kif, _, wi_s, qm_s = _dsa_split(pa, pb, tabs_s, bs * tp, nh, nkv, ih, memw, F32)
            outs["sdk"].append(kf.reshape(bs, tp, nkv, HEAD_DIM)[:, :ts])
            outs["sdv"].append(vf.reshape(bs, tp, nkv, HEAD_DIM)[:, :ts])
            outs["sdki"].append(kif.reshape(bs, tp, IDX_DIM)[:, :ts])
            qi_ht = qi_s.reshape(bs, tp, ih, IDX_DIM).transpose(0, 2, 1, 3).reshape(bs, ih * tp, IDX_DIM)
            wi_ht = wi_s.reshape(bs, tp, 128)[:, :, :ih].transpose(0, 2, 1).reshape(bs, ih * tp, 1)
            wi_ht = jnp.broadcast_to(wi_ht, (bs, ih * tp, page))
            scores = _dsa_sample_index(page_table, qi_ht, wi_ht, cache_dsa_kidx[j], ih)
            ki_new = _pad_rows(kif.reshape(bs, tp, IDX_DIM), page)
            bias = _dsa_sample_select(scores, qi_ht, wi_ht, ki_new, ih, ts, min(DSA_TOPK, (past + ts) // 4))
            k_new = _pad_rows(kf.reshape(bs, tp, kvw), page)
            v_new = _pad_rows(vf.reshape(bs, tp, kvw), page)
            o_s = _dsa_sample_attn(page_table, _heads_to_kv_major(q_s, bs, nkv), bias, k_new, v_new,
                                   cache_dsa_k[j].reshape(-1, page, kvw), cache_dsa_v[j].reshape(-1, page, kvw), nkv)
            mix_s = _kv_major_to_rows(o_s, bs, nkv)
        else:
            w = w_in_moba[j].astype(BF16)
            p = _matmul(hn_p, w, F32)
            q, kf, kb, vf, vb, qm, kmean = _moba_split(p, tabs_p, t, nh, nkv, memw, BF16)
            outs["pmk"].append(kf.reshape(bsz, t, nkv, HEAD_DIM))
            outs["pmv"].append(vf.reshape(bsz, t, nkv, HEAD_DIM))
            mix_p = _moba_prompt(q, kb, vb, kmean, bsz, t, nkv)
            p = _matmul(hn_s, w, F32)
            q_s, kf, _, vf, _, qm_s, _ = _moba_split(p, tabs_s, bs * tp, nh, nkv, memw, F32)
            outs["smk"].append(kf.reshape(bs, tp, nkv, HEAD_DIM)[:, :ts])
            outs["smv"].append(vf.reshape(bs, tp, nkv, HEAD_DIM)[:, :ts])
            k_pool = cache_moba_k[j].reshape(-1, page, kvw)
            v_pool = cache_moba_v[j].reshape(-1, page, kvw)
            kmean_s = _moba_sample_kmean(page_table, k_pool, nkv)
            k_new = _pad_rows(kf.reshape(bs, tp, kvw), page)
            v_new = _pad_rows(vf.reshape(bs, tp, kvw), page)
            o_s = _moba_sample_attn(page_table, _heads_to_kv_major(q_s, bs, nkv), kmean_s, k_new, v_new,
                                    k_pool, v_pool, nkv, ts)
            mix_s = _kv_major_to_rows(o_s, bs, nkv)
        hp = _finish_layer(hp, mix_p, qm, mk_p, mv_p, bsz, w_o, norm_ffn[i], wu, wd)
        hs = _finish_layer(hs, mix_s, qm_s, mk_s, mv_s, bs, w_o, norm_ffn[i], wu, wd)

    y_prompt = _rmsnorm(hp, norm_final, F32).reshape(bsz, t, d)
    y_sample = _rmsnorm(hs, norm_final, F32).reshape(bs, tp, d)[:, :ts]
    st = jnp.stack
    return (y_prompt, y_sample, st(outs["pdk"]), st(outs["pdv"]), st(outs["pdki"]), st(outs["pmk"]), st(outs["pmv"]),
            st(outs["mk"]), st(outs["mv"]), st(outs["sdk"]), st(outs["sdv"]), st(outs["sdki"]),
            st(outs["smk"]), st(outs["smv"]))
```

```python
import functools

import jax
import jax.numpy as jnp
from jax import lax
from jax.experimental import pallas as pl
from jax.experimental.pallas import tpu as pltpu

F32 = jnp.float32
BF16 = jnp.bfloat16
I32 = jnp.int32

HEAD_DIM = 128
GROUP = 3
ROT_DIM = 32
ROPE_THETA = 500000.0
IDX_DIM = 128
MEM_HEADS = 4
DSA_TOPK = 256
MOBA_BLOCK = 256
MOBA_TOPB = 3
EPS = 1e-6
NEG = -1e30
LOG2E = 1.4426950408889634
SOFTMAX_C2 = (HEAD_DIM ** -0.5) * LOG2E
QTILE = 256
SAMPLE_TPAD = 8
VMEM_LIMIT = 56 * 1024 * 1024

_CONTRACT_LAST = (((1,), (1,)), ((), ()))


def _cparams(*sem):
    return pltpu.CompilerParams(dimension_semantics=sem, vmem_limit_bytes=VMEM_LIMIT)


def _pick_block(n, pref, align=128):
    if n <= pref:
        return n
    b = (pref // align) * align
    while b >= align:
        if n % b == 0:
            return b
        b -= align
    return n


def _rmsnorm_kernel(x_ref, g_ref, o_ref):
    x = x_ref[...]
    ms = jnp.mean(x * x, axis=-1, keepdims=True)
    o_ref[...] = ((x * lax.rsqrt(ms + EPS)) * g_ref[...]).astype(o_ref.dtype)


def _rmsnorm(x, g, out_dtype):
    m, d = x.shape
    bm = _pick_block(m, 256, 8)
    return pl.pallas_call(
        _rmsnorm_kernel,
        grid=(m // bm,),
        in_specs=[pl.BlockSpec((bm, d), lambda i: (i, 0)), pl.BlockSpec((1, d), lambda i: (0, 0))],
        out_specs=pl.BlockSpec((bm, d), lambda i: (i, 0)),
        out_shape=jax.ShapeDtypeStruct((m, d), out_dtype),
        compiler_params=_cparams("parallel"),
        name="rmsnorm",
    )(x, g.reshape(1, d).astype(F32))


def _mm_kernel(*refs, nk, epilogue):
    if epilogue == "residual":
        a_ref, w_ref, r_ref, o_ref, acc_ref = refs
    else:
        a_ref, w_ref, o_ref, acc_ref = refs
        r_ref = None
    k = pl.program_id(2)
    part = jnp.dot(a_ref[...], w_ref[...], preferred_element_type=F32)

    def finish(acc):
        if epilogue == "relu2":
            r = jnp.maximum(acc, 0.0)
            acc = r * r
        elif epilogue == "residual":
            acc = r_ref[...] + acc
        o_ref[...] = acc.astype(o_ref.dtype)

    if nk == 1:
        finish(part)
        return

    @pl.when(k == 0)
    def _():
        acc_ref[...] = part

    @pl.when(jnp.logical_and(k > 0, k < nk - 1))
    def _():
        acc_ref[...] += part

    @pl.when(k == nk - 1)
    def _():
        finish(acc_ref[...] + part)


def _matmul(a, w, layer, out_dtype, epilogue=None, residual=None, n=None, bm_pref=512, bn_pref=1024, bk_pref=4096):
    m, kdim = a.shape
    n = w.shape[2] if n is None else n
    bm = _pick_block(m, bm_pref, 8)
    bn = _pick_block(n, bn_pref)
    bk = _pick_block(kdim, bk_pref)
    nk = kdim // bk
    in_specs = [pl.BlockSpec((bm, bk), lambda j, i, k: (i, k)),
                pl.BlockSpec((None, bk, bn), lambda j, i, k: (layer, k, j))]
    args = [a, w]
    if epilogue == "residual":
        in_specs.append(pl.BlockSpec((bm, bn), lambda j, i, k: (i, j)))
        args.append(residual)
    return pl.pallas_call(
        functools.partial(_mm_kernel, nk=nk, epilogue=epilogue),
        grid=(n // bn, m // bm, nk),
        in_specs=in_specs,
        out_specs=pl.BlockSpec((bm, bn), lambda j, i, k: (i, j)),
        out_shape=jax.ShapeDtypeStruct((m, n), out_dtype),
        scratch_shapes=[pltpu.VMEM((bm, bn), F32)],
        compiler_params=_cparams("parallel", "parallel", "arbitrary"),
        name="matmul",
    )(*args)


def _rope_tables(pos):
    half = ROT_DIM // 2
    inv = ROPE_THETA ** (-jnp.arange(half, dtype=F32) / half)
    ang = pos.astype(F32)[:, None] * inv[None, :]
    cos, sin = jnp.cos(ang), jnp.sin(ang)
    r = pos.shape[0]
    z16 = jnp.zeros((r, half), F32)
    zrest = jnp.zeros((r, HEAD_DIM - ROT_DIM), F32)
    c = jnp.concatenate([cos, cos, jnp.ones((r, HEAD_DIM - ROT_DIM), F32)], axis=1)
    sm = jnp.concatenate([-sin, z16, zrest], axis=1)
    sp = jnp.concatenate([z16, sin, zrest], axis=1)
    return c, sm, sp


def _rope_head(x, c, sm, sp):
    half = ROT_DIM // 2
    return x * c + pltpu.roll(x, HEAD_DIM - half, 1) * sm + pltpu.roll(x, half, 1) * sp


def _dsa_split_kernel(pa_ref, pb_ref, c_ref, sm_ref, sp_ref,
                      q_ref, kf_ref, kb_ref, vf_ref, vb_ref, qi_ref, kif_ref, kib_ref, wi_ref, qm_ref,
                      *, nh, nkv, ih, memw):
    c, sm, sp = c_ref[...], sm_ref[...], sp_ref[...]
    qw, kvw = nh * HEAD_DIM, nkv * HEAD_DIM
    for h in range(nh):
        sl = slice(h * HEAD_DIM, (h + 1) * HEAD_DIM)
        q_ref[:, sl] = _rope_head(pa_ref[:, sl], c, sm, sp).astype(q_ref.dtype)
    for h in range(nkv):
        sl = slice(h * HEAD_DIM, (h + 1) * HEAD_DIM)
        kr = _rope_head(pa_ref[:, qw + h * HEAD_DIM: qw + (h + 1) * HEAD_DIM], c, sm, sp)
        kf_ref[:, sl] = kr
        kb_ref[:, sl] = kr.astype(kb_ref.dtype)
    v = pa_ref[:, qw + kvw: qw + 2 * kvw]
    vf_ref[...] = v
    vb_ref[...] = v.astype(vb_ref.dtype)
    base = qw + 2 * kvw
    for h in range(ih):
        sl = slice(h * IDX_DIM, (h + 1) * IDX_DIM)
        qi_ref[:, sl] = _rope_head(pa_ref[:, base + h * IDX_DIM: base + (h + 1) * IDX_DIM], c, sm, sp).astype(qi_ref.dtype)
    ki = _rope_head(pb_ref[:, 0:IDX_DIM], c, sm, sp)
    kif_ref[...] = ki
    kib_ref[...] = ki.astype(kib_ref.dtype)
    qm_ref[...] = pb_ref[:, IDX_DIM:IDX_DIM + memw].astype(qm_ref.dtype)
    wi_ref[...] = pb_ref[:, IDX_DIM + memw:IDX_DIM + memw + 128] * ((ih * IDX_DIM) ** -0.5)


def _dsa_split(pa, pb, tabs, rows_per_seq, nh, nkv, ih, memw, act_dtype):
    m = pa.shape[0]
    bm = _pick_block(rows_per_seq, 128, 8)
    nt = rows_per_seq // bm
    qw, kvw = nh * HEAD_DIM, nkv * HEAD_DIM
    row = lambda w: pl.BlockSpec((bm, w), lambda i: (i, 0))
    tab = pl.BlockSpec((bm, HEAD_DIM), lambda i: (i % nt, 0))
    shapes = [(qw, act_dtype), (kvw, F32), (kvw, act_dtype), (kvw, F32), (kvw, act_dtype),
              (ih * IDX_DIM, act_dtype), (IDX_DIM, F32), (IDX_DIM, act_dtype), (128, F32), (memw, act_dtype)]
    return pl.pallas_call(
        functools.partial(_dsa_split_kernel, nh=nh, nkv=nkv, ih=ih, memw=memw),
        grid=(m // bm,),
        in_specs=[row(pa.shape[1]), row(pb.shape[1]), tab, tab, tab],
        out_specs=[row(w) for w, _ in shapes],
        out_shape=[jax.ShapeDtypeStruct((m, w), dt) for w, dt in shapes],
        compiler_params=_cparams("parallel"),
        name="dsa_split",
    )(pa, pb, *tabs)


def _moba_split_kernel(p_ref, c_ref, sm_ref, sp_ref, q_ref, kf_ref, kb_ref, vf_ref, vb_ref, qm_ref, km_ref,
                       *, nh, nkv, memw):
    c, sm, sp = c_ref[...], sm_ref[...], sp_ref[...]
    qw, kvw = nh * HEAD_DIM, nkv * HEAD_DIM
    for h in range(nh):
        sl = slice(h * HEAD_DIM, (h + 1) * HEAD_DIM)
        q_ref[:, sl] = _rope_head(p_ref[:, sl], c, sm, sp).astype(q_ref.dtype)
    rows = p_ref.shape[0]
    for h in range(nkv):
        sl = slice(h * HEAD_DIM, (h + 1) * HEAD_DIM)
        kr = _rope_head(p_ref[:, qw + h * HEAD_DIM: qw + (h + 1) * HEAD_DIM], c, sm, sp)
        kf_ref[:, sl] = kr
        kb_ref[:, sl] = kr.astype(kb_ref.dtype)
        km_ref[0, :, sl] = jnp.sum(kr, axis=0, keepdims=True) * (1.0 / rows)
    v = p_ref[:, qw + kvw: qw + 2 * kvw]
    vf_ref[...] = v
    vb_ref[...] = v.astype(vb_ref.dtype)
    qm_ref[...] = p_ref[:, qw + 2 * kvw: qw + 2 * kvw + memw].astype(qm_ref.dtype)


def _moba_split(p, tabs, rows_per_seq, nh, nkv, memw, act_dtype):
    m = p.shape[0]
    bm = _pick_block(rows_per_seq, MOBA_BLOCK, 8)
    nt = rows_per_seq // bm
    qw, kvw = nh * HEAD_DIM, nkv * HEAD_DIM
    row = lambda w: pl.BlockSpec((bm, w), lambda i: (i, 0))
    tab = pl.BlockSpec((bm, HEAD_DIM), lambda i: (i % nt, 0))
    shapes = [(qw, act_dtype), (kvw, F32), (kvw, act_dtype), (kvw, F32), (kvw, act_dtype), (memw, act_dtype)]
    return pl.pallas_call(
        functools.partial(_moba_split_kernel, nh=nh, nkv=nkv, memw=memw),
        grid=(m // bm,),
        in_specs=[row(p.shape[1]), tab, tab, tab],
        out_specs=[row(w) for w, _ in shapes] + [pl.BlockSpec((1, 1, kvw), lambda i: (i, 0, 0))],
        out_shape=[jax.ShapeDtypeStruct((m, w), dt) for w, dt in shapes]
        + [jax.ShapeDtypeStruct((m // bm, 1, kvw), F32)],
        compiler_params=_cparams("parallel"),
        name="moba_split",
    )(p, *tabs)


def _sortable_key(x):
    bits = pltpu.bitcast(x, I32)
    return jnp.where(bits < 0, bits ^ jnp.int32(0x7FFFFFFF), bits)


def _kth_largest_key(count_ge, rows, k):
    imin = jnp.int32(-2 ** 31)
    c0 = count_ge(jnp.zeros((rows, 1), I32))
    thr = jnp.where(c0 >= k, jnp.int32(0), imin)

    def bit_body(it, thr):
        cand = thr + jnp.left_shift(jnp.int32(1), jnp.int32(30) - it)
        return jnp.where(count_ge(cand) >= k, cand, thr)

    return lax.fori_loop(0, 31, bit_body, thr)


def _lane_fold(x, acc, op):
    for w in range(x.shape[1] // 128):
        acc = op(acc, x[:, w * 128:(w + 1) * 128])
    return acc


def _paired_loop(n, body, carry):
    def pair(c2, carry):
        return body(2 * c2 + 1, body(2 * c2, carry))
    npair = n // 2
    carry = lax.fori_loop(0, npair, pair, carry)
    return lax.fori_loop(2 * npair, n, body, carry)


def _masked_attention(rows, n_dyn, logits_of, load_v, s_ref, own=None):
    mx = jnp.full((rows, 128), NEG, F32)
    if own is not None:
        slot, own_s, own_v = own
        s_ref[slot] = own_s
        mx = _lane_fold(own_s, mx, jnp.maximum)

    def pass_a(c, mx):
        s = logits_of(c)
        s_ref[c] = s
        return _lane_fold(s, mx, jnp.maximum)

    mx = _paired_loop(n_dyn, pass_a, mx)
    m = jnp.max(mx, axis=1, keepdims=True)

    def accumulate(c, vb, carry):
        lsum, acc = carry
        p = jnp.exp2(s_ref[c] - m)
        return _lane_fold(p, lsum, jnp.add), acc + jnp.dot(p.astype(BF16), vb, preferred_element_type=F32)

    carry = (jnp.zeros((rows, 128), F32), jnp.zeros((rows, HEAD_DIM), F32))
    if own is not None:
        carry = accumulate(slot, own_v, carry)
    lsum, acc = _paired_loop(n_dyn, lambda c, carry: accumulate(c, load_v(c), carry), carry)
    return acc / jnp.sum(lsum, axis=1, keepdims=True)


def _dsa_prompt_kernel(qi_ref, wi_ref, ki_ref, q_ref, k_ref, v_ref, o_ref, key_ref, bias_ref, s_ref,
                       *, topk, nkv, ih):
    i = pl.program_id(1)
    tq = ck = QTILE
    row = lax.broadcasted_iota(I32, (tq, ck), 0)
    col = lax.broadcasted_iota(I32, (tq, ck), 1)

    def causal_ok(kc):
        return col <= row + jnp.where(kc < i, ck, 0)

    def idx_body(kc, _):
        off = pl.multiple_of(kc * ck, ck)
        kic = ki_ref[0, pl.ds(off, ck), :]
        acc = jnp.zeros((tq, ck), F32)
        for h in range(ih):
            s = lax.dot_general(qi_ref[:, h * IDX_DIM:(h + 1) * IDX_DIM], kic, _CONTRACT_LAST,
                                preferred_element_type=F32)
            acc = acc + jnp.maximum(s, 0.0) * wi_ref[:, h:h + 1]
        key_ref[kc] = _sortable_key(jnp.where(causal_ok(kc), acc, -jnp.inf))
        return 0

    lax.fori_loop(0, i + 1, idx_body, 0)

    def count_ge(cand):
        def body(kc, acc):
            ge = (key_ref[kc] >= cand).astype(F32)
            for c in range(ck // 128):
                acc = acc + ge[:, c * 128:(c + 1) * 128]
            return acc
        acc = lax.fori_loop(0, i + 1, body, jnp.zeros((tq, 128), F32))
        return jnp.sum(acc, axis=1, keepdims=True)

    thr = _kth_largest_key(count_ge, tq, topk)

    def bias_body(kc, _):
        sel = jnp.logical_and(key_ref[kc] >= thr, causal_ok(kc))
        bias_ref[kc] = jnp.where(sel, 0.0, NEG)
        return 0

    lax.fori_loop(0, i + 1, bias_body, 0)

    for n in range(nkv):
        lanes = slice(n * HEAD_DIM, (n + 1) * HEAD_DIM)
        q3 = jnp.concatenate(
            [q_ref[:, (n * GROUP + g) * HEAD_DIM:(n * GROUP + g + 1) * HEAD_DIM] for g in range(GROUP)], axis=0)
        chunk = lambda ref, kc, lanes=lanes: ref[0, pl.ds(pl.multiple_of(kc * ck, ck), ck), lanes]

        def logits_of(kc, q3=q3, chunk=chunk):
            s = lax.dot_general(q3, chunk(k_ref, kc), _CONTRACT_LAST, preferred_element_type=F32)
            return s * SOFTMAX_C2 + jnp.concatenate([bias_ref[kc]] * GROUP, axis=0)

        out = _masked_attention(GROUP * tq, i + 1, logits_of, functools.partial(chunk, v_ref), s_ref)
        for g in range(GROUP):
            o_ref[:, (n * GROUP + g) * HEAD_DIM:(n * GROUP + g + 1) * HEAD_DIM] = (
                out[g * tq:(g + 1) * tq].astype(o_ref.dtype))


def _dsa_prompt(qi, wi, ki, q, k, v, bsz, t, nkv, ih, topk):
    nt = t // QTILE
    qw = nkv * GROUP * HEAD_DIM
    kvw = nkv * HEAD_DIM
    rows = lambda w: pl.BlockSpec((QTILE, w), lambda b, i: (b * nt + i, 0))
    seq = lambda w: pl.BlockSpec((1, t, w), lambda b, i: (b, 0, 0))
    return pl.pallas_call(
        functools.partial(_dsa_prompt_kernel, topk=topk, nkv=nkv, ih=ih),
        grid=(bsz, nt),
        in_specs=[rows(ih * IDX_DIM), rows(128), seq(IDX_DIM), rows(qw), seq(kvw), seq(kvw)],
        out_specs=rows(qw),
        out_shape=jax.ShapeDtypeStruct((bsz * t, qw), BF16),
        scratch_shapes=[pltpu.VMEM((nt, QTILE, QTILE), I32), pltpu.VMEM((nt, QTILE, QTILE), F32),
                        pltpu.VMEM((nt, GROUP * QTILE, QTILE), F32)],
        compiler_params=_cparams("parallel", "arbitrary"),
        name="dsa_prompt_attn",
    )(qi, wi, ki.reshape(bsz, t, IDX_DIM), q, k.reshape(bsz, t, kvw), v.reshape(bsz, t, kvw))


def _topb_select_bias(gate, valid, blk_iota, nblk):
    g = jnp.where(valid, gate, -jnp.inf)
    rank = jnp.zeros(g.shape, I32)
    for m in range(nblk):
        gm = g[:, m:m + 1]
        beats = jnp.logical_or(gm > g, jnp.logical_and(gm == g, blk_iota > m))
        rank = rank + beats.astype(I32)
    sel = jnp.logical_and(valid, rank < MOBA_TOPB)
    return jnp.where(sel, 0.0, NEG)


def _topb_select_rows(gate_t, valid_t, nblk):
    g = jnp.where(valid_t, gate_t, -jnp.inf)
    blk = lax.broadcasted_iota(I32, g.shape, 0)
    rank = jnp.zeros(g.shape, I32)
    for m in range(nblk):
        gm = g[m:m + 1, :]
        beats = jnp.logical_or(gm > g, jnp.logical_and(gm == g, blk > m))
        rank = rank + beats.astype(I32)
    return jnp.logical_and(valid_t, rank < MOBA_TOPB)


def _moba_prompt_kernel(q_ref, k_ref, v_ref, km_ref, o_ref, s_ref, *, nkv, nblk):
    j = pl.program_id(1)
    tq = blk = QTILE
    rows = GROUP * tq
    row = lax.broadcasted_iota(I32, (tq, blk), 0)
    col = lax.broadcasted_iota(I32, (tq, blk), 1)
    cb = jnp.where(col <= row, 0.0, NEG)
    causal_bias = jnp.concatenate([cb] * GROUP, axis=0)
    valid_t = lax.broadcasted_iota(I32, (nblk, rows), 0) < j
    lane = lax.broadcasted_iota(I32, (blk, HEAD_DIM), 1)
    km_pad = jnp.zeros((16 - nblk % 16, HEAD_DIM), F32)
    for n in range(nkv):
        lanes = slice(n * HEAD_DIM, (n + 1) * HEAD_DIM)
        q3 = jnp.concatenate(
            [q_ref[:, (n * GROUP + g) * HEAD_DIM:(n * GROUP + g + 1) * HEAD_DIM] for g in range(GROUP)], axis=0)
        km = jnp.concatenate([km_ref[0, :, lanes], km_pad], axis=0).astype(BF16)
        gate_t = lax.dot_general(km, q3, _CONTRACT_LAST, preferred_element_type=F32)[:nblk]
        selb_t = jnp.where(_topb_select_rows(gate_t, valid_t, nblk), 0.0, NEG)
        selb = jnp.concatenate([selb_t, jnp.zeros((HEAD_DIM - nblk, rows), F32)], axis=0).T
        q_aug = jnp.concatenate([q3, selb.astype(BF16)], axis=1)
        chunk = lambda ref, m, lanes=lanes: ref[0, pl.ds(pl.multiple_of(m * blk, blk), blk), lanes]

        def logits_of(m, q_aug=q_aug, chunk=chunk):
            onehot = jnp.where(lane == m, 1.0, 0.0).astype(BF16)
            k_aug = jnp.concatenate([chunk(k_ref, m), onehot], axis=1)
            return lax.dot_general(q_aug, k_aug, _CONTRACT_LAST, preferred_element_type=F32) * SOFTMAX_C2

        own_s = lax.dot_general(q3, chunk(k_ref, j), _CONTRACT_LAST, preferred_element_type=F32) * SOFTMAX_C2
        out = _masked_attention(rows, j, logits_of, functools.partial(chunk, v_ref), s_ref,
                                own=(j, own_s + causal_bias, chunk(v_ref, j)))
        for g in range(GROUP):
            o_ref[:, (n * GROUP + g) * HEAD_DIM:(n * GROUP + g + 1) * HEAD_DIM] = (
                out[g * tq:(g + 1) * tq].astype(o_ref.dtype))


def _moba_prompt(q, k, v, kmean, bsz, t, nkv):
    nt = t // QTILE
    qw = nkv * GROUP * HEAD_DIM
    kvw = nkv * HEAD_DIM
    rows = lambda w: pl.BlockSpec((QTILE, w), lambda b, i: (b * nt + i, 0))
    seq = lambda w: pl.BlockSpec((1, t, w), lambda b, i: (b, 0, 0))
    return pl.pallas_call(
        functools.partial(_moba_prompt_kernel, nkv=nkv, nblk=nt),
        grid=(bsz, nt),
        in_specs=[rows(qw), seq(kvw), seq(kvw), pl.BlockSpec((1, nt, kvw), lambda b, i: (b, 0, 0))],
        out_specs=rows(qw),
        out_shape=jax.ShapeDtypeStruct((bsz * t, qw), BF16),
        scratch_shapes=[pltpu.VMEM((nt, GROUP * QTILE, QTILE), F32)],
        compiler_params=_cparams("parallel", "arbitrary"),
        name="moba_prompt_attn",
    )(q, k.reshape(bsz, t, kvw), v.reshape(bsz, t, kvw), kmean.reshape(bsz, nt, kvw))


def _mem_attn_kernel(q_ref, mk_ref, mv_ref, o_ref, *, hd):
    scale = hd ** -0.5
    for c in range(MEM_HEADS):
        lanes = slice(c * hd, (c + 1) * hd)
        q = q_ref[0, :, lanes].astype(BF16)
        mk = mk_ref[0, :, lanes].astype(BF16)
        mv = mv_ref[0, :, lanes].astype(BF16)
        s = lax.dot_general(q, mk, _CONTRACT_LAST, preferred_element_type=F32) * scale
        p = jnp.exp(s - jnp.max(s, axis=1, keepdims=True))
        l = jnp.sum(p, axis=1, keepdims=True)
        o = jnp.dot(p.astype(BF16), mv, preferred_element_type=F32) / l
        o_ref[0, :, lanes] = o.astype(o_ref.dtype)


def _mem_attn(qm, mk, mv, out_dtype):
    bsz, t, w = qm.shape
    mlen = mk.shape[1]
    tq = _pick_block(t, 512, 8)
    return pl.pallas_call(
        functools.partial(_mem_attn_kernel, hd=w // MEM_HEADS),
        grid=(bsz, t // tq),
        in_specs=[pl.BlockSpec((1, tq, w), lambda b, i: (b, i, 0)),
                  pl.BlockSpec((1, mlen, w), lambda b, i: (b, 0, 0)),
                  pl.BlockSpec((1, mlen, w), lambda b, i: (b, 0, 0))],
        out_specs=pl.BlockSpec((1, tq, w), lambda b, i: (b, i, 0)),
        out_shape=jax.ShapeDtypeStruct((bsz, t, w), out_dtype),
        compiler_params=_cparams("parallel", "parallel"),
        name="mem_attn",
    )(qm, mk, mv)


SAMPLE_PP = 8


def _page_specs(shape_tail, pp):
    nd = len(shape_tail)
    return [pl.BlockSpec((None,) + shape_tail, lambda b, p, pt, c=c: (pt[b, p * pp + c],) + (0,) * nd)
            for c in range(pp)]


def _head_sum(w, ih):
    acc = w[0:SAMPLE_TPAD]
    for h in range(1, ih):
        acc = acc + w[h * SAMPLE_TPAD:(h + 1) * SAMPLE_TPAD]
    return acc


def _dsa_sample_index_kernel(pt_ref, qi_ref, wi_ref, *refs, pp, ih, page):
    page_refs, o_ref = refs[:pp], refs[pp]
    qi = qi_ref[0].astype(BF16)
    wi = wi_ref[0]
    for c in range(pp):
        kp = page_refs[c][...].astype(BF16)
        s = lax.dot_general(qi, kp, _CONTRACT_LAST, preferred_element_type=F32)
        o_ref[0, :, c * page:(c + 1) * page] = _head_sum(jnp.maximum(s, 0.0) * wi, ih)


def _dsa_sample_index(page_table, qi_ht, wi_ht, kidx_pool, ih):
    bs, npg = page_table.shape
    page = kidx_pool.shape[1]
    pp = SAMPLE_PP
    r = ih * SAMPLE_TPAD
    grid_spec = pltpu.PrefetchScalarGridSpec(
        num_scalar_prefetch=1,
        grid=(bs, npg // pp),
        in_specs=[pl.BlockSpec((1, r, IDX_DIM), lambda b, p, pt: (b, 0, 0)),
                  pl.BlockSpec((1, r, page), lambda b, p, pt: (b, 0, 0))]
        + _page_specs((page, IDX_DIM), pp),
        out_specs=pl.BlockSpec((1, SAMPLE_TPAD, page * pp), lambda b, p, pt: (b, 0, p)),
    )
    return pl.pallas_call(
        functools.partial(_dsa_sample_index_kernel, pp=pp, ih=ih, page=page),
        grid_spec=grid_spec,
        out_shape=jax.ShapeDtypeStruct((bs, SAMPLE_TPAD, npg * page), F32),
        compiler_params=_cparams("parallel", "arbitrary"),
        name="dsa_sample_index",
    )(page_table, qi_ht, wi_ht, *([kidx_pool] * pp))


def _dsa_sample_select_kernel(sc_ref, qi_ref, wi_ref, kin_ref, o_ref, *, topk, ih, ts, past):
    tail_w = kin_ref.shape[1]
    s = lax.dot_general(qi_ref[0].astype(BF16), kin_ref[0].astype(BF16), _CONTRACT_LAST,
                        preferred_element_type=F32)
    tail = _head_sum(jnp.maximum(s, 0.0) * wi_ref[0][:, :tail_w], ih)
    row = lax.broadcasted_iota(I32, (SAMPLE_TPAD, tail_w), 0)
    col = lax.broadcasted_iota(I32, (SAMPLE_TPAD, tail_w), 1)
    tail_ok = jnp.logical_and(col <= row, col < ts)
    full = jnp.concatenate([sc_ref[0], jnp.where(tail_ok, tail, -jnp.inf)], axis=1)
    keys = _sortable_key(full)
    width = past + tail_w
    col_f = lax.broadcasted_iota(I32, (SAMPLE_TPAD, width), 1)
    row_f = lax.broadcasted_iota(I32, (SAMPLE_TPAD, width), 0)
    visible = jnp.logical_or(col_f < past, jnp.logical_and(col_f - past <= row_f, col_f - past < ts))

    def count_ge(cand):
        return jnp.sum((keys >= cand).astype(F32), axis=1, keepdims=True)

    thr = _kth_largest_key(count_ge, SAMPLE_TPAD, topk)
    o_ref[0] = jnp.where(jnp.logical_and(keys >= thr, visible), 0.0, NEG)


def _dsa_sample_select(scores, qi_ht, wi_ht, ki_new, ih, ts, topk):
    bs, _, past = scores.shape
    tail_w = ki_new.shape[1]
    r = ih * SAMPLE_TPAD
    return pl.pallas_call(
        functools.partial(_dsa_sample_select_kernel, topk=topk, ih=ih, ts=ts, past=past),
        grid=(bs,),
        in_specs=[pl.BlockSpec((1, SAMPLE_TPAD, past), lambda b: (b, 0, 0)),
                  pl.BlockSpec((1, r, IDX_DIM), lambda b: (b, 0, 0)),
                  pl.BlockSpec((1, r, wi_ht.shape[2]), lambda b: (b, 0, 0)),
                  pl.BlockSpec((1, tail_w, IDX_DIM), lambda b: (b, 0, 0))],
        out_specs=pl.BlockSpec((1, SAMPLE_TPAD, past + tail_w), lambda b: (b, 0, 0)),
        out_shape=jax.ShapeDtypeStruct((bs, SAMPLE_TPAD, past + tail_w), F32),
        compiler_params=_cparams("parallel"),
        name="dsa_sample_select",
    )(scores, qi_ht, wi_ht, ki_new)


def _sample_flash_update(qbd, kblk, vblk, bias, m_ref, l_ref, acc_ref, nkv):
    rows_per_head = GROUP * SAMPLE_TPAD
    s = lax.dot_general(qbd, kblk, _CONTRACT_LAST, preferred_element_type=F32) * (HEAD_DIM ** -0.5) + bias
    m_old = m_ref[...]
    m_new = jnp.maximum(m_old, jnp.max(s, axis=1, keepdims=True))
    alpha = jnp.exp(m_old - m_new)
    p = jnp.exp(s - m_new)
    l_ref[...] = alpha * l_ref[...] + jnp.sum(p, axis=1, keepdims=True)
    m_ref[...] = m_new
    o_full = jnp.dot(p.astype(BF16), vblk, preferred_element_type=F32)
    o_diag = jnp.concatenate(
        [o_full[n * rows_per_head:(n + 1) * rows_per_head, n * HEAD_DIM:(n + 1) * HEAD_DIM] for n in range(nkv)],
        axis=0)
    acc_ref[...] = alpha * acc_ref[...] + o_diag


def _sample_flash_reset(m_ref, l_ref, acc_ref):
    m_ref[...] = jnp.full(m_ref.shape, NEG, F32)
    l_ref[...] = jnp.zeros(l_ref.shape, F32)
    acc_ref[...] = jnp.zeros(acc_ref.shape, F32)


def _concat_pages(refs):
    return jnp.concatenate([r[...].astype(BF16) for r in refs], axis=0)


def _tile_query_bias(b8, reps):
    return jnp.concatenate([b8] * reps, axis=0)


def _dsa_sample_attn_kernel(pt_ref, q_ref, bias_ref, tbias_ref, kn_ref, vn_ref, *refs, pp, nkv):
    k_refs, v_refs = refs[:pp], refs[pp:2 * pp]
    o_ref, m_ref, l_ref, acc_ref = refs[2 * pp:]
    p = pl.program_id(1)
    reps = nkv * GROUP

    @pl.when(p == 0)
    def _():
        _sample_flash_reset(m_ref, l_ref, acc_ref)

    _sample_flash_update(q_ref[0], _concat_pages(k_refs), _concat_pages(v_refs),
                         _tile_query_bias(bias_ref[0], reps), m_ref, l_ref, acc_ref, nkv)

    @pl.when(p == pl.num_programs(1) - 1)
    def _():
        _sample_flash_update(q_ref[0], kn_ref[0].astype(BF16), vn_ref[0].astype(BF16),
                             _tile_query_bias(tbias_ref[0], reps), m_ref, l_ref, acc_ref, nkv)
        o_ref[0] = acc_ref[...] / l_ref[...]


def _dsa_sample_attn(page_table, qbd, bias, k_new, v_new, k_pool, v_pool, nkv):
    bs, npg = page_table.shape
    page = k_pool.shape[1]
    kvw = nkv * HEAD_DIM
    pp = SAMPLE_PP
    rows = nkv * GROUP * SAMPLE_TPAD
    tail_w = k_new.shape[1]
    grid_spec = pltpu.PrefetchScalarGridSpec(
        num_scalar_prefetch=1,
        grid=(bs, npg // pp),
        in_specs=[pl.BlockSpec((1, rows, kvw), lambda b, p, pt: (b, 0, 0)),
                  pl.BlockSpec((1, SAMPLE_TPAD, page * pp), lambda b, p, pt: (b, 0, p)),
                  pl.BlockSpec((1, SAMPLE_TPAD, tail_w), lambda b, p, pt: (b, 0, (npg * page) // tail_w)),
                  pl.BlockSpec((1, tail_w, kvw), lambda b, p, pt: (b, 0, 0)),
                  pl.BlockSpec((1, tail_w, kvw), lambda b, p, pt: (b, 0, 0))]
        + _page_specs((page, kvw), pp) + _page_specs((page, kvw), pp),
        out_specs=pl.BlockSpec((1, rows, HEAD_DIM), lambda b, p, pt: (b, 0, 0)),
        scratch_shapes=[pltpu.VMEM((rows, 1), F32), pltpu.VMEM((rows, 1), F32), pltpu.VMEM((rows, HEAD_DIM), F32)],
    )
    return pl.pallas_call(
        functools.partial(_dsa_sample_attn_kernel, pp=pp, nkv=nkv),
        grid_spec=grid_spec,
        out_shape=jax.ShapeDtypeStruct((bs, rows, HEAD_DIM), F32),
        compiler_params=_cparams("parallel", "arbitrary"),
        name="dsa_sample_attn",
    )(page_table, qbd, bias, bias, k_new, v_new, *([k_pool] * pp), *([v_pool] * pp))


KMEAN_BLOCKS = 8


def _moba_kmean_kernel(pt_ref, *refs, nblk, ppb):
    k_refs, o_ref = refs[:nblk * ppb], refs[nblk * ppb]
    for blk in range(nblk):
        acc = jnp.sum(k_refs[blk * ppb][...], axis=0, keepdims=True)
        for c in range(1, ppb):
            acc = acc + jnp.sum(k_refs[blk * ppb + c][...], axis=0, keepdims=True)
        o_ref[0, blk:blk + 1, :] = acc * (1.0 / MOBA_BLOCK)


def _moba_sample_kmean(page_table, k_pool, nkv):
    bs, npg = page_table.shape
    page = k_pool.shape[1]
    ppb = MOBA_BLOCK // page
    nb = npg // ppb
    nblk = KMEAN_BLOCKS
    kvw = nkv * HEAD_DIM
    grid_spec = pltpu.PrefetchScalarGridSpec(
        num_scalar_prefetch=1,
        grid=(bs, nb // nblk),
        in_specs=_page_specs((page, kvw), nblk * ppb),
        out_specs=pl.BlockSpec((1, nblk, kvw), lambda b, p, pt: (b, p, 0)),
    )
    return pl.pallas_call(
        functools.partial(_moba_kmean_kernel, nblk=nblk, ppb=ppb),
        grid_spec=grid_spec,
        out_shape=jax.ShapeDtypeStruct((bs, nb, kvw), F32),
        compiler_params=_cparams("parallel", "arbitrary"),
        name="moba_sample_kmean",
    )(page_table, *([k_pool] * (nblk * ppb)))


def _moba_sample_attn_kernel(pt_ref, q_ref, km_ref, kn_ref, vn_ref, *refs, pp, nkv, page, nb, ts):
    k_refs, v_refs = refs[:pp], refs[pp:2 * pp]
    o_ref, m_ref, l_ref, acc_ref, sb_ref = refs[2 * pp:]
    p = pl.program_id(1)
    rows = nkv * GROUP * SAMPLE_TPAD
    ppb = MOBA_BLOCK // page
    blk_iota = lax.broadcasted_iota(I32, (rows, nb), 1)

    @pl.when(p == 0)
    def _():
        _sample_flash_reset(m_ref, l_ref, acc_ref)
        valid = blk_iota >= 0
        gate = lax.dot_general(q_ref[0], km_ref[0].astype(BF16), _CONTRACT_LAST, preferred_element_type=F32)
        sb_ref[...] = _topb_select_bias(gate, valid, blk_iota, nb)

    selb = sb_ref[...]
    cols = []
    for c in range(pp // ppb):
        blk = p * (pp // ppb) + c
        col = jnp.sum(jnp.where(blk_iota == blk, selb, 0.0), axis=1, keepdims=True)
        cols.append(jnp.broadcast_to(col, (rows, MOBA_BLOCK)))
    _sample_flash_update(q_ref[0], _concat_pages(k_refs), _concat_pages(v_refs),
                         jnp.concatenate(cols, axis=1), m_ref, l_ref, acc_ref, nkv)

    @pl.when(p == pl.num_programs(1) - 1)
    def _():
        tail_w = kn_ref.shape[1]
        row = lax.broadcasted_iota(I32, (rows, tail_w), 0) & (SAMPLE_TPAD - 1)
        col = lax.broadcasted_iota(I32, (rows, tail_w), 1)
        tb = jnp.where(jnp.logical_and(col <= row, col < ts), 0.0, NEG)
        _sample_flash_update(q_ref[0], kn_ref[0].astype(BF16), vn_ref[0].astype(BF16), tb,
                             m_ref, l_ref, acc_ref, nkv)
        o_ref[0] = acc_ref[...] / l_ref[...]


def _moba_sample_attn(page_table, qbd, kmean, k_new, v_new, k_pool, v_pool, nkv, ts):
    bs, npg = page_table.shape
    page = k_pool.shape[1]
    kvw = nkv * HEAD_DIM
    pp = SAMPLE_PP
    rows = nkv * GROUP * SAMPLE_TPAD
    nb = kmean.shape[1]
    tail_w = k_new.shape[1]
    grid_spec = pltpu.PrefetchScalarGridSpec(
        num_scalar_prefetch=1,
        grid=(bs, npg // pp),
        in_specs=[pl.BlockSpec((1, rows, kvw), lambda b, p, pt: (b, 0, 0)),
                  pl.BlockSpec((1, nb, kvw), lambda b, p, pt: (b, 0, 0)),
                  pl.BlockSpec((1, tail_w, kvw), lambda b, p, pt: (b, 0, 0)),
                  pl.BlockSpec((1, tail_w, kvw), lambda b, p, pt: (b, 0, 0))]
        + _page_specs((page, kvw), pp) + _page_specs((page, kvw), pp),
        out_specs=pl.BlockSpec((1, rows, HEAD_DIM), lambda b, p, pt: (b, 0, 0)),
        scratch_shapes=[pltpu.VMEM((rows, 1), F32), pltpu.VMEM((rows, 1), F32), pltpu.VMEM((rows, HEAD_DIM), F32),
                        pltpu.VMEM((rows, nb), F32)],
    )
    return pl.pallas_call(
        functools.partial(_moba_sample_attn_kernel, pp=pp, nkv=nkv, page=page, nb=nb, ts=ts),
        grid_spec=grid_spec,
        out_shape=jax.ShapeDtypeStruct((bs, rows, HEAD_DIM), F32),
        compiler_params=_cparams("parallel", "arbitrary"),
        name="moba_sample_attn",
    )(page_table, qbd, kmean, k_new, v_new, *([k_pool] * pp), *([v_pool] * pp))


def _heads_to_kv_major(q, bs, nkv):
    x = q.reshape(bs, SAMPLE_TPAD, nkv, GROUP, HEAD_DIM).transpose(0, 2, 3, 1, 4)
    eye = jnp.eye(nkv, dtype=q.dtype)
    bd = x[:, :, :, :, None, :] * eye[None, :, None, None, :, None]
    return bd.reshape(bs, nkv * GROUP * SAMPLE_TPAD, nkv * HEAD_DIM).astype(BF16)


def _kv_major_to_rows(o, bs, nkv):
    x = o.reshape(bs, nkv, GROUP, SAMPLE_TPAD, HEAD_DIM)
    return x.transpose(0, 3, 1, 2, 4).reshape(bs * SAMPLE_TPAD, nkv * GROUP * HEAD_DIM)


def _pad_rows(x, n):
    return jnp.pad(x, ((0, 0), (0, n - x.shape[1]), (0, 0)))


def _finish_layer(h, mix, qm, mk, mv, bsz, layer, w_o, g_ffn, w_up, w_down):
    m = h.shape[0]
    ma = _mem_attn(qm.reshape(bsz, m // bsz, -1), mk, mv, qm.dtype).reshape(m, -1)
    merged = jnp.concatenate([mix.astype(BF16), ma.astype(BF16)], axis=1)
    h = _matmul(merged, w_o, layer, F32, epilogue="residual", residual=h)
    u = _matmul(_rmsnorm(h, g_ffn, BF16), w_up, layer, BF16, epilogue="relu2")
    return _matmul(u, w_down, layer, F32, epilogue="residual", residual=h)


def kernel(x_prompt, x_sample, cache_dsa_k, cache_dsa_v, cache_dsa_kidx, cache_moba_k, cache_moba_v, cache_mem_k, cache_mem_v, page_table, mem_prompt, norm_mix, norm_mem, w_in_dsa, w_in_moba, w_mem_kv, w_out, norm_ffn, w_up, w_down, norm_final):
    bsz, t, d = x_prompt.shape
    bs, ts, _ = x_sample.shape
    depth = norm_mix.shape[0]
    nh = (3 * d) // (4 * HEAD_DIM)
    nkv = nh // GROUP
    qw, kvw = nh * HEAD_DIM, nkv * HEAD_DIM
    memw = d // 4
    mhd = memw // MEM_HEADS
    ih = d // 128
    npg = page_table.shape[1]
    page = cache_dsa_k.shape[2]
    past = npg * page
    mlen = mem_prompt.shape[1]
    tp = SAMPLE_TPAD

    hp = x_prompt.reshape(bsz * t, d)
    hs = jnp.pad(x_sample, ((0, 0), (0, tp - ts), (0, 0))).reshape(bs * tp, d)
    pos_p = jnp.arange(t, dtype=I32)
    pos_s = past + (jnp.arange(tp, dtype=I32) % ts)
    tabs_p = _rope_tables(pos_p)
    tabs_s = _rope_tables(jnp.tile(pos_s, bs))

    outs = {name: [] for name in ("pdk", "pdv", "pdki", "sdk", "sdv", "sdki", "pmk", "pmv", "smk", "smv", "mk", "mv")}
    mem_flat = mem_prompt.reshape(bsz * mlen, d)

    w_o, wu, wd, w_mem = w_out.astype(BF16), w_up.astype(BF16), w_down.astype(BF16), w_mem_kv.astype(BF16)
    w_dsa, w_moba = w_in_dsa.astype(BF16), w_in_moba.astype(BF16)
    c0 = qw + 2 * kvw + ih * IDX_DIM
    w_dsa_tail = jnp.concatenate(
        [w_in_dsa[:, :, c0 + ih:c0 + ih + IDX_DIM], w_in_dsa[:, :, c0 + ih + IDX_DIM:], w_in_dsa[:, :, c0:c0 + ih],
         jnp.zeros(w_in_dsa.shape[:2] + (128 - ih,), F32)], axis=2).astype(BF16)

    for i in range(depth):
        j = i // 2
        mkv = _matmul(_rmsnorm(mem_flat, norm_mem[i], BF16), w_mem, i, F32)
        mk_p = mkv[:, :memw].reshape(bsz, mlen, memw)
        mv_p = mkv[:, memw:].reshape(bsz, mlen, memw)
        outs["mk"].append(mk_p.reshape(bsz, mlen, MEM_HEADS, mhd))
        outs["mv"].append(mv_p.reshape(bsz, mlen, MEM_HEADS, mhd))
        hn_p = _rmsnorm(hp, norm_mix[i], BF16)
        hn_s = _rmsnorm(hs, norm_mix[i], BF16)
        mk_s = cache_mem_k[i].reshape(bs, mlen, memw)
        mv_s = cache_mem_v[i].reshape(bs, mlen, memw)
        if i % 2 == 0:
            pa = _matmul(hn_p, w_dsa, j, F32, n=c0)
            pb = _matmul(hn_p, w_dsa_tail, j, F32)
            q, kf, kb, vf, vb, qi, kif, kib, wi, qm = _dsa_split(pa, pb, tabs_p, t, nh, nkv, ih, memw, BF16)
            outs["pdk"].append(kf.reshape(bsz, t, nkv, HEAD_DIM))
            outs["pdv"].append(vf.reshape(bsz, t, nkv, HEAD_DIM))
            outs["pdki"].append(kif.reshape(bsz, t, IDX_DIM))
            mix_p = _dsa_prompt(qi, wi, kib, q, kb, vb, bsz, t, nkv, ih, min(DSA_TOPK, t // 4))
            pa = _matmul(hn_s, w_dsa, j, F32, n=c0)
            pb = _matmul(hn_s, w_dsa_tail, j, F32)
            q_s, kf, _, vf, _, qi_s, kif, _, wi_s, qm_s = _dsa_split(pa, pb, tabs_s, bs * tp, nh, nkv, ih, memw, F32)
            outs["sdk"].append(kf.reshape(bs, tp, nkv, HEAD_DIM)[:, :ts])
            outs["sdv"].append(vf.reshape(bs, tp, nkv, HEAD_DIM)[:, :ts])
            outs["sdki"].append(kif.reshape(bs, tp, IDX_DIM)[:, :ts])
            qi_ht = qi_s.reshape(bs, tp, ih, IDX_DIM).transpose(0, 2, 1, 3).reshape(bs, ih * tp, IDX_DIM)
            wi_ht = wi_s.reshape(bs, tp, 128)[:, :, :ih].transpose(0, 2, 1).reshape(bs, ih * tp, 1)
            wi_ht = jnp.broadcast_to(wi_ht, (bs, ih * tp, page))
            scores = _dsa_sample_index(page_table, qi_ht, wi_ht, cache_dsa_kidx[j], ih)
            ki_new = _pad_rows(kif.reshape(bs, tp, IDX_DIM), page)
            bias = _dsa_sample_select(scores, qi_ht, wi_ht, ki_new, ih, ts, min(DSA_TOPK, (past + ts) // 4))
            k_new = _pad_rows(kf.reshape(bs, tp, kvw), page)
            v_new = _pad_rows(vf.reshape(bs, tp, kvw), page)
            o_s = _dsa_sample_attn(page_table, _heads_to_kv_major(q_s, bs, nkv), bias, k_new, v_new,
                                   cache_dsa_k[j].reshape(-1, page, kvw), cache_dsa_v[j].reshape(-1, page, kvw), nkv)
            mix_s = _kv_major_to_rows(o_s, bs, nkv)
        else:
            p = _matmul(hn_p, w_moba, j, F32)
            q, kf, kb, vf, vb, qm, kmean = _moba_split(p, tabs_p, t, nh, nkv, memw, BF16)
            outs["pmk"].append(kf.reshape(bsz, t, nkv, HEAD_DIM))
            outs["pmv"].append(vf.reshape(bsz, t, nkv, HEAD_DIM))
            mix_p = _moba_prompt(q, kb, vb, kmean, bsz, t, nkv)
            p = _matmul(hn_s, w_moba, j, F32)
            q_s, kf, _, vf, _, qm_s, _ = _moba_split(p, tabs_s, bs * tp, nh, nkv, memw, F32)
            outs["smk"].append(kf.reshape(bs, tp, nkv, HEAD_DIM)[:, :ts])
            outs["smv"].append(vf.reshape(bs, tp, nkv, HEAD_DIM)[:, :ts])
            k_pool = cache_moba_k[j].reshape(-1, page, kvw)
            v_pool = cache_moba_v[j].reshape(-1, page, kvw)
            kmean_s = _moba_sample_kmean(page_table, k_pool, nkv)
            k_new = _pad_rows(kf.reshape(bs, tp, kvw), page)
            v_new = _pad_rows(vf.reshape(bs, tp, kvw), page)
            o_s = _moba_sample_attn(page_table, _heads_to_kv_major(q_s, bs, nkv), kmean_s, k_new, v_new,
                                    k_pool, v_pool, nkv, ts)
            mix_s = _kv_major_to_rows(o_s, bs, nkv)
        hp = _finish_layer(hp, mix_p, qm, mk_p, mv_p, bsz, i, w_o, norm_ffn[i], wu, wd)
        hs = _finish_layer(hs, mix_s, qm_s, mk_s, mv_s, bs, i, w_o, norm_ffn[i], wu, wd)

    y_prompt = _rmsnorm(hp, norm_final, F32).reshape(bsz, t, d)
    y_sample = _rmsnorm(hs, norm_final, F32).reshape(bs, tp, d)[:, :ts]
    st = jnp.stack
    return (y_prompt, y_sample, st(outs["pdk"]), st(outs["pdv"]), st(outs["pdki"]), st(outs["pmk"]), st(outs["pmv"]),
            st(outs["mk"]), st(outs["mv"]), st(outs["sdk"]), st(outs["sdv"]), st(outs["sdki"]),
            st(outs["smk"]), st(outs["smv"]))
```

```python
import functools

import jax
import jax.numpy as jnp
from jax import lax
from jax.experimental import pallas as pl
from jax.experimental.pallas import tpu as pltpu

F32 = jnp.float32
BF16 = jnp.bfloat16
I32 = jnp.int32

HEAD_DIM = 128
GROUP = 3
ROT_DIM = 32
ROPE_THETA = 500000.0
IDX_DIM = 128
MEM_HEADS = 4
DSA_TOPK = 256
MOBA_BLOCK = 256
MOBA_TOPB = 3
EPS = 1e-6
NEG = -1e30
LOG2E = 1.4426950408889634
SOFTMAX_C2 = (HEAD_DIM ** -0.5) * LOG2E
QTILE = 256
SAMPLE_TPAD = 8
VMEM_LIMIT = 56 * 1024 * 1024

_CONTRACT_LAST = (((1,), (1,)), ((), ()))


def _cparams(*sem):
    return pltpu.CompilerParams(dimension_semantics=sem, vmem_limit_bytes=VMEM_LIMIT)


def _pick_block(n, pref, align=128):
    if n <= pref:
        return n
    b = (pref // align) * align
    while b >= align:
        if n % b == 0:
            return b
        b -= align
    return n


def _rmsnorm_kernel(x_ref, g_ref, o_ref):
    x = x_ref[...]
    ms = jnp.mean(x * x, axis=-1, keepdims=True)
    o_ref[...] = ((x * lax.rsqrt(ms + EPS)) * g_ref[...]).astype(o_ref.dtype)


def _rmsnorm(x, g, out_dtype):
    m, d = x.shape
    bm = _pick_block(m, 256, 8)
    return pl.pallas_call(
        _rmsnorm_kernel,
        grid=(m // bm,),
        in_specs=[pl.BlockSpec((bm, d), lambda i: (i, 0)), pl.BlockSpec((1, d), lambda i: (0, 0))],
        out_specs=pl.BlockSpec((bm, d), lambda i: (i, 0)),
        out_shape=jax.ShapeDtypeStruct((m, d), out_dtype),
        compiler_params=_cparams("parallel"),
        name="rmsnorm",
    )(x, g.reshape(1, d).astype(F32))


def _mm_kernel(*refs, nk, epilogue):
    if epilogue == "residual":
        a_ref, w_ref, r_ref, o_ref, acc_ref = refs
    else:
        a_ref, w_ref, o_ref, acc_ref = refs
        r_ref = None
    k = pl.program_id(2)
    part = jnp.dot(a_ref[...], w_ref[...], preferred_element_type=F32)

    def finish(acc):
        if epilogue == "relu2":
            r = jnp.maximum(acc, 0.0)
            acc = r * r
        elif epilogue == "residual":
            acc = r_ref[...] + acc
        o_ref[...] = acc.astype(o_ref.dtype)

    if nk == 1:
        finish(part)
        return

    @pl.when(k == 0)
    def _():
        acc_ref[...] = part

    @pl.when(jnp.logical_and(k > 0, k < nk - 1))
    def _():
        acc_ref[...] += part

    @pl.when(k == nk - 1)
    def _():
        finish(acc_ref[...] + part)


def _matmul(a, w, layer, out_dtype, epilogue=None, residual=None, n=None, bm_pref=512, bn_pref=1024, bk_pref=4096):
    m, kdim = a.shape
    n = w.shape[2] if n is None else n
    bm = _pick_block(m, bm_pref, 8)
    bn = _pick_block(n, bn_pref)
    bk = _pick_block(kdim, bk_pref)
    nk = kdim // bk
    in_specs = [pl.BlockSpec((bm, bk), lambda j, i, k: (i, k)),
                pl.BlockSpec((None, bk, bn), lambda j, i, k: (layer, k, j))]
    args = [a, w]
    if epilogue == "residual":
        in_specs.append(pl.BlockSpec((bm, bn), lambda j, i, k: (i, j)))
        args.append(residual)
    return pl.pallas_call(
        functools.partial(_mm_kernel, nk=nk, epilogue=epilogue),
        grid=(n // bn, m // bm, nk),
        in_specs=in_specs,
        out_specs=pl.BlockSpec((bm, bn), lambda j, i, k: (i, j)),
        out_shape=jax.ShapeDtypeStruct((m, n), out_dtype),
        scratch_shapes=[pltpu.VMEM((bm, bn), F32)],
        compiler_params=_cparams("parallel", "parallel", "arbitrary"),
        name="matmul",
    )(*args)


def _rope_tables(pos):
    half = ROT_DIM // 2
    inv = ROPE_THETA ** (-jnp.arange(half, dtype=F32) / half)
    ang = pos.astype(F32)[:, None] * inv[None, :]
    cos, sin = jnp.cos(ang), jnp.sin(ang)
    r = pos.shape[0]
    z16 = jnp.zeros((r, half), F32)
    zrest = jnp.zeros((r, HEAD_DIM - ROT_DIM), F32)
    c = jnp.concatenate([cos, cos, jnp.ones((r, HEAD_DIM - ROT_DIM), F32)], axis=1)
    sm = jnp.concatenate([-sin, z16, zrest], axis=1)
    sp = jnp.concatenate([z16, sin, zrest], axis=1)
    return c, sm, sp


def _rope_head(x, c, sm, sp):
    half = ROT_DIM // 2
    return x * c + pltpu.roll(x, HEAD_DIM - half, 1) * sm + pltpu.roll(x, half, 1) * sp


def _dsa_split_kernel(pa_ref, pb_ref, c_ref, sm_ref, sp_ref,
                      q_ref, kf_ref, kb_ref, vf_ref, vb_ref, qi_ref, kif_ref, kib_ref, wi_ref, qm_ref,
                      *, nh, nkv, ih, memw):
    c, sm, sp = c_ref[...], sm_ref[...], sp_ref[...]
    qw, kvw = nh * HEAD_DIM, nkv * HEAD_DIM
    for h in range(nh):
        sl = slice(h * HEAD_DIM, (h + 1) * HEAD_DIM)
        q_ref[:, sl] = _rope_head(pa_ref[:, sl], c, sm, sp).astype(q_ref.dtype)
    for h in range(nkv):
        sl = slice(h * HEAD_DIM, (h + 1) * HEAD_DIM)
        kr = _rope_head(pa_ref[:, qw + h * HEAD_DIM: qw + (h + 1) * HEAD_DIM], c, sm, sp)
        kf_ref[:, sl] = kr
        kb_ref[:, sl] = kr.astype(kb_ref.dtype)
    v = pa_ref[:, qw + kvw: qw + 2 * kvw]
    vf_ref[...] = v
    vb_ref[...] = v.astype(vb_ref.dtype)
    base = qw + 2 * kvw
    for h in range(ih):
        sl = slice(h * IDX_DIM, (h + 1) * IDX_DIM)
        qi_ref[:, sl] = _rope_head(pa_ref[:, base + h * IDX_DIM: base + (h + 1) * IDX_DIM], c, sm, sp).astype(qi_ref.dtype)
    ki = _rope_head(pb_ref[:, 0:IDX_DIM], c, sm, sp)
    kif_ref[...] = ki
    kib_ref[...] = ki.astype(kib_ref.dtype)
    qm_ref[...] = pb_ref[:, IDX_DIM:IDX_DIM + memw].astype(qm_ref.dtype)
    wi_ref[...] = pb_ref[:, IDX_DIM + memw:IDX_DIM + memw + 128] * ((ih * IDX_DIM) ** -0.5)


def _dsa_split(pa, pb, tabs, rows_per_seq, nh, nkv, ih, memw, act_dtype):
    m = pa.shape[0]
    bm = _pick_block(rows_per_seq, 128, 8)
    nt = rows_per_seq // bm
    qw, kvw = nh * HEAD_DIM, nkv * HEAD_DIM
    row = lambda w: pl.BlockSpec((bm, w), lambda i: (i, 0))
    tab = pl.BlockSpec((bm, HEAD_DIM), lambda i: (i % nt, 0))
    shapes = [(qw, act_dtype), (kvw, F32), (kvw, act_dtype), (kvw, F32), (kvw, act_dtype),
              (ih * IDX_DIM, act_dtype), (IDX_DIM, F32), (IDX_DIM, act_dtype), (128, F32), (memw, act_dtype)]
    return pl.pallas_call(
        functools.partial(_dsa_split_kernel, nh=nh, nkv=nkv, ih=ih, memw=memw),
        grid=(m // bm,),
        in_specs=[row(pa.shape[1]), row(pb.shape[1]), tab, tab, tab],
        out_specs=[row(w) for w, _ in shapes],
        out_shape=[jax.ShapeDtypeStruct((m, w), dt) for w, dt in shapes],
        compiler_params=_cparams("parallel"),
        name="dsa_split",
    )(pa, pb, *tabs)


def _moba_split_kernel(p_ref, c_ref, sm_ref, sp_ref, q_ref, kf_ref, kb_ref, vf_ref, vb_ref, qm_ref, km_ref,
                       *, nh, nkv, memw):
    c, sm, sp = c_ref[...], sm_ref[...], sp_ref[...]
    qw, kvw = nh * HEAD_DIM, nkv * HEAD_DIM
    for h in range(nh):
        sl = slice(h * HEAD_DIM, (h + 1) * HEAD_DIM)
        q_ref[:, sl] = _rope_head(p_ref[:, sl], c, sm, sp).astype(q_ref.dtype)
    rows = p_ref.shape[0]
    for h in range(nkv):
        sl = slice(h * HEAD_DIM, (h + 1) * HEAD_DIM)
        kr = _rope_head(p_ref[:, qw + h * HEAD_DIM: qw + (h + 1) * HEAD_DIM], c, sm, sp)
        kf_ref[:, sl] = kr
        kb_ref[:, sl] = kr.astype(kb_ref.dtype)
        km_ref[0, :, sl] = jnp.sum(kr, axis=0, keepdims=True) * (1.0 / rows)
    v = p_ref[:, qw + kvw: qw + 2 * kvw]
    vf_ref[...] = v
    vb_ref[...] = v.astype(vb_ref.dtype)
    qm_ref[...] = p_ref[:, qw + 2 * kvw: qw + 2 * kvw + memw].astype(qm_ref.dtype)


def _moba_split(p, tabs, rows_per_seq, nh, nkv, memw, act_dtype):
    m = p.shape[0]
    bm = _pick_block(rows_per_seq, MOBA_BLOCK, 8)
    nt = rows_per_seq // bm
    qw, kvw = nh * HEAD_DIM, nkv * HEAD_DIM
    row = lambda w: pl.BlockSpec((bm, w), lambda i: (i, 0))
    tab = pl.BlockSpec((bm, HEAD_DIM), lambda i: (i % nt, 0))
    shapes = [(qw, act_dtype), (kvw, F32), (kvw, act_dtype), (kvw, F32), (kvw, act_dtype), (memw, act_dtype)]
    return pl.pallas_call(
        functools.partial(_moba_split_kernel, nh=nh, nkv=nkv, memw=memw),
        grid=(m // bm,),
        in_specs=[row(p.shape[1]), tab, tab, tab],
        out_specs=[row(w) for w, _ in shapes] + [pl.BlockSpec((1, 1, kvw), lambda i: (i, 0, 0))],
        out_shape=[jax.ShapeDtypeStruct((m, w), dt) for w, dt in shapes]
        + [jax.ShapeDtypeStruct((m // bm, 1, kvw), F32)],
        compiler_params=_cparams("parallel"),
        name="moba_split",
    )(p, *tabs)


def _sortable_key(x):
    bits = pltpu.bitcast(x, I32)
    return jnp.where(bits < 0, bits ^ jnp.int32(0x7FFFFFFF), bits)


def _kth_largest_key(count_ge, rows, k):
    imin = jnp.int32(-2 ** 31)
    c0 = count_ge(jnp.zeros((rows, 1), I32))
    thr = jnp.where(c0 >= k, jnp.int32(0), imin)

    def bit_body(it, thr):
        cand = thr + jnp.left_shift(jnp.int32(1), jnp.int32(30) - it)
        return jnp.where(count_ge(cand) >= k, cand, thr)

    return lax.fori_loop(0, 31, bit_body, thr)


def _lane_fold(x, acc, op):
    for w in range(x.shape[1] // 128):
        acc = op(acc, x[:, w * 128:(w + 1) * 128])
    return acc


def _paired_loop(n, body, carry):
    def pair(c2, carry):
        return body(2 * c2 + 1, body(2 * c2, carry))
    npair = n // 2
    carry = lax.fori_loop(0, npair, pair, carry)
    return lax.fori_loop(2 * npair, n, body, carry)


def _masked_attention(rows, n_dyn, logits_of, load_v, s_ref, own=None):
    mx = jnp.full((rows, 128), NEG, F32)
    if own is not None:
        slot, own_s, own_v = own
        s_ref[slot] = own_s
        mx = _lane_fold(own_s, mx, jnp.maximum)

    def pass_a(c, mx):
        s = logits_of(c)
        s_ref[c] = s
        return _lane_fold(s, mx, jnp.maximum)

    mx = _paired_loop(n_dyn, pass_a, mx)
    m = jnp.max(mx, axis=1, keepdims=True)

    def accumulate(c, vb, carry):
        lsum, acc = carry
        p = jnp.exp2(s_ref[c] - m)
        return _lane_fold(p, lsum, jnp.add), acc + jnp.dot(p.astype(BF16), vb, preferred_element_type=F32)

    carry = (jnp.zeros((rows, 128), F32), jnp.zeros((rows, HEAD_DIM), F32))
    if own is not None:
        carry = accumulate(slot, own_v, carry)
    lsum, acc = _paired_loop(n_dyn, lambda c, carry: accumulate(c, load_v(c), carry), carry)
    return acc / jnp.sum(lsum, axis=1, keepdims=True)


def _dsa_prompt_kernel(qi_ref, wi_ref, ki_ref, q_ref, k_ref, v_ref, o_ref, key_ref, bias_ref, s_ref,
                       *, topk, nkv, ih):
    i = pl.program_id(1)
    tq = ck = QTILE
    row = lax.broadcasted_iota(I32, (tq, ck), 0)
    col = lax.broadcasted_iota(I32, (tq, ck), 1)

    def causal_ok(kc):
        return col <= row + jnp.where(kc < i, ck, 0)

    def idx_body(kc, _):
        off = pl.multiple_of(kc * ck, ck)
        kic = ki_ref[0, pl.ds(off, ck), :]
        acc = jnp.zeros((tq, ck), F32)
        for h in range(ih):
            s = lax.dot_general(qi_ref[:, h * IDX_DIM:(h + 1) * IDX_DIM], kic, _CONTRACT_LAST,
                                preferred_element_type=F32)
            acc = acc + jnp.maximum(s, 0.0) * wi_ref[:, h:h + 1]
        key_ref[kc] = _sortable_key(jnp.where(causal_ok(kc), acc, -jnp.inf))
        return 0

    lax.fori_loop(0, i + 1, idx_body, 0)

    def count_ge(cand):
        def body(kc, acc):
            ge = (key_ref[kc] >= cand).astype(F32)
            for c in range(ck // 128):
                acc = acc + ge[:, c * 128:(c + 1) * 128]
            return acc
        acc = lax.fori_loop(0, i + 1, body, jnp.zeros((tq, 128), F32))
        return jnp.sum(acc, axis=1, keepdims=True)

    thr = _kth_largest_key(count_ge, tq, topk)

    def bias_body(kc, _):
        sel = jnp.logical_and(key_ref[kc] >= thr, causal_ok(kc))
        bias_ref[kc] = jnp.where(sel, 0.0, NEG)
        return 0

    lax.fori_loop(0, i + 1, bias_body, 0)

    for n in range(nkv):
        lanes = slice(n * HEAD_DIM, (n + 1) * HEAD_DIM)
        q3 = jnp.concatenate(
            [q_ref[:, (n * GROUP + g) * HEAD_DIM:(n * GROUP + g + 1) * HEAD_DIM] for g in range(GROUP)], axis=0)
        chunk = lambda ref, kc, lanes=lanes: ref[0, pl.ds(pl.multiple_of(kc * ck, ck), ck), lanes]

        def logits_of(kc, q3=q3, chunk=chunk):
            s = lax.dot_general(q3, chunk(k_ref, kc), _CONTRACT_LAST, preferred_element_type=F32)
            return s * SOFTMAX_C2 + jnp.concatenate([bias_ref[kc]] * GROUP, axis=0)

        out = _masked_attention(GROUP * tq, i + 1, logits_of, functools.partial(chunk, v_ref), s_ref)
        for g in range(GROUP):
            o_ref[:, (n * GROUP + g) * HEAD_DIM:(n * GROUP + g + 1) * HEAD_DIM] = (
                out[g * tq:(g + 1) * tq].astype(o_ref.dtype))


def _dsa_prompt(qi, wi, ki, q, k, v, bsz, t, nkv, ih, topk):
    nt = t // QTILE
    qw = nkv * GROUP * HEAD_DIM
    kvw = nkv * HEAD_DIM
    rows = lambda w: pl.BlockSpec((QTILE, w), lambda b, i: (b * nt + i, 0))
    seq = lambda w: pl.BlockSpec((1, t, w), lambda b, i: (b, 0, 0))
    return pl.pallas_call(
        functools.partial(_dsa_prompt_kernel, topk=topk, nkv=nkv, ih=ih),
        grid=(bsz, nt),
        in_specs=[rows(ih * IDX_DIM), rows(128), seq(IDX_DIM), rows(qw), seq(kvw), seq(kvw)],
        out_specs=rows(qw),
        out_shape=jax.ShapeDtypeStruct((bsz * t, qw), BF16),
        scratch_shapes=[pltpu.VMEM((nt, QTILE, QTILE), I32), pltpu.VMEM((nt, QTILE, QTILE), F32),
                        pltpu.VMEM((nt, GROUP * QTILE, QTILE), F32)],
        compiler_params=_cparams("parallel", "arbitrary"),
        name="dsa_prompt_attn",
    )(qi, wi, ki.reshape(bsz, t, IDX_DIM), q, k.reshape(bsz, t, kvw), v.reshape(bsz, t, kvw))


def _topb_select_bias(gate, valid, blk_iota, nblk):
    g = jnp.where(valid, gate, -jnp.inf)
    rank = jnp.zeros(g.shape, I32)
    for m in range(nblk):
        gm = g[:, m:m + 1]
        beats = jnp.logical_or(gm > g, jnp.logical_and(gm == g, blk_iota > m))
        rank = rank + beats.astype(I32)
    sel = jnp.logical_and(valid, rank < MOBA_TOPB)
    return jnp.where(sel, 0.0, NEG)


def _topb_select_rows(gate_t, valid_t, nblk):
    g = jnp.where(valid_t, gate_t, -jnp.inf)
    blk = lax.broadcasted_iota(I32, g.shape, 0)
    rank = jnp.zeros(g.shape, I32)
    for m in range(nblk):
        gm = g[m:m + 1, :]
        beats = jnp.logical_or(gm > g, jnp.logical_and(gm == g, blk > m))
        rank = rank + beats.astype(I32)
    return jnp.logical_and(valid_t, rank < MOBA_TOPB)


def _moba_prompt_kernel(q_ref, k_ref, v_ref, km_ref, o_ref, s_ref, *, nkv, nblk):
    j = pl.program_id(1)
    tq = blk = QTILE
    rows = GROUP * tq
    row = lax.broadcasted_iota(I32, (tq, blk), 0)
    col = lax.broadcasted_iota(I32, (tq, blk), 1)
    cb = jnp.where(col <= row, 0.0, NEG)
    causal_bias = jnp.concatenate([cb] * GROUP, axis=0)
    valid_t = lax.broadcasted_iota(I32, (nblk, rows), 0) < j
    lane = lax.broadcasted_iota(I32, (blk, HEAD_DIM), 1)
    km_pad = jnp.zeros((16 - nblk % 16, HEAD_DIM), F32)
    for n in range(nkv):
        lanes = slice(n * HEAD_DIM, (n + 1) * HEAD_DIM)
        q3 = jnp.concatenate(
            [q_ref[:, (n * GROUP + g) * HEAD_DIM:(n * GROUP + g + 1) * HEAD_DIM] for g in range(GROUP)], axis=0)
        km = jnp.concatenate([km_ref[0, :, lanes], km_pad], axis=0).astype(BF16)
        gate_t = lax.dot_general(km, q3, _CONTRACT_LAST, preferred_element_type=F32)[:nblk]
        selb_t = jnp.where(_topb_select_rows(gate_t, valid_t, nblk), 0.0, NEG)
        selb = jnp.concatenate([selb_t, jnp.zeros((HEAD_DIM - nblk, rows), F32)], axis=0).T
        q_aug = jnp.concatenate([q3, selb.astype(BF16)], axis=1)
        chunk = lambda ref, m, lanes=lanes: ref[0, pl.ds(pl.multiple_of(m * blk, blk), blk), lanes]

        def logits_of(m, q_aug=q_aug, chunk=chunk):
            onehot = jnp.where(lane == m, 1.0, 0.0).astype(BF16)
            k_aug = jnp.concatenate([chunk(k_ref, m), onehot], axis=1)
            return lax.dot_general(q_aug, k_aug, _CONTRACT_LAST, preferred_element_type=F32) * SOFTMAX_C2

        own_s = lax.dot_general(q3, chunk(k_ref, j), _CONTRACT_LAST, preferred_element_type=F32) * SOFTMAX_C2
        out = _masked_attention(rows, j, logits_of, functools.partial(chunk, v_ref), s_ref,
                                own=(j, own_s + causal_bias, chunk(v_ref, j)))
        for g in range(GROUP):
            o_ref[:, (n * GROUP + g) * HEAD_DIM:(n * GROUP + g + 1) * HEAD_DIM] = (
                out[g * tq:(g + 1) * tq].astype(o_ref.dtype))


def _moba_prompt(q, k, v, kmean, bsz, t, nkv):
    nt = t // QTILE
    qw = nkv * GROUP * HEAD_DIM
    kvw = nkv * HEAD_DIM
    rows = lambda w: pl.BlockSpec((QTILE, w), lambda b, i: (b * nt + i, 0))
    seq = lambda w: pl.BlockSpec((1, t, w), lambda b, i: (b, 0, 0))
    return pl.pallas_call(
        functools.partial(_moba_prompt_kernel, nkv=nkv, nblk=nt),
        grid=(bsz, nt),
        in_specs=[rows(qw), seq(kvw), seq(kvw), pl.BlockSpec((1, nt, kvw), lambda b, i: (b, 0, 0))],
        out_specs=rows(qw),
        out_shape=jax.ShapeDtypeStruct((bsz * t, qw), BF16),
        scratch_shapes=[pltpu.VMEM((nt, GROUP * QTILE, QTILE), F32)],
        compiler_params=_cparams("parallel", "arbitrary"),
        name="moba_prompt_attn",
    )(q, k.reshape(bsz, t, kvw), v.reshape(bsz, t, kvw), kmean.reshape(bsz, nt, kvw))


def _mem_attn_kernel(q_ref, mk_ref, mv_ref, o_ref, *, hd):
    scale = hd ** -0.5
    for c in range(MEM_HEADS):
        lanes = slice(c * hd, (c + 1) * hd)
        q = q_ref[0, :, lanes].astype(BF16)
        mk = mk_ref[0, :, lanes].astype(BF16)
        mv = mv_ref[0, :, lanes].astype(BF16)
        s = lax.dot_general(q, mk, _CONTRACT_LAST, preferred_element_type=F32) * scale
        p = jnp.exp(s - jnp.max(s, axis=1, keepdims=True))
        l = jnp.sum(p, axis=1, keepdims=True)
        o = jnp.dot(p.astype(BF16), mv, preferred_element_type=F32) / l
        o_ref[0, :, lanes] = o.astype(o_ref.dtype)


def _mem_attn(qm, mk, mv, out_dtype):
    bsz, t, w = qm.shape
    mlen = mk.shape[1]
    tq = _pick_block(t, 512, 8)
    return pl.pallas_call(
        functools.partial(_mem_attn_kernel, hd=w // MEM_HEADS),
        grid=(bsz, t // tq),
        in_specs=[pl.BlockSpec((1, tq, w), lambda b, i: (b, i, 0)),
                  pl.BlockSpec((1, mlen, w), lambda b, i: (b, 0, 0)),
                  pl.BlockSpec((1, mlen, w), lambda b, i: (b, 0, 0))],
        out_specs=pl.BlockSpec((1, tq, w), lambda b, i: (b, i, 0)),
        out_shape=jax.ShapeDtypeStruct((bsz, t, w), out_dtype),
        compiler_params=_cparams("parallel", "parallel"),
        name="mem_attn",
    )(qm, mk, mv)


SAMPLE_PP = 8


def _page_specs(shape_tail, pp, layer):
    nd = len(shape_tail)
    return [pl.BlockSpec((None, None) + shape_tail,
                         lambda b, p, pt, c=c: (layer, pt[b, p * pp + c]) + (0,) * nd)
            for c in range(pp)]


def _head_sum(w, ih):
    acc = w[0:SAMPLE_TPAD]
    for h in range(1, ih):
        acc = acc + w[h * SAMPLE_TPAD:(h + 1) * SAMPLE_TPAD]
    return acc


def _dsa_sample_index_kernel(pt_ref, qi_ref, wi_ref, *refs, pp, ih, page):
    page_refs, o_ref = refs[:pp], refs[pp]
    qi = qi_ref[0].astype(BF16)
    wi = wi_ref[0]
    for c in range(pp):
        kp = page_refs[c][...].astype(BF16)
        s = lax.dot_general(qi, kp, _CONTRACT_LAST, preferred_element_type=F32)
        o_ref[0, :, c * page:(c + 1) * page] = _head_sum(jnp.maximum(s, 0.0) * wi, ih)


def _dsa_sample_index(page_table, qi_ht, wi_ht, kidx_pool, layer, ih):
    bs, npg = page_table.shape
    page = kidx_pool.shape[2]
    pp = SAMPLE_PP
    r = ih * SAMPLE_TPAD
    grid_spec = pltpu.PrefetchScalarGridSpec(
        num_scalar_prefetch=1,
        grid=(bs, npg // pp),
        in_specs=[pl.BlockSpec((1, r, IDX_DIM), lambda b, p, pt: (b, 0, 0)),
                  pl.BlockSpec((1, r, page), lambda b, p, pt: (b, 0, 0))]
        + _page_specs((page, IDX_DIM), pp, layer),
        out_specs=pl.BlockSpec((1, SAMPLE_TPAD, page * pp), lambda b, p, pt: (b, 0, p)),
    )
    return pl.pallas_call(
        functools.partial(_dsa_sample_index_kernel, pp=pp, ih=ih, page=page),
        grid_spec=grid_spec,
        out_shape=jax.ShapeDtypeStruct((bs, SAMPLE_TPAD, npg * page), F32),
        compiler_params=_cparams("parallel", "arbitrary"),
        name="dsa_sample_index",
    )(page_table, qi_ht, wi_ht, *([kidx_pool] * pp))


def _dsa_sample_select_kernel(sc_ref, qi_ref, wi_ref, kin_ref, o_ref, *, topk, ih, ts, past):
    tail_w = kin_ref.shape[1]
    s = lax.dot_general(qi_ref[0].astype(BF16), kin_ref[0].astype(BF16), _CONTRACT_LAST,
                        preferred_element_type=F32)
    tail = _head_sum(jnp.maximum(s, 0.0) * wi_ref[0][:, :tail_w], ih)
    row = lax.broadcasted_iota(I32, (SAMPLE_TPAD, tail_w), 0)
    col = lax.broadcasted_iota(I32, (SAMPLE_TPAD, tail_w), 1)
    tail_ok = jnp.logical_and(col <= row, col < ts)
    full = jnp.concatenate([sc_ref[0], jnp.where(tail_ok, tail, -jnp.inf)], axis=1)
    keys = _sortable_key(full)
    width = past + tail_w
    col_f = lax.broadcasted_iota(I32, (SAMPLE_TPAD, width), 1)
    row_f = lax.broadcasted_iota(I32, (SAMPLE_TPAD, width), 0)
    visible = jnp.logical_or(col_f < past, jnp.logical_and(col_f - past <= row_f, col_f - past < ts))

    def count_ge(cand):
        return jnp.sum((keys >= cand).astype(F32), axis=1, keepdims=True)

    thr = _kth_largest_key(count_ge, SAMPLE_TPAD, topk)
    o_ref[0] = jnp.where(jnp.logical_and(keys >= thr, visible), 0.0, NEG)


def _dsa_sample_select(scores, qi_ht, wi_ht, ki_new, ih, ts, topk):
    bs, _, past = scores.shape
    tail_w = ki_new.shape[1]
    r = ih * SAMPLE_TPAD
    return pl.pallas_call(
        functools.partial(_dsa_sample_select_kernel, topk=topk, ih=ih, ts=ts, past=past),
        grid=(bs,),
        in_specs=[pl.BlockSpec((1, SAMPLE_TPAD, past), lambda b: (b, 0, 0)),
                  pl.BlockSpec((1, r, IDX_DIM), lambda b: (b, 0, 0)),
                  pl.BlockSpec((1, r, wi_ht.shape[2]), lambda b: (b, 0, 0)),
                  pl.BlockSpec((1, tail_w, IDX_DIM), lambda b: (b, 0, 0))],
        out_specs=pl.BlockSpec((1, SAMPLE_TPAD, past + tail_w), lambda b: (b, 0, 0)),
        out_shape=jax.ShapeDtypeStruct((bs, SAMPLE_TPAD, past + tail_w), F32),
        compiler_params=_cparams("parallel"),
        name="dsa_sample_select",
    )(scores, qi_ht, wi_ht, ki_new)


def _bias_expander(nkv, page):
    cols = jnp.arange(page * nkv)
    per_key = cols[None, :] // nkv == jnp.arange(page)[:, None]
    per_head = cols[None, :] % nkv == jnp.arange(128)[:, None]
    return jnp.concatenate([per_key, per_head], axis=0).astype(BF16)


def _head_mask_lanes(nkv):
    n_of_row = jnp.arange(nkv * GROUP * SAMPLE_TPAD) // (GROUP * SAMPLE_TPAD)
    lane = jnp.arange(128)
    off_head = jnp.logical_and(lane[None, :] < nkv, lane[None, :] != n_of_row[:, None])
    return jnp.where(off_head, NEG, 0.0).astype(BF16)


def _sample_flash_update(q, k_pages, v_pages, key_bias_pages, hm, expander, m_ref, l_ref, acc_ref, width=None):
    npages = len(k_pages)
    r = q.shape[0]
    lhs = jnp.concatenate([jnp.concatenate([kb.astype(BF16), hm], axis=1) for kb in key_bias_pages], axis=0)
    bias_all = jnp.dot(lhs, expander, preferred_element_type=F32)
    scale = HEAD_DIM ** -0.5
    parts = []
    for c in range(npages):
        bias = bias_all[c * r:(c + 1) * r]
        if width is not None:
            bias = bias[:, :width]
        parts.append(lax.dot_general(q, k_pages[c], _CONTRACT_LAST, preferred_element_type=F32) * scale + bias)
    mx = parts[0]
    for c in range(1, npages):
        mx = jnp.maximum(mx, parts[c])
    m_old = m_ref[...]
    m_new = jnp.maximum(m_old, jnp.max(mx, axis=1, keepdims=True))
    alpha = jnp.exp(m_old - m_new)
    lsum = None
    o = None
    for c in range(npages):
        p = jnp.exp(parts[c] - m_new)
        pv = jnp.dot(p.astype(BF16), v_pages[c], preferred_element_type=F32)
        lsum = p if lsum is None else lsum + p
        o = pv if o is None else o + pv
    l_ref[...] = alpha * l_ref[...] + jnp.sum(lsum, axis=1, keepdims=True)
    m_ref[...] = m_new
    acc_ref[...] = alpha * acc_ref[...] + o


def _sample_flash_reset(m_ref, l_ref, acc_ref):
    m_ref[...] = jnp.full(m_ref.shape, NEG, F32)
    l_ref[...] = jnp.zeros(l_ref.shape, F32)
    acc_ref[...] = jnp.zeros(acc_ref.shape, F32)


def _flat_pages(refs):
    return [r[...].reshape(-1, HEAD_DIM).astype(BF16) for r in refs]


def _tile_query_bias(b8, reps):
    return jnp.concatenate([b8] * reps, axis=0)


def _dsa_sample_attn_kernel(pt_ref, q_ref, bias_ref, tbias_ref, kn_ref, vn_ref, hm_ref, ex_ref, *refs,
                            pp, nkv, page):
    k_refs, v_refs = refs[:pp], refs[pp:2 * pp]
    o_ref, m_ref, l_ref, acc_ref = refs[2 * pp:]
    p = pl.program_id(1)
    reps = nkv * GROUP
    hm = hm_ref[...]

    @pl.when(p == 0)
    def _():
        _sample_flash_reset(m_ref, l_ref, acc_ref)

    key_bias = [_tile_query_bias(bias_ref[0, :, c * page:(c + 1) * page], reps) for c in range(pp)]
    _sample_flash_update(q_ref[0], _flat_pages(k_refs), _flat_pages(v_refs), key_bias, hm, ex_ref[...],
                         m_ref, l_ref, acc_ref)

    @pl.when(p == pl.num_programs(1) - 1)
    def _():
        _sample_flash_update(q_ref[0], [kn_ref[0].astype(BF16)], [vn_ref[0].astype(BF16)],
                             [_tile_query_bias(tbias_ref[0], reps)], hm, ex_ref[...], m_ref, l_ref, acc_ref,
                             width=kn_ref.shape[1])
        o_ref[0] = acc_ref[...] / l_ref[...]


def _sample_attn_specs(bs, npg, page, nkv, pp, tail_rows):
    rows = nkv * GROUP * SAMPLE_TPAD
    q_spec = pl.BlockSpec((1, rows, HEAD_DIM), lambda b, p, pt: (b, 0, 0))
    tail_spec = pl.BlockSpec((1, tail_rows, HEAD_DIM), lambda b, p, pt: (b, 0, 0))
    hm_spec = pl.BlockSpec((rows, 128), lambda b, p, pt: (0, 0))
    ex_spec = pl.BlockSpec((page + 128, page * nkv), lambda b, p, pt: (0, 0))
    out_spec = pl.BlockSpec((1, rows, HEAD_DIM), lambda b, p, pt: (b, 0, 0))
    scratch = [pltpu.VMEM((rows, 1), F32), pltpu.VMEM((rows, 1), F32), pltpu.VMEM((rows, HEAD_DIM), F32)]
    return rows, q_spec, tail_spec, hm_spec, ex_spec, out_spec, scratch


def _dsa_sample_attn(page_table, q_rows, bias, k_new, v_new, k_pool, v_pool, layer, nkv):
    bs, npg = page_table.shape
    page = k_pool.shape[2]
    pp = SAMPLE_PP
    rows, q_spec, tail_spec, hm_spec, ex_spec, out_spec, scratch = _sample_attn_specs(
        bs, npg, page, nkv, pp, k_new.shape[1])
    grid_spec = pltpu.PrefetchScalarGridSpec(
        num_scalar_prefetch=1,
        grid=(bs, npg // pp),
        in_specs=[q_spec,
                  pl.BlockSpec((1, SAMPLE_TPAD, page * pp), lambda b, p, pt: (b, 0, p)),
                  pl.BlockSpec((1, SAMPLE_TPAD, page), lambda b, p, pt: (b, 0, npg)),
                  tail_spec, tail_spec, hm_spec, ex_spec]
        + _page_specs((page, nkv, HEAD_DIM), pp, layer) + _page_specs((page, nkv, HEAD_DIM), pp, layer),
        out_specs=out_spec,
        scratch_shapes=scratch,
    )
    return pl.pallas_call(
        functools.partial(_dsa_sample_attn_kernel, pp=pp, nkv=nkv, page=page),
        grid_spec=grid_spec,
        out_shape=jax.ShapeDtypeStruct((bs, rows, HEAD_DIM), F32),
        compiler_params=_cparams("parallel", "arbitrary"),
        name="dsa_sample_attn",
    )(page_table, q_rows, bias, bias, k_new, v_new, _head_mask_lanes(nkv), _bias_expander(nkv, page),
      *([k_pool] * pp), *([v_pool] * pp))


KMEAN_BLOCKS = 8


def _moba_kmean_kernel(pt_ref, *refs, nblk, ppb):
    k_refs, o_ref = refs[:nblk * ppb], refs[nblk * ppb]
    for blk in range(nblk):
        acc = jnp.sum(k_refs[blk * ppb][...], axis=0)
        for c in range(1, ppb):
            acc = acc + jnp.sum(k_refs[blk * ppb + c][...], axis=0)
        o_ref[0, blk] = acc * (1.0 / MOBA_BLOCK)


def _moba_sample_kmean(page_table, k_pool, layer, nkv):
    bs, npg = page_table.shape
    page = k_pool.shape[2]
    ppb = MOBA_BLOCK // page
    nb = npg // ppb
    nblk = KMEAN_BLOCKS
    grid_spec = pltpu.PrefetchScalarGridSpec(
        num_scalar_prefetch=1,
        grid=(bs, nb // nblk),
        in_specs=_page_specs((page, nkv, HEAD_DIM), nblk * ppb, layer),
        out_specs=pl.BlockSpec((1, nblk, nkv, HEAD_DIM), lambda b, p, pt: (b, p, 0, 0)),
    )
    return pl.pallas_call(
        functools.partial(_moba_kmean_kernel, nblk=nblk, ppb=ppb),
        grid_spec=grid_spec,
        out_shape=jax.ShapeDtypeStruct((bs, nb, nkv, HEAD_DIM), F32),
        compiler_params=_cparams("parallel", "arbitrary"),
        name="moba_sample_kmean",
    )(page_table, *([k_pool] * (nblk * ppb)))


def _moba_sample_attn_kernel(pt_ref, q_ref, qbd_ref, km_ref, kn_ref, vn_ref, hm_ref, ex_ref, *refs,
                             pp, nkv, page, nb, ts):
    k_refs, v_refs = refs[:pp], refs[pp:2 * pp]
    o_ref, m_ref, l_ref, acc_ref, sb_ref = refs[2 * pp:]
    p = pl.program_id(1)
    rows = nkv * GROUP * SAMPLE_TPAD
    ppb = MOBA_BLOCK // page
    blk_iota = lax.broadcasted_iota(I32, (rows, nb), 1)
    hm = hm_ref[...]

    @pl.when(p == 0)
    def _():
        _sample_flash_reset(m_ref, l_ref, acc_ref)
        valid = blk_iota >= 0
        gate = lax.dot_general(qbd_ref[0], km_ref[0].astype(BF16), _CONTRACT_LAST, preferred_element_type=F32)
        sb_ref[...] = _topb_select_bias(gate, valid, blk_iota, nb)

    selb = sb_ref[...]
    key_bias = []
    for c in range(pp):
        blk = p * (pp // ppb) + c // ppb
        if c % ppb == 0:
            col = jnp.sum(jnp.where(blk_iota == blk, selb, 0.0), axis=1, keepdims=True)
            col_b = jnp.broadcast_to(col, (rows, 128))
        key_bias.append(col_b)
    _sample_flash_update(q_ref[0], _flat_pages(k_refs), _flat_pages(v_refs), key_bias, hm, ex_ref[...],
                         m_ref, l_ref, acc_ref)

    @pl.when(p == pl.num_programs(1) - 1)
    def _():
        row = lax.broadcasted_iota(I32, (rows, 128), 0) & (SAMPLE_TPAD - 1)
        col = lax.broadcasted_iota(I32, (rows, 128), 1)
        tb = jnp.where(jnp.logical_and(col <= row, col < ts), 0.0, NEG)
        _sample_flash_update(q_ref[0], [kn_ref[0].astype(BF16)], [vn_ref[0].astype(BF16)], [tb], hm, ex_ref[...],
                             m_ref, l_ref, acc_ref, width=kn_ref.shape[1])
        o_ref[0] = acc_ref[...] / l_ref[...]


def _moba_sample_attn(page_table, q_rows, qbd, kmean, k_new, v_new, k_pool, v_pool, layer, nkv, ts):
    bs, npg = page_table.shape
    page = k_pool.shape[2]
    kvw = nkv * HEAD_DIM
    pp = SAMPLE_PP
    nb = kmean.shape[1]
    rows, q_spec, tail_spec, hm_spec, ex_spec, out_spec, scratch = _sample_attn_specs(
        bs, npg, page, nkv, pp, k_new.shape[1])
    grid_spec = pltpu.PrefetchScalarGridSpec(
        num_scalar_prefetch=1,
        grid=(bs, npg // pp),
        in_specs=[q_spec,
                  pl.BlockSpec((1, rows, kvw), lambda b, p, pt: (b, 0, 0)),
                  pl.BlockSpec((1, nb, kvw), lambda b, p, pt: (b, 0, 0)),
                  tail_spec, tail_spec, hm_spec, ex_spec]
        + _page_specs((page, nkv, HEAD_DIM), pp, layer) + _page_specs((page, nkv, HEAD_DIM), pp, layer),
        out_specs=out_spec,
        scratch_shapes=scratch + [pltpu.VMEM((rows, nb), F32)],
    )
    return pl.pallas_call(
        functools.partial(_moba_sample_attn_kernel, pp=pp, nkv=nkv, page=page, nb=nb, ts=ts),
        grid_spec=grid_spec,
        out_shape=jax.ShapeDtypeStruct((bs, rows, HEAD_DIM), F32),
        compiler_params=_cparams("parallel", "arbitrary"),
        name="moba_sample_attn",
    )(page_table, q_rows, qbd, kmean, k_new, v_new, _head_mask_lanes(nkv), _bias_expander(nkv, page),
      *([k_pool] * pp), *([v_pool] * pp))


def _heads_to_kv_major(q, bs, nkv):
    x = q.reshape(bs, SAMPLE_TPAD, nkv, GROUP, HEAD_DIM).transpose(0, 2, 3, 1, 4)
    eye = jnp.eye(nkv, dtype=q.dtype)
    bd = x[:, :, :, :, None, :] * eye[None, :, None, None, :, None]
    return bd.reshape(bs, nkv * GROUP * SAMPLE_TPAD, nkv * HEAD_DIM).astype(BF16)


def _sample_q_rows(q, bs, nkv):
    x = q.reshape(bs, SAMPLE_TPAD, nkv, GROUP, HEAD_DIM).transpose(0, 2, 3, 1, 4)
    return x.reshape(bs, nkv * GROUP * SAMPLE_TPAD, HEAD_DIM).astype(BF16)


def _new_kv_flat(x, bs, nkv):
    x = x.reshape(bs, SAMPLE_TPAD, nkv, HEAD_DIM)
    x = jnp.pad(x, ((0, 0), (0, SAMPLE_TPAD), (0, 0), (0, 0)))
    return x.reshape(bs, 2 * SAMPLE_TPAD * nkv, HEAD_DIM)


def _kv_major_to_rows(o, bs, nkv):
    x = o.reshape(bs, nkv, GROUP, SAMPLE_TPAD, HEAD_DIM)
    return x.transpose(0, 3, 1, 2, 4).reshape(bs * SAMPLE_TPAD, nkv * GROUP * HEAD_DIM)


def _pad_rows(x, n):
    return jnp.pad(x, ((0, 0), (0, n - x.shape[1]), (0, 0)))


def _finish_layer(h, mix, qm, mk, mv, bsz, layer, w_o, g_ffn, w_up, w_down):
    m = h.shape[0]
    ma = _mem_attn(qm.reshape(bsz, m // bsz, -1), mk, mv, qm.dtype).reshape(m, -1)
    merged = jnp.concatenate([mix.astype(BF16), ma.astype(BF16)], axis=1)
    h = _matmul(merged, w_o, layer, F32, epilogue="residual", residual=h)
    u = _matmul(_rmsnorm(h, g_ffn, BF16), w_up, layer, BF16, epilogue="relu2")
    return _matmul(u, w_down, layer, F32, epilogue="residual", residual=h)


def kernel(x_prompt, x_sample, cache_dsa_k, cache_dsa_v, cache_dsa_kidx, cache_moba_k, cache_moba_v, cache_mem_k, cache_mem_v, page_table, mem_prompt, norm_mix, norm_mem, w_in_dsa, w_in_moba, w_mem_kv, w_out, norm_ffn, w_up, w_down, norm_final):
    bsz, t, d = x_prompt.shape
    bs, ts, _ = x_sample.shape
    depth = norm_mix.shape[0]
    nh = (3 * d) // (4 * HEAD_DIM)
    nkv = nh // GROUP
    qw, kvw = nh * HEAD_DIM, nkv * HEAD_DIM
    memw = d // 4
    mhd = memw // MEM_HEADS
    ih = d // 128
    npg = page_table.shape[1]
    page = cache_dsa_k.shape[2]
    past = npg * page
    mlen = mem_prompt.shape[1]
    tp = SAMPLE_TPAD

    hp = x_prompt.reshape(bsz * t, d)
    hs = jnp.pad(x_sample, ((0, 0), (0, tp - ts), (0, 0))).reshape(bs * tp, d)
    pos_p = jnp.arange(t, dtype=I32)
    pos_s = past + (jnp.arange(tp, dtype=I32) % ts)
    tabs_p = _rope_tables(pos_p)
    tabs_s = _rope_tables(jnp.tile(pos_s, bs))

    outs = {name: [] for name in ("pdk", "pdv", "pdki", "sdk", "sdv", "sdki", "pmk", "pmv", "smk", "smv", "mk", "mv")}
    mem_flat = mem_prompt.reshape(bsz * mlen, d)

    w_o, wu, wd, w_mem = w_out.astype(BF16), w_up.astype(BF16), w_down.astype(BF16), w_mem_kv.astype(BF16)
    w_dsa, w_moba = w_in_dsa.astype(BF16), w_in_moba.astype(BF16)
    c0 = qw + 2 * kvw + ih * IDX_DIM
    w_dsa_tail = jnp.concatenate(
        [w_in_dsa[:, :, c0 + ih:c0 + ih + IDX_DIM], w_in_dsa[:, :, c0 + ih + IDX_DIM:], w_in_dsa[:, :, c0:c0 + ih],
         jnp.zeros(w_in_dsa.shape[:2] + (128 - ih,), F32)], axis=2).astype(BF16)

    for i in range(depth):
        j = i // 2
        mkv = _matmul(_rmsnorm(mem_flat, norm_mem[i], BF16), w_mem, i, F32)
        mk_p = mkv[:, :memw].reshape(bsz, mlen, memw)
        mv_p = mkv[:, memw:].reshape(bsz, mlen, memw)
        outs["mk"].append(mk_p.reshape(bsz, mlen, MEM_HEADS, mhd))
        outs["mv"].append(mv_p.reshape(bsz, mlen, MEM_HEADS, mhd))
        hn_p = _rmsnorm(hp, norm_mix[i], BF16)
        hn_s = _rmsnorm(hs, norm_mix[i], BF16)
        mk_s = cache_mem_k[i].reshape(bs, mlen, memw)
        mv_s = cache_mem_v[i].reshape(bs, mlen, memw)
        if i % 2 == 0:
            pa = _matmul(hn_p, w_dsa, j, F32, n=c0)
            pb = _matmul(hn_p, w_dsa_tail, j, F32)
            q, kf, kb, vf, vb, qi, kif, kib, wi, qm = _dsa_split(pa, pb, tabs_p, t, nh, nkv, ih, memw, BF16)
            outs["pdk"].append(kf.reshape(bsz, t, nkv, HEAD_DIM))
            outs["pdv"].append(vf.reshape(bsz, t, nkv, HEAD_DIM))
            outs["pdki"].append(kif.reshape(bsz, t, IDX_DIM))
            mix_p = _dsa_prompt(qi, wi, kib, q, kb, vb, bsz, t, nkv, ih, min(DSA_TOPK, t // 4))
            pa = _matmul(hn_s, w_dsa, j, F32, n=c0)
            pb = _matmul(hn_s, w_dsa_tail, j, F32)
            q_s, kf, _, vf, _, qi_s, kif, _, wi_s, qm_s = _dsa_split(pa, pb, tabs_s, bs * tp, nh, nkv, ih, memw, F32)
            outs["sdk"].append(kf.reshape(bs, tp, nkv, HEAD_DIM)[:, :ts])
            outs["sdv"].append(vf.reshape(bs, tp, nkv, HEAD_DIM)[:, :ts])
            outs["sdki"].append(kif.reshape(bs, tp, IDX_DIM)[:, :ts])
            qi_ht = qi_s.reshape(bs, tp, ih, IDX_DIM).transpose(0, 2, 1, 3).reshape(bs, ih * tp, IDX_DIM)
            wi_ht = wi_s.reshape(bs, tp, 128)[:, :, :ih].transpose(0, 2, 1).reshape(bs, ih * tp, 1)
            wi_ht = jnp.broadcast_to(wi_ht, (bs, ih * tp, page))
            scores = _dsa_sample_index(page_table, qi_ht, wi_ht, cache_dsa_kidx, j, ih)
            ki_new = _pad_rows(kif.reshape(bs, tp, IDX_DIM), page)
            bias = _dsa_sample_select(scores, qi_ht, wi_ht, ki_new, ih, ts, min(DSA_TOPK, (past + ts) // 4))
            o_s = _dsa_sample_attn(page_table, _sample_q_rows(q_s, bs, nkv), bias, _new_kv_flat(kf, bs, nkv),
                                   _new_kv_flat(vf, bs, nkv), cache_dsa_k, cache_dsa_v, j, nkv)
            mix_s = _kv_major_to_rows(o_s, bs, nkv)
        else:
            p = _matmul(hn_p, w_moba, j, F32)
            q, kf, kb, vf, vb, qm, kmean = _moba_split(p, tabs_p, t, nh, nkv, memw, BF16)
            outs["pmk"].append(kf.reshape(bsz, t, nkv, HEAD_DIM))
            outs["pmv"].append(vf.reshape(bsz, t, nkv, HEAD_DIM))
            mix_p = _moba_prompt(q, kb, vb, kmean, bsz, t, nkv)
            p = _matmul(hn_s, w_moba, j, F32)
            q_s, kf, _, vf, _, qm_s, _ = _moba_split(p, tabs_s, bs * tp, nh, nkv, memw, F32)
            outs["smk"].append(kf.reshape(bs, tp, nkv, HEAD_DIM)[:, :ts])
            outs["smv"].append(vf.reshape(bs, tp, nkv, HEAD_DIM)[:, :ts])
            kmean_s = _moba_sample_kmean(page_table, cache_moba_k, j, nkv)
            kmean_s = kmean_s.reshape(bs, kmean_s.shape[1], kvw)
            o_s = _moba_sample_attn(page_table, _sample_q_rows(q_s, bs, nkv), _heads_to_kv_major(q_s, bs, nkv),
                                    kmean_s, _new_kv_flat(kf, bs, nkv), _new_kv_flat(vf, bs, nkv),
                                    cache_moba_k, cache_moba_v, j, nkv, ts)
            mix_s = _kv_major_to_rows(o_s, bs, nkv)
        hp = _finish_layer(hp, mix_p, qm, mk_p, mv_p, bsz, i, w_o, norm_ffn[i], wu, wd)
        hs = _finish_layer(hs, mix_s, qm_s, mk_s, mv_s, bs, i, w_o, norm_ffn[i], wu, wd)

    y_prompt = _rmsnorm(hp, norm_final, F32).reshape(bsz, t, d)
    y_sample = _rmsnorm(hs, norm_final, F32).reshape(bs, tp, d)[:, :ts]
    st = jnp.stack
    return (y_prompt, y_sample, st(outs["pdk"]), st(outs["pdv"]), st(outs["pdki"]), st(outs["pmk"]), st(outs["pmv"]),
            st(outs["mk"]), st(outs["mv"]), st(outs["sdk"]), st(outs["sdv"]), st(outs["sdki"]),
            st(outs["smk"]), st(outs["smv"]))
```

```python
import functools

import jax
import jax.numpy as jnp
from jax import lax
from jax.experimental import pallas as pl
from jax.experimental.pallas import tpu as pltpu

F32 = jnp.float32
BF16 = jnp.bfloat16
I32 = jnp.int32

HEAD_DIM = 128
GROUP = 3
ROT_DIM = 32
ROPE_THETA = 500000.0
IDX_DIM = 128
MEM_HEADS = 4
DSA_TOPK = 256
MOBA_BLOCK = 256
MOBA_TOPB = 3
EPS = 1e-6
NEG = -1e30
LOG2E = 1.4426950408889634
SOFTMAX_C2 = (HEAD_DIM ** -0.5) * LOG2E
QTILE = 256
SAMPLE_TPAD = 8
VMEM_LIMIT = 60 * 1024 * 1024

_CONTRACT_LAST = (((1,), (1,)), ((), ()))


def _cparams(*sem):
    return pltpu.CompilerParams(dimension_semantics=sem, vmem_limit_bytes=VMEM_LIMIT)


def _pick_block(n, pref, align=128):
    if n <= pref:
        return n
    b = (pref // align) * align
    while b >= align:
        if n % b == 0:
            return b
        b -= align
    return n


def _rmsnorm_kernel(x_ref, g_ref, o_ref):
    x = x_ref[...]
    ms = jnp.mean(x * x, axis=-1, keepdims=True)
    o_ref[...] = ((x * lax.rsqrt(ms + EPS)) * g_ref[...]).astype(o_ref.dtype)


def _rmsnorm(x, g, out_dtype):
    m, d = x.shape
    bm = _pick_block(m, 256, 8)
    return pl.pallas_call(
        _rmsnorm_kernel,
        grid=(m // bm,),
        in_specs=[pl.BlockSpec((bm, d), lambda i: (i, 0)), pl.BlockSpec((1, d), lambda i: (0, 0))],
        out_specs=pl.BlockSpec((bm, d), lambda i: (i, 0)),
        out_shape=jax.ShapeDtypeStruct((m, d), out_dtype),
        compiler_params=_cparams("parallel"),
        name="rmsnorm",
    )(x, g.reshape(1, d).astype(F32))


def _mm_kernel(*refs, nk, epilogue):
    if epilogue == "residual":
        a_ref, w_ref, r_ref, o_ref = refs
    else:
        a_ref, w_ref, o_ref = refs
        r_ref = None
    k = pl.program_id(2)
    part = jnp.dot(a_ref[...], w_ref[...], preferred_element_type=F32)

    def finish(acc):
        if epilogue == "relu2":
            r = jnp.maximum(acc, 0.0)
            acc = r * r
        elif epilogue == "residual":
            acc = r_ref[...] + acc
        o_ref[...] = acc.astype(o_ref.dtype)

    if nk == 1:
        finish(part)
        return

    @pl.when(k == 0)
    def _():
        o_ref[...] = part

    @pl.when(jnp.logical_and(k > 0, k < nk - 1))
    def _():
        o_ref[...] += part

    @pl.when(k == nk - 1)
    def _():
        finish(o_ref[...] + part)


def _matmul(a, w, layer, out_dtype, epilogue=None, residual=None, n=None, bm_pref=1024, bn_pref=1024, bk_pref=4096):
    m, kdim = a.shape
    n = w.shape[2] if n is None else n
    bm = _pick_block(m, bm_pref, 8)
    bn = _pick_block(n, bn_pref)
    bk = _pick_block(kdim, bk_pref)
    nk = kdim // bk
    assert nk == 1 or out_dtype == F32, "a split contraction accumulates in the f32 output block"
    in_specs = [pl.BlockSpec((bm, bk), lambda j, i, k: (i, k)),
                pl.BlockSpec((None, bk, bn), lambda j, i, k: (layer, k, j))]
    args = [a, w]
    if epilogue == "residual":
        in_specs.append(pl.BlockSpec((bm, bn), lambda j, i, k: (i, j)))
        args.append(residual)
    return pl.pallas_call(
        functools.partial(_mm_kernel, nk=nk, epilogue=epilogue),
        grid=(n // bn, m // bm, nk),
        in_specs=in_specs,
        out_specs=pl.BlockSpec((bm, bn), lambda j, i, k: (i, j)),
        out_shape=jax.ShapeDtypeStruct((m, n), out_dtype),
        compiler_params=_cparams("parallel", "parallel", "arbitrary"),
        name="matmul",
    )(*args)


def _rope_tables(pos):
    half = ROT_DIM // 2
    inv = ROPE_THETA ** (-jnp.arange(half, dtype=F32) / half)
    ang = pos.astype(F32)[:, None] * inv[None, :]
    cos, sin = jnp.cos(ang), jnp.sin(ang)
    r = pos.shape[0]
    z16 = jnp.zeros((r, half), F32)
    zrest = jnp.zeros((r, HEAD_DIM - ROT_DIM), F32)
    c = jnp.concatenate([cos, cos, jnp.ones((r, HEAD_DIM - ROT_DIM), F32)], axis=1)
    sm = jnp.concatenate([-sin, z16, zrest], axis=1)
    sp = jnp.concatenate([z16, sin, zrest], axis=1)
    return c, sm, sp


def _rope_head(x, c, sm, sp):
    half = ROT_DIM // 2
    return x * c + pltpu.roll(x, HEAD_DIM - half, 1) * sm + pltpu.roll(x, half, 1) * sp


def _dsa_split_kernel(pa_ref, pb_ref, c_ref, sm_ref, sp_ref,
                      q_ref, kf_ref, kb_ref, vf_ref, vb_ref, qi_ref, kif_ref, kib_ref, wi_ref, qm_ref,
                      *, nh, nkv, ih, memw):
    c, sm, sp = c_ref[...], sm_ref[...], sp_ref[...]
    qw, kvw = nh * HEAD_DIM, nkv * HEAD_DIM
    for h in range(nh):
        sl = slice(h * HEAD_DIM, (h + 1) * HEAD_DIM)
        q_ref[:, sl] = _rope_head(pa_ref[:, sl], c, sm, sp).astype(q_ref.dtype)
    for h in range(nkv):
        sl = slice(h * HEAD_DIM, (h + 1) * HEAD_DIM)
        kr = _rope_head(pa_ref[:, qw + h * HEAD_DIM: qw + (h + 1) * HEAD_DIM], c, sm, sp)
        kf_ref[:, sl] = kr
        kb_ref[:, sl] = kr.astype(kb_ref.dtype)
    v = pa_ref[:, qw + kvw: qw + 2 * kvw]
    vf_ref[...] = v
    vb_ref[...] = v.astype(vb_ref.dtype)
    base = qw + 2 * kvw
    for h in range(ih):
        sl = slice(h * IDX_DIM, (h + 1) * IDX_DIM)
        qi_ref[:, sl] = _rope_head(pa_ref[:, base + h * IDX_DIM: base + (h + 1) * IDX_DIM], c, sm, sp).astype(qi_ref.dtype)
    ki = _rope_head(pb_ref[:, 0:IDX_DIM], c, sm, sp)
    kif_ref[...] = ki
    kib_ref[...] = ki.astype(kib_ref.dtype)
    qm_ref[...] = pb_ref[:, IDX_DIM:IDX_DIM + memw].astype(qm_ref.dtype)
    wi_ref[...] = pb_ref[:, IDX_DIM + memw:IDX_DIM + memw + 128] * ((ih * IDX_DIM) ** -0.5)


def _dsa_split(pa, pb, tabs, rows_per_seq, nh, nkv, ih, memw, act_dtype):
    m = pa.shape[0]
    bm = _pick_block(rows_per_seq, 128, 8)
    nt = rows_per_seq // bm
    qw, kvw = nh * HEAD_DIM, nkv * HEAD_DIM
    row = lambda w: pl.BlockSpec((bm, w), lambda i: (i, 0))
    tab = pl.BlockSpec((bm, HEAD_DIM), lambda i: (i % nt, 0))
    shapes = [(qw, act_dtype), (kvw, F32), (kvw, act_dtype), (kvw, F32), (kvw, act_dtype),
              (ih * IDX_DIM, act_dtype), (IDX_DIM, F32), (IDX_DIM, act_dtype), (128, F32), (memw, act_dtype)]
    return pl.pallas_call(
        functools.partial(_dsa_split_kernel, nh=nh, nkv=nkv, ih=ih, memw=memw),
        grid=(m // bm,),
        in_specs=[row(pa.shape[1]), row(pb.shape[1]), tab, tab, tab],
        out_specs=[row(w) for w, _ in shapes],
        out_shape=[jax.ShapeDtypeStruct((m, w), dt) for w, dt in shapes],
        compiler_params=_cparams("parallel"),
        name="dsa_split",
    )(pa, pb, *tabs)


def _moba_split_kernel(p_ref, c_ref, sm_ref, sp_ref, q_ref, kf_ref, kb_ref, vf_ref, vb_ref, qm_ref, km_ref,
                       *, nh, nkv, memw):
    c, sm, sp = c_ref[...], sm_ref[...], sp_ref[...]
    qw, kvw = nh * HEAD_DIM, nkv * HEAD_DIM
    for h in range(nh):
        sl = slice(h * HEAD_DIM, (h + 1) * HEAD_DIM)
        q_ref[:, sl] = _rope_head(p_ref[:, sl], c, sm, sp).astype(q_ref.dtype)
    rows = p_ref.shape[0]
    for h in range(nkv):
        sl = slice(h * HEAD_DIM, (h + 1) * HEAD_DIM)
        kr = _rope_head(p_ref[:, qw + h * HEAD_DIM: qw + (h + 1) * HEAD_DIM], c, sm, sp)
        kf_ref[:, sl] = kr
        kb_ref[:, sl] = kr.astype(kb_ref.dtype)
        km_ref[0, :, sl] = jnp.sum(kr, axis=0, keepdims=True) * (1.0 / rows)
    v = p_ref[:, qw + kvw: qw + 2 * kvw]
    vf_ref[...] = v
    vb_ref[...] = v.astype(vb_ref.dtype)
    qm_ref[...] = p_ref[:, qw + 2 * kvw: qw + 2 * kvw + memw].astype(qm_ref.dtype)


def _moba_split(p, tabs, rows_per_seq, nh, nkv, memw, act_dtype):
    m = p.shape[0]
    bm = _pick_block(rows_per_seq, MOBA_BLOCK, 8)
    nt = rows_per_seq // bm
    qw, kvw = nh * HEAD_DIM, nkv * HEAD_DIM
    row = lambda w: pl.BlockSpec((bm, w), lambda i: (i, 0))
    tab = pl.BlockSpec((bm, HEAD_DIM), lambda i: (i % nt, 0))
    shapes = [(qw, act_dtype), (kvw, F32), (kvw, act_dtype), (kvw, F32), (kvw, act_dtype), (memw, act_dtype)]
    return pl.pallas_call(
        functools.partial(_moba_split_kernel, nh=nh, nkv=nkv, memw=memw),
        grid=(m // bm,),
        in_specs=[row(p.shape[1]), tab, tab, tab],
        out_specs=[row(w) for w, _ in shapes] + [pl.BlockSpec((1, 1, kvw), lambda i: (i, 0, 0))],
        out_shape=[jax.ShapeDtypeStruct((m, w), dt) for w, dt in shapes]
        + [jax.ShapeDtypeStruct((m // bm, 1, kvw), F32)],
        compiler_params=_cparams("parallel"),
        name="moba_split",
    )(p, *tabs)


def _sortable_key(x):
    bits = pltpu.bitcast(x, I32)
    return jnp.where(bits < 0, bits ^ jnp.int32(0x7FFFFFFF), bits)


def _kth_largest_key(count_ge, shape, k):
    imin = jnp.int32(-2 ** 31)
    c0 = count_ge(jnp.zeros(shape, I32))
    thr = jnp.where(c0 >= k, jnp.int32(0), imin)

    def bit_body(it, thr):
        cand = thr + jnp.left_shift(jnp.int32(1), jnp.int32(30) - it)
        return jnp.where(count_ge(cand) >= k, cand, thr)

    return lax.fori_loop(0, 31, bit_body, thr)


def _lane_fold(x, acc, op):
    for w in range(x.shape[1] // 128):
        acc = op(acc, x[:, w * 128:(w + 1) * 128])
    return acc


def _paired_loop(n, body, carry):
    def pair(c2, carry):
        return body(2 * c2 + 1, body(2 * c2, carry))
    npair = n // 2
    carry = lax.fori_loop(0, npair, pair, carry)
    return lax.fori_loop(2 * npair, n, body, carry)


def _masked_attention(rows, n_dyn, logits_of, load_v, s_ref, own=None):
    mx = jnp.full((rows, 128), NEG, F32)
    if own is not None:
        slot, own_s, own_v = own
        s_ref[slot] = own_s
        mx = _lane_fold(own_s, mx, jnp.maximum)

    def pass_a(c, mx):
        s = logits_of(c)
        s_ref[c] = s
        return _lane_fold(s, mx, jnp.maximum)

    mx = _paired_loop(n_dyn, pass_a, mx)
    m = jnp.max(mx, axis=1, keepdims=True)

    def accumulate(c, vb, carry):
        lsum, acc = carry
        p = jnp.exp2(s_ref[c] - m)
        return _lane_fold(p, lsum, jnp.add), acc + jnp.dot(p.astype(BF16), vb, preferred_element_type=F32)

    carry = (jnp.zeros((rows, 128), F32), jnp.zeros((rows, HEAD_DIM), F32))
    if own is not None:
        carry = accumulate(slot, own_v, carry)
    lsum, acc = _paired_loop(n_dyn, lambda c, carry: accumulate(c, load_v(c), carry), carry)
    return acc / jnp.sum(lsum, axis=1, keepdims=True)


def _dsa_prompt_kernel(qi_ref, wi_ref, ki_ref, q_ref, k_ref, v_ref, o_ref, key_ref, bias_ref, s_ref,
                       *, topk, nkv, ih):
    i = pl.program_id(1)
    tq = ck = QTILE
    key_i = lax.broadcasted_iota(I32, (ck, tq), 0)
    qry_i = lax.broadcasted_iota(I32, (ck, tq), 1)
    wi_t = wi_ref[...].T

    def causal_ok(kc):
        return key_i <= qry_i + jnp.where(kc < i, ck, 0)

    def idx_body(kc, _):
        off = pl.multiple_of(kc * ck, ck)
        kic = ki_ref[0, pl.ds(off, ck), :]
        acc = jnp.zeros((ck, tq), F32)
        for h in range(ih):
            s = lax.dot_general(kic, qi_ref[:, h * IDX_DIM:(h + 1) * IDX_DIM], _CONTRACT_LAST,
                                preferred_element_type=F32)
            acc = acc + jnp.maximum(s, 0.0) * wi_t[h:h + 1, :]
        key_ref[kc] = _sortable_key(jnp.where(causal_ok(kc), acc, -jnp.inf))
        return 0

    lax.fori_loop(0, i + 1, idx_body, 0)

    def count_ge(cand):
        def body(kc, acc):
            ge = jnp.where(key_ref[kc] >= cand, 1.0, 0.0)
            return acc + jnp.sum(ge.reshape(ck // 8, 8, tq), axis=0)
        acc = lax.fori_loop(0, i + 1, body, jnp.zeros((8, tq), F32))
        return jnp.sum(acc, axis=0, keepdims=True)

    thr = _kth_largest_key(count_ge, (1, tq), topk)

    def bias_body(kc, _):
        sel = jnp.logical_and(key_ref[kc] >= thr, causal_ok(kc))
        bias_ref[kc] = jnp.where(sel, 0.0, NEG).T
        return 0

    lax.fori_loop(0, i + 1, bias_body, 0)

    for n in range(nkv):
        lanes = slice(n * HEAD_DIM, (n + 1) * HEAD_DIM)
        q3 = jnp.concatenate(
            [q_ref[:, (n * GROUP + g) * HEAD_DIM:(n * GROUP + g + 1) * HEAD_DIM] for g in range(GROUP)], axis=0)
        chunk = lambda ref, kc, lanes=lanes: ref[0, pl.ds(pl.multiple_of(kc * ck, ck), ck), lanes]

        def logits_of(kc, q3=q3, chunk=chunk):
            s = lax.dot_general(q3, chunk(k_ref, kc), _CONTRACT_LAST, preferred_element_type=F32)
            return s * SOFTMAX_C2 + jnp.concatenate([bias_ref[kc]] * GROUP, axis=0)

        out = _masked_attention(GROUP * tq, i + 1, logits_of, functools.partial(chunk, v_ref), s_ref)
        for g in range(GROUP):
            o_ref[:, (n * GROUP + g) * HEAD_DIM:(n * GROUP + g + 1) * HEAD_DIM] = (
                out[g * tq:(g + 1) * tq].astype(o_ref.dtype))


def _dsa_prompt(qi, wi, ki, q, k, v, bsz, t, nkv, ih, topk):
    nt = t // QTILE
    qw = nkv * GROUP * HEAD_DIM
    kvw = nkv * HEAD_DIM
    rows = lambda w: pl.BlockSpec((QTILE, w), lambda b, i: (b * nt + i, 0))
    seq = lambda w: pl.BlockSpec((1, t, w), lambda b, i: (b, 0, 0))
    return pl.pallas_call(
        functools.partial(_dsa_prompt_kernel, topk=topk, nkv=nkv, ih=ih),
        grid=(bsz, nt),
        in_specs=[rows(ih * IDX_DIM), rows(128), seq(IDX_DIM), rows(qw), seq(kvw), seq(kvw)],
        out_specs=rows(qw),
        out_shape=jax.ShapeDtypeStruct((bsz * t, qw), BF16),
        scratch_shapes=[pltpu.VMEM((nt, QTILE, QTILE), I32), pltpu.VMEM((nt, QTILE, QTILE), F32),
                        pltpu.VMEM((nt, GROUP * QTILE, QTILE), F32)],
        compiler_params=_cparams("parallel", "arbitrary"),
        name="dsa_prompt_attn",
    )(qi, wi, ki.reshape(bsz, t, IDX_DIM), q, k.reshape(bsz, t, kvw), v.reshape(bsz, t, kvw))


def _topb_select_bias(gate, valid, blk_iota, nblk):
    g = jnp.where(valid, gate, -jnp.inf)
    rank = jnp.zeros(g.shape, I32)
    for m in range(nblk):
        gm = g[:, m:m + 1]
        beats = jnp.logical_or(gm > g, jnp.logical_and(gm == g, blk_iota > m))
        rank = rank + beats.astype(I32)
    sel = jnp.logical_and(valid, rank < MOBA_TOPB)
    return jnp.where(sel, 0.0, NEG)


def _topb_select_rows(gate_t, valid_t, nblk):
    g = jnp.where(valid_t, gate_t, -jnp.inf)
    blk = lax.broadcasted_iota(I32, g.shape, 0)
    rank = jnp.zeros(g.shape, I32)
    for m in range(nblk):
        gm = g[m:m + 1, :]
        beats = jnp.logical_or(gm > g, jnp.logical_and(gm == g, blk > m))
        rank = rank + beats.astype(I32)
    return jnp.logical_and(valid_t, rank < MOBA_TOPB)


def _moba_prompt_kernel(q_ref, k_ref, v_ref, km_ref, o_ref, s_ref, *, nkv, nblk):
    j = pl.program_id(1)
    tq = blk = QTILE
    rows = GROUP * tq
    row = lax.broadcasted_iota(I32, (tq, blk), 0)
    col = lax.broadcasted_iota(I32, (tq, blk), 1)
    cb = jnp.where(col <= row, 0.0, NEG)
    causal_bias = jnp.concatenate([cb] * GROUP, axis=0)
    valid_t = lax.broadcasted_iota(I32, (nblk, rows), 0) < j
    lane = lax.broadcasted_iota(I32, (blk, HEAD_DIM), 1)
    km_pad = jnp.zeros((16 - nblk % 16, HEAD_DIM), F32)
    for n in range(nkv):
        lanes = slice(n * HEAD_DIM, (n + 1) * HEAD_DIM)
        q3 = jnp.concatenate(
            [q_ref[:, (n * GROUP + g) * HEAD_DIM:(n * GROUP + g + 1) * HEAD_DIM] for g in range(GROUP)], axis=0)
        km = jnp.concatenate([km_ref[0, :, lanes], km_pad], axis=0).astype(BF16)
        gate_t = lax.dot_general(km, q3, _CONTRACT_LAST, preferred_element_type=F32)[:nblk]
        selb_t = jnp.where(_topb_select_rows(gate_t, valid_t, nblk), 0.0, NEG)
        selb = jnp.concatenate([selb_t, jnp.zeros((HEAD_DIM - nblk, rows), F32)], axis=0).T
        q_aug = jnp.concatenate([q3, selb.astype(BF16)], axis=1)
        chunk = lambda ref, m, lanes=lanes: ref[0, pl.ds(pl.multiple_of(m * blk, blk), blk), lanes]

        def logits_of(m, q_aug=q_aug, chunk=chunk):
            onehot = jnp.where(lane == m, 1.0, 0.0).astype(BF16)
            k_aug = jnp.concatenate([chunk(k_ref, m), onehot], axis=1)
            return lax.dot_general(q_aug, k_aug, _CONTRACT_LAST, preferred_element_type=F32) * SOFTMAX_C2

        own_s = lax.dot_general(q3, chunk(k_ref, j), _CONTRACT_LAST, preferred_element_type=F32) * SOFTMAX_C2
        out = _masked_attention(rows, j, logits_of, functools.partial(chunk, v_ref), s_ref,
                                own=(j, own_s + causal_bias, chunk(v_ref, j)))
        for g in range(GROUP):
            o_ref[:, (n * GROUP + g) * HEAD_DIM:(n * GROUP + g + 1) * HEAD_DIM] = (
                out[g * tq:(g + 1) * tq].astype(o_ref.dtype))


def _moba_prompt(q, k, v, kmean, bsz, t, nkv):
    nt = t // QTILE
    qw = nkv * GROUP * HEAD_DIM
    kvw = nkv * HEAD_DIM
    rows = lambda w: pl.BlockSpec((QTILE, w), lambda b, i: (b * nt + i, 0))
    seq = lambda w: pl.BlockSpec((1, t, w), lambda b, i: (b, 0, 0))
    return pl.pallas_call(
        functools.partial(_moba_prompt_kernel, nkv=nkv, nblk=nt),
        grid=(bsz, nt),
        in_specs=[rows(qw), seq(kvw), seq(kvw), pl.BlockSpec((1, nt, kvw), lambda b, i: (b, 0, 0))],
        out_specs=rows(qw),
        out_shape=jax.ShapeDtypeStruct((bsz * t, qw), BF16),
        scratch_shapes=[pltpu.VMEM((nt, GROUP * QTILE, QTILE), F32)],
        compiler_params=_cparams("parallel", "arbitrary"),
        name="moba_prompt_attn",
    )(q, k.reshape(bsz, t, kvw), v.reshape(bsz, t, kvw), kmean.reshape(bsz, nt, kvw))


def _mem_attn_kernel(q_ref, mk_ref, mv_ref, o_ref, *, hd):
    scale = hd ** -0.5
    for c in range(MEM_HEADS):
        lanes = slice(c * hd, (c + 1) * hd)
        q = q_ref[0, :, lanes].astype(BF16)
        mk = mk_ref[0, :, lanes].astype(BF16)
        mv = mv_ref[0, :, lanes].astype(BF16)
        s = lax.dot_general(q, mk, _CONTRACT_LAST, preferred_element_type=F32) * scale
        p = jnp.exp(s - jnp.max(s, axis=1, keepdims=True))
        l = jnp.sum(p, axis=1, keepdims=True)
        o = jnp.dot(p.astype(BF16), mv, preferred_element_type=F32) / l
        o_ref[0, :, lanes] = o.astype(o_ref.dtype)


def _mem_attn(qm, mk, mv, out_dtype):
    bsz, t, w = qm.shape
    mlen = mk.shape[1]
    tq = _pick_block(t, 512, 8)
    return pl.pallas_call(
        functools.partial(_mem_attn_kernel, hd=w // MEM_HEADS),
        grid=(bsz, t // tq),
        in_specs=[pl.BlockSpec((1, tq, w), lambda b, i: (b, i, 0)),
                  pl.BlockSpec((1, mlen, w), lambda b, i: (b, 0, 0)),
                  pl.BlockSpec((1, mlen, w), lambda b, i: (b, 0, 0))],
        out_specs=pl.BlockSpec((1, tq, w), lambda b, i: (b, i, 0)),
        out_shape=jax.ShapeDtypeStruct((bsz, t, w), out_dtype),
        compiler_params=_cparams("parallel", "parallel"),
        name="mem_attn",
    )(qm, mk, mv)


SAMPLE_PP = 8


def _page_specs(shape_tail, pp, layer):
    nd = len(shape_tail)
    return [pl.BlockSpec((None, None) + shape_tail,
                         lambda b, p, pt, c=c: (layer, pt[b, p * pp + c]) + (0,) * nd)
            for c in range(pp)]


def _head_sum(w, ih):
    acc = w[0:SAMPLE_TPAD]
    for h in range(1, ih):
        acc = acc + w[h * SAMPLE_TPAD:(h + 1) * SAMPLE_TPAD]
    return acc


def _dsa_sample_index_kernel(pt_ref, qi_ref, wi_ref, *refs, pp, ih, page):
    page_refs, o_ref = refs[:pp], refs[pp]
    qi = qi_ref[0].astype(BF16)
    wi = wi_ref[0]
    for c in range(pp):
        kp = page_refs[c][...].astype(BF16)
        s = lax.dot_general(qi, kp, _CONTRACT_LAST, preferred_element_type=F32)
        o_ref[0, :, c * page:(c + 1) * page] = _head_sum(jnp.maximum(s, 0.0) * wi, ih)


def _dsa_sample_index(page_table, qi_ht, wi_ht, kidx_pool, layer, ih):
    bs, npg = page_table.shape
    page = kidx_pool.shape[2]
    pp = SAMPLE_PP
    r = ih * SAMPLE_TPAD
    grid_spec = pltpu.PrefetchScalarGridSpec(
        num_scalar_prefetch=1,
        grid=(bs, npg // pp),
        in_specs=[pl.BlockSpec((1, r, IDX_DIM), lambda b, p, pt: (b, 0, 0)),
                  pl.BlockSpec((1, r, page), lambda b, p, pt: (b, 0, 0))]
        + _page_specs((page, IDX_DIM), pp, layer),
        out_specs=pl.BlockSpec((1, SAMPLE_TPAD, page * pp), lambda b, p, pt: (b, 0, p)),
    )
    return pl.pallas_call(
        functools.partial(_dsa_sample_index_kernel, pp=pp, ih=ih, page=page),
        grid_spec=grid_spec,
        out_shape=jax.ShapeDtypeStruct((bs, SAMPLE_TPAD, npg * page), F32),
        compiler_params=_cparams("parallel", "arbitrary"),
        name="dsa_sample_index",
    )(page_table, qi_ht, wi_ht, *([kidx_pool] * pp))


def _dsa_sample_select_kernel(sc_ref, qi_ref, wi_ref, kin_ref, o_ref, *, topk, ih, ts, past):
    tail_w = kin_ref.shape[1]
    s = lax.dot_general(qi_ref[0].astype(BF16), kin_ref[0].astype(BF16), _CONTRACT_LAST,
                        preferred_element_type=F32)
    tail = _head_sum(jnp.maximum(s, 0.0) * wi_ref[0][:, :tail_w], ih)
    row = lax.broadcasted_iota(I32, (SAMPLE_TPAD, tail_w), 0)
    col = lax.broadcasted_iota(I32, (SAMPLE_TPAD, tail_w), 1)
    tail_ok = jnp.logical_and(col <= row, col < ts)
    full = jnp.concatenate([sc_ref[0], jnp.where(tail_ok, tail, -jnp.inf)], axis=1)
    keys = _sortable_key(full)
    width = past + tail_w
    col_f = lax.broadcasted_iota(I32, (SAMPLE_TPAD, width), 1)
    row_f = lax.broadcasted_iota(I32, (SAMPLE_TPAD, width), 0)
    visible = jnp.logical_or(col_f < past, jnp.logical_and(col_f - past <= row_f, col_f - past < ts))

    def count_ge(cand):
        return jnp.sum((keys >= cand).astype(F32), axis=1, keepdims=True)

    thr = _kth_largest_key(count_ge, (SAMPLE_TPAD, 1), topk)
    o_ref[0] = jnp.where(jnp.logical_and(keys >= thr, visible), 0.0, NEG)


def _dsa_sample_select(scores, qi_ht, wi_ht, ki_new, ih, ts, topk):
    bs, _, past = scores.shape
    tail_w = ki_new.shape[1]
    r = ih * SAMPLE_TPAD
    return pl.pallas_call(
        functools.partial(_dsa_sample_select_kernel, topk=topk, ih=ih, ts=ts, past=past),
        grid=(bs,),
        in_specs=[pl.BlockSpec((1, SAMPLE_TPAD, past), lambda b: (b, 0, 0)),
                  pl.BlockSpec((1, r, IDX_DIM), lambda b: (b, 0, 0)),
                  pl.BlockSpec((1, r, wi_ht.shape[2]), lambda b: (b, 0, 0)),
                  pl.BlockSpec((1, tail_w, IDX_DIM), lambda b: (b, 0, 0))],
        out_specs=pl.BlockSpec((1, SAMPLE_TPAD, past + tail_w), lambda b: (b, 0, 0)),
        out_shape=jax.ShapeDtypeStruct((bs, SAMPLE_TPAD, past + tail_w), F32),
        compiler_params=_cparams("parallel"),
        name="dsa_sample_select",
    )(scores, qi_ht, wi_ht, ki_new)


def _sample_flash_update(qbd, kblk, vblk, bias, m_ref, l_ref, acc_ref, nkv):
    rows_per_head = GROUP * SAMPLE_TPAD
    s = lax.dot_general(qbd, kblk, _CONTRACT_LAST, preferred_element_type=F32) * (HEAD_DIM ** -0.5) + bias
    m_old = m_ref[...]
    m_new = jnp.maximum(m_old, jnp.max(s, axis=1, keepdims=True))
    alpha = jnp.exp(m_old - m_new)
    p = jnp.exp(s - m_new)
    l_ref[...] = alpha * l_ref[...] + jnp.sum(p, axis=1, keepdims=True)
    m_ref[...] = m_new
    o_full = jnp.dot(p.astype(BF16), vblk, preferred_element_type=F32)
    o_diag = jnp.concatenate(
        [o_full[n * rows_per_head:(n + 1) * rows_per_head, n * HEAD_DIM:(n + 1) * HEAD_DIM] for n in range(nkv)],
        axis=0)
    acc_ref[...] = alpha * acc_ref[...] + o_diag


def _sample_flash_reset(m_ref, l_ref, acc_ref):
    m_ref[...] = jnp.full(m_ref.shape, NEG, F32)
    l_ref[...] = jnp.zeros(l_ref.shape, F32)
    acc_ref[...] = jnp.zeros(acc_ref.shape, F32)


def _heads_on_lanes(refs, nkv):
    keys = refs[0].shape[0] // nkv
    pages = [jnp.concatenate([r[pl.ds(n, keys, stride=nkv), :] for n in range(nkv)], axis=1).astype(BF16)
             for r in refs]
    return jnp.concatenate(pages, axis=0)


def _flat_pool(pool):
    return pool.reshape(pool.shape[0], pool.shape[1], pool.shape[2] * pool.shape[3], pool.shape[4])


def _tile_query_bias(b8, reps):
    return jnp.concatenate([b8] * reps, axis=0)


def _dsa_sample_attn_kernel(pt_ref, q_ref, bias_ref, tbias_ref, kn_ref, vn_ref, *refs, pp, nkv):
    k_refs, v_refs = refs[:pp], refs[pp:2 * pp]
    o_ref, m_ref, l_ref, acc_ref = refs[2 * pp:]
    p = pl.program_id(1)
    reps = nkv * GROUP

    @pl.when(p == 0)
    def _():
        _sample_flash_reset(m_ref, l_ref, acc_ref)

    _sample_flash_update(q_ref[0], _heads_on_lanes(k_refs, nkv), _heads_on_lanes(v_refs, nkv),
                         _tile_query_bias(bias_ref[0], reps), m_ref, l_ref, acc_ref, nkv)

    @pl.when(p == pl.num_programs(1) - 1)
    def _():
        _sample_flash_update(q_ref[0], kn_ref[0].astype(BF16), vn_ref[0].astype(BF16),
                             _tile_query_bias(tbias_ref[0], reps), m_ref, l_ref, acc_ref, nkv)
        o_ref[0] = acc_ref[...] / l_ref[...]


def _sample_attn_specs(nkv, tail_rows):
    rows = nkv * GROUP * SAMPLE_TPAD
    kvw = nkv * HEAD_DIM
    q_spec = pl.BlockSpec((1, rows, kvw), lambda b, p, pt: (b, 0, 0))
    tail_spec = pl.BlockSpec((1, tail_rows, kvw), lambda b, p, pt: (b, 0, 0))
    out_spec = pl.BlockSpec((1, rows, HEAD_DIM), lambda b, p, pt: (b, 0, 0))
    scratch = [pltpu.VMEM((rows, 1), F32), pltpu.VMEM((rows, 1), F32), pltpu.VMEM((rows, HEAD_DIM), F32)]
    return rows, q_spec, tail_spec, out_spec, scratch


def _dsa_sample_attn(page_table, qbd, bias, k_new, v_new, k_pool, v_pool, layer, nkv):
    bs, npg = page_table.shape
    page = k_pool.shape[2]
    pp = SAMPLE_PP
    rows, q_spec, tail_spec, out_spec, scratch = _sample_attn_specs(nkv, k_new.shape[1])
    grid_spec = pltpu.PrefetchScalarGridSpec(
        num_scalar_prefetch=1,
        grid=(bs, npg // pp),
        in_specs=[q_spec,
                  pl.BlockSpec((1, SAMPLE_TPAD, page * pp), lambda b, p, pt: (b, 0, p)),
                  pl.BlockSpec((1, SAMPLE_TPAD, page), lambda b, p, pt: (b, 0, npg)),
                  tail_spec, tail_spec]
        + _page_specs((page * nkv, HEAD_DIM), pp, layer) + _page_specs((page * nkv, HEAD_DIM), pp, layer),
        out_specs=out_spec,
        scratch_shapes=scratch,
    )
    return pl.pallas_call(
        functools.partial(_dsa_sample_attn_kernel, pp=pp, nkv=nkv),
        grid_spec=grid_spec,
        out_shape=jax.ShapeDtypeStruct((bs, rows, HEAD_DIM), F32),
        compiler_params=_cparams("parallel", "arbitrary"),
        name="dsa_sample_attn",
    )(page_table, qbd, bias, bias, k_new, v_new, *([_flat_pool(k_pool)] * pp), *([_flat_pool(v_pool)] * pp))


KMEAN_BLOCKS = 8


def _moba_kmean_kernel(pt_ref, *refs, nblk, ppb):
    k_refs, o_ref = refs[:nblk * ppb], refs[nblk * ppb]
    for blk in range(nblk):
        acc = jnp.sum(k_refs[blk * ppb][...], axis=0)
        for c in range(1, ppb):
            acc = acc + jnp.sum(k_refs[blk * ppb + c][...], axis=0)
        o_ref[0, blk] = acc * (1.0 / MOBA_BLOCK)


def _moba_sample_kmean(page_table, k_pool, layer, nkv):
    bs, npg = page_table.shape
    page = k_pool.shape[2]
    ppb = MOBA_BLOCK // page
    nb = npg // ppb
    nblk = KMEAN_BLOCKS
    grid_spec = pltpu.PrefetchScalarGridSpec(
        num_scalar_prefetch=1,
        grid=(bs, nb // nblk),
        in_specs=_page_specs((page, nkv, HEAD_DIM), nblk * ppb, layer),
        out_specs=pl.BlockSpec((1, nblk, nkv, HEAD_DIM), lambda b, p, pt: (b, p, 0, 0)),
    )
    return pl.pallas_call(
        functools.partial(_moba_kmean_kernel, nblk=nblk, ppb=ppb),
        grid_spec=grid_spec,
        out_shape=jax.ShapeDtypeStruct((bs, nb, nkv, HEAD_DIM), F32),
        compiler_params=_cparams("parallel", "arbitrary"),
        name="moba_sample_kmean",
    )(page_table, *([k_pool] * (nblk * ppb)))


def _moba_sample_attn_kernel(pt_ref, q_ref, km_ref, kn_ref, vn_ref, *refs, pp, nkv, page, nb, ts):
    k_refs, v_refs = refs[:pp], refs[pp:2 * pp]
    o_ref, m_ref, l_ref, acc_ref, sb_ref = refs[2 * pp:]
    p = pl.program_id(1)
    rows = nkv * GROUP * SAMPLE_TPAD
    ppb = MOBA_BLOCK // page
    blk_iota = lax.broadcasted_iota(I32, (rows, nb), 1)

    @pl.when(p == 0)
    def _():
        _sample_flash_reset(m_ref, l_ref, acc_ref)
        valid = blk_iota >= 0
        gate = lax.dot_general(q_ref[0], km_ref[0].astype(BF16), _CONTRACT_LAST, preferred_element_type=F32)
        sb_ref[...] = _topb_select_bias(gate, valid, blk_iota, nb)

    selb = sb_ref[...]
    cols = []
    for c in range(pp // ppb):
        blk = p * (pp // ppb) + c
        col = jnp.sum(jnp.where(blk_iota == blk, selb, 0.0), axis=1, keepdims=True)
        cols.append(jnp.broadcast_to(col, (rows, MOBA_BLOCK)))
    _sample_flash_update(q_ref[0], _heads_on_lanes(k_refs, nkv), _heads_on_lanes(v_refs, nkv),
                         jnp.concatenate(cols, axis=1), m_ref, l_ref, acc_ref, nkv)

    @pl.when(p == pl.num_programs(1) - 1)
    def _():
        tail_w = kn_ref.shape[1]
        row = lax.broadcasted_iota(I32, (rows, tail_w), 0) & (SAMPLE_TPAD - 1)
        col = lax.broadcasted_iota(I32, (rows, tail_w), 1)
        tb = jnp.where(jnp.logical_and(col <= row, col < ts), 0.0, NEG)
        _sample_flash_update(q_ref[0], kn_ref[0].astype(BF16), vn_ref[0].astype(BF16), tb,
                             m_ref, l_ref, acc_ref, nkv)
        o_ref[0] = acc_ref[...] / l_ref[...]


def _moba_sample_attn(page_table, qbd, kmean, k_new, v_new, k_pool, v_pool, layer, nkv, ts):
    bs, npg = page_table.shape
    page = k_pool.shape[2]
    kvw = nkv * HEAD_DIM
    pp = SAMPLE_PP
    nb = kmean.shape[1]
    rows, q_spec, tail_spec, out_spec, scratch = _sample_attn_specs(nkv, k_new.shape[1])
    grid_spec = pltpu.PrefetchScalarGridSpec(
        num_scalar_prefetch=1,
        grid=(bs, npg // pp),
        in_specs=[q_spec, pl.BlockSpec((1, nb, kvw), lambda b, p, pt: (b, 0, 0)), tail_spec, tail_spec]
        + _page_specs((page * nkv, HEAD_DIM), pp, layer) + _page_specs((page * nkv, HEAD_DIM), pp, layer),
        out_specs=out_spec,
        scratch_shapes=scratch + [pltpu.VMEM((rows, nb), F32)],
    )
    return pl.pallas_call(
        functools.partial(_moba_sample_attn_kernel, pp=pp, nkv=nkv, page=page, nb=nb, ts=ts),
        grid_spec=grid_spec,
        out_shape=jax.ShapeDtypeStruct((bs, rows, HEAD_DIM), F32),
        compiler_params=_cparams("parallel", "arbitrary"),
        name="moba_sample_attn",
    )(page_table, qbd, kmean, k_new, v_new, *([_flat_pool(k_pool)] * pp), *([_flat_pool(v_pool)] * pp))


def _heads_to_kv_major(q, bs, nkv):
    x = q.reshape(bs, SAMPLE_TPAD, nkv, GROUP, HEAD_DIM).transpose(0, 2, 3, 1, 4)
    eye = jnp.eye(nkv, dtype=q.dtype)
    bd = x[:, :, :, :, None, :] * eye[None, :, None, None, :, None]
    return bd.reshape(bs, nkv * GROUP * SAMPLE_TPAD, nkv * HEAD_DIM).astype(BF16)


def _kv_major_to_rows(o, bs, nkv):
    x = o.reshape(bs, nkv, GROUP, SAMPLE_TPAD, HEAD_DIM)
    return x.transpose(0, 3, 1, 2, 4).reshape(bs * SAMPLE_TPAD, nkv * GROUP * HEAD_DIM)


def _pad_rows(x, n):
    return jnp.pad(x, ((0, 0), (0, n - x.shape[1]), (0, 0)))


def _finish_layer(h, mix, qm, mk, mv, bsz, layer, w_o, g_ffn, w_up, w_down):
    m = h.shape[0]
    ma = _mem_attn(qm.reshape(bsz, m // bsz, -1), mk, mv, qm.dtype).reshape(m, -1)
    merged = jnp.concatenate([mix.astype(BF16), ma.astype(BF16)], axis=1)
    h = _matmul(merged, w_o, layer, F32, epilogue="residual", residual=h)
    u = _matmul(_rmsnorm(h, g_ffn, BF16), w_up, layer, BF16, epilogue="relu2")
    return _matmul(u, w_down, layer, F32, epilogue="residual", residual=h)


def kernel(x_prompt, x_sample, cache_dsa_k, cache_dsa_v, cache_dsa_kidx, cache_moba_k, cache_moba_v, cache_mem_k, cache_mem_v, page_table, mem_prompt, norm_mix, norm_mem, w_in_dsa, w_in_moba, w_mem_kv, w_out, norm_ffn, w_up, w_down, norm_final):
    bsz, t, d = x_prompt.shape
    bs, ts, _ = x_sample.shape
    depth = norm_mix.shape[0]
    nh = (3 * d) // (4 * HEAD_DIM)
    nkv = nh // GROUP
    qw, kvw = nh * HEAD_DIM, nkv * HEAD_DIM
    memw = d // 4
    mhd = memw // MEM_HEADS
    ih = d // 128
    npg = page_table.shape[1]
    page = cache_dsa_k.shape[2]
    past = npg * page
    mlen = mem_prompt.shape[1]
    tp = SAMPLE_TPAD

    hp = x_prompt.reshape(bsz * t, d)
    hs = jnp.pad(x_sample, ((0, 0), (0, tp - ts), (0, 0))).reshape(bs * tp, d)
    pos_p = jnp.arange(t, dtype=I32)
    pos_s = past + (jnp.arange(tp, dtype=I32) % ts)
    tabs_p = _rope_tables(pos_p)
    tabs_s = _rope_tables(jnp.tile(pos_s, bs))

    outs = {name: [] for name in ("pdk", "pdv", "pdki", "sdk", "sdv", "sdki", "pmk", "pmv", "smk", "smv", "mk", "mv")}
    mem_flat = mem_prompt.reshape(bsz * mlen, d)

    w_o, wu, wd, w_mem = w_out.astype(BF16), w_up.astype(BF16), w_down.astype(BF16), w_mem_kv.astype(BF16)
    w_dsa, w_moba = w_in_dsa.astype(BF16), w_in_moba.astype(BF16)
    c0 = qw + 2 * kvw + ih * IDX_DIM
    w_dsa_tail = jnp.concatenate(
        [w_in_dsa[:, :, c0 + ih:c0 + ih + IDX_DIM], w_in_dsa[:, :, c0 + ih + IDX_DIM:], w_in_dsa[:, :, c0:c0 + ih],
         jnp.zeros(w_in_dsa.shape[:2] + (128 - ih,), F32)], axis=2).astype(BF16)

    for i in range(depth):
        j = i // 2
        mkv = _matmul(_rmsnorm(mem_flat, norm_mem[i], BF16), w_mem, i, F32)
        mk_p = mkv[:, :memw].reshape(bsz, mlen, memw)
        mv_p = mkv[:, memw:].reshape(bsz, mlen, memw)
        outs["mk"].append(mk_p.reshape(bsz, mlen, MEM_HEADS, mhd))
        outs["mv"].append(mv_p.reshape(bsz, mlen, MEM_HEADS, mhd))
        hn_p = _rmsnorm(hp, norm_mix[i], BF16)
        hn_s = _rmsnorm(hs, norm_mix[i], BF16)
        mk_s = cache_mem_k[i].reshape(bs, mlen, memw)
        mv_s = cache_mem_v[i].reshape(bs, mlen, memw)
        if i % 2 == 0:
            pa = _matmul(hn_p, w_dsa, j, F32, n=c0)
            pb = _matmul(hn_p, w_dsa_tail, j, F32)
            q, kf, kb, vf, vb, qi, kif, kib, wi, qm = _dsa_split(pa, pb, tabs_p, t, nh, nkv, ih, memw, BF16)
            outs["pdk"].append(kf.reshape(bsz, t, nkv, HEAD_DIM))
            outs["pdv"].append(vf.reshape(bsz, t, nkv, HEAD_DIM))
            outs["pdki"].append(kif.reshape(bsz, t, IDX_DIM))
            mix_p = _dsa_prompt(qi, wi, kib, q, kb, vb, bsz, t, nkv, ih, min(DSA_TOPK, t // 4))
            pa = _matmul(hn_s, w_dsa, j, F32, n=c0)
            pb = _matmul(hn_s, w_dsa_tail, j, F32)
            q_s, kf, _, vf, _, qi_s, kif, _, wi_s, qm_s = _dsa_split(pa, pb, tabs_s, bs * tp, nh, nkv, ih, memw, F32)
            outs["sdk"].append(kf.reshape(bs, tp, nkv, HEAD_DIM)[:, :ts])
            outs["sdv"].append(vf.reshape(bs, tp, nkv, HEAD_DIM)[:, :ts])
            outs["sdki"].append(kif.reshape(bs, tp, IDX_DIM)[:, :ts])
            qi_ht = qi_s.reshape(bs, tp, ih, IDX_DIM).transpose(0, 2, 1, 3).reshape(bs, ih * tp, IDX_DIM)
            wi_ht = wi_s.reshape(bs, tp, 128)[:, :, :ih].transpose(0, 2, 1).reshape(bs, ih * tp, 1)
            wi_ht = jnp.broadcast_to(wi_ht, (bs, ih * tp, page))
            scores = _dsa_sample_index(page_table, qi_ht, wi_ht, cache_dsa_kidx, j, ih)
            ki_new = _pad_rows(kif.reshape(bs, tp, IDX_DIM), page)
            bias = _dsa_sample_select(scores, qi_ht, wi_ht, ki_new, ih, ts, min(DSA_TOPK, (past + ts) // 4))
            o_s = _dsa_sample_attn(page_table, _heads_to_kv_major(q_s, bs, nkv), bias,
                                   _pad_rows(kf.reshape(bs, tp, kvw), page), _pad_rows(vf.reshape(bs, tp, kvw), page),
                                   cache_dsa_k, cache_dsa_v, j, nkv)
            mix_s = _kv_major_to_rows(o_s, bs, nkv)
        else:
            p = _matmul(hn_p, w_moba, j, F32)
            q, kf, kb, vf, vb, qm, kmean = _moba_split(p, tabs_p, t, nh, nkv, memw, BF16)
            outs["pmk"].append(kf.reshape(bsz, t, nkv, HEAD_DIM))
            outs["pmv"].append(vf.reshape(bsz, t, nkv, HEAD_DIM))
            mix_p = _moba_prompt(q, kb, vb, kmean, bsz, t, nkv)
            p = _matmul(hn_s, w_moba, j, F32)
            q_s, kf, _, vf, _, qm_s, _ = _moba_split(p, tabs_s, bs * tp, nh, nkv, memw, F32)
            outs["smk"].append(kf.reshape(bs, tp, nkv, HEAD_DIM)[:, :ts])
            outs["smv"].append(vf.reshape(bs, tp, nkv, HEAD_DIM)[:, :ts])
            kmean_s = _moba_sample_kmean(page_table, cache_moba_k, j, nkv)
            kmean_s = kmean_s.reshape(bs, kmean_s.shape[1], kvw)
            o_s = _moba_sample_attn(page_table, _heads_to_kv_major(q_s, bs, nkv), kmean_s,
                                    _pad_rows(kf.reshape(bs, tp, kvw), page), _pad_rows(vf.reshape(bs, tp, kvw), page),
                                    cache_moba_k, cache_moba_v, j, nkv, ts)
            mix_s = _kv_major_to_rows(o_s, bs, nkv)
        hp = _finish_layer(hp, mix_p, qm, mk_p, mv_p, bsz, i, w_o, norm_ffn[i], wu, wd)
        hs = _finish_layer(hs, mix_s, qm_s, mk_s, mv_s, bs, i, w_o, norm_ffn[i], wu, wd)

    y_prompt = _rmsnorm(hp, norm_final, F32).reshape(bsz, t, d)
    y_sample = _rmsnorm(hs, norm_final, F32).reshape(bs, tp, d)[:, :ts]
    st = jnp.stack
    return (y_prompt, y_sample, st(outs["pdk"]), st(outs["pdv"]), st(outs["pdki"]), st(outs["pmk"]), st(outs["pmv"]),
            st(outs["mk"]), st(outs["mv"]), st(outs["sdk"]), st(outs["sdv"]), st(outs["sdki"]),
            st(outs["smk"]), st(outs["smv"]))
```

```python
import functools

import jax
import jax.numpy as jnp
from jax import lax
from jax.experimental import pallas as pl
from jax.experimental.pallas import tpu as pltpu

F32 = jnp.float32
BF16 = jnp.bfloat16
I32 = jnp.int32

HEAD_DIM = 128
GROUP = 3
ROT_DIM = 32
ROPE_THETA = 500000.0
IDX_DIM = 128
MEM_HEADS = 4
DSA_TOPK = 256
MOBA_BLOCK = 256
MOBA_TOPB = 3
EPS = 1e-6
NEG = -1e30
LOG2E = 1.4426950408889634
SOFTMAX_C2 = (HEAD_DIM ** -0.5) * LOG2E
QTILE = 256
SAMPLE_TPAD = 8
VMEM_LIMIT = 60 * 1024 * 1024

_CONTRACT_LAST = (((1,), (1,)), ((), ()))


def _cparams(*sem):
    return pltpu.CompilerParams(dimension_semantics=sem, vmem_limit_bytes=VMEM_LIMIT)


def _pick_block(n, pref, align=128):
    if n <= pref:
        return n
    b = (pref // align) * align
    while b >= align:
        if n % b == 0:
            return b
        b -= align
    return n


def _rmsnorm_kernel(x_ref, g_ref, o_ref):
    x = x_ref[...]
    ms = jnp.mean(x * x, axis=-1, keepdims=True)
    o_ref[...] = ((x * lax.rsqrt(ms + EPS)) * g_ref[...]).astype(o_ref.dtype)


def _rmsnorm(x, g, out_dtype):
    m, d = x.shape
    bm = _pick_block(m, 256, 8)
    return pl.pallas_call(
        _rmsnorm_kernel,
        grid=(m // bm,),
        in_specs=[pl.BlockSpec((bm, d), lambda i: (i, 0)), pl.BlockSpec((1, d), lambda i: (0, 0))],
        out_specs=pl.BlockSpec((bm, d), lambda i: (i, 0)),
        out_shape=jax.ShapeDtypeStruct((m, d), out_dtype),
        compiler_params=_cparams("parallel"),
        name="rmsnorm",
    )(x, g.reshape(1, d).astype(F32))


def _mm_tile(a_ref, w, r_ref, o_ref, k, nk, epilogue):
    part = jnp.dot(a_ref[...], w, preferred_element_type=F32)

    def finish(acc):
        if epilogue == "relu2":
            r = jnp.maximum(acc, 0.0)
            acc = r * r
        elif epilogue == "residual":
            acc = r_ref[...] + acc
        o_ref[...] = acc.astype(o_ref.dtype)

    if nk == 1:
        finish(part)
        return

    @pl.when(k == 0)
    def _():
        o_ref[...] = part

    @pl.when(jnp.logical_and(k > 0, k < nk - 1))
    def _():
        o_ref[...] += part

    @pl.when(k == nk - 1)
    def _():
        finish(o_ref[...] + part)


def _mm_kernel(*refs, nk, epilogue, dual):
    refs = list(refs)
    a_ref, w_ref = refs.pop(0), refs.pop(0)
    r_ref = refs.pop(0) if epilogue == "residual" else None
    a2_ref = refs.pop(0) if dual else None
    r2_ref = refs.pop(0) if dual and epilogue == "residual" else None
    o_ref = refs.pop(0)
    k = pl.program_id(2)
    w = w_ref[...]
    _mm_tile(a_ref, w, r_ref, o_ref, k, nk, epilogue)
    if dual:
        @pl.when(pl.program_id(1) == 0)
        def _():
            _mm_tile(a2_ref, w, r2_ref, refs[0], k, nk, epilogue)


def _matmul(a, w, layer, out_dtype, epilogue=None, residual=None, n=None, a2=None, residual2=None,
            bm_pref=1024, bn_pref=1024, bk_pref=4096):
    m, kdim = a.shape
    n = w.shape[2] if n is None else n
    if epilogue == "residual":
        bm_pref //= 2
    bm = _pick_block(m, bm_pref, 8)
    bn = _pick_block(n, bn_pref)
    bk = _pick_block(kdim, bk_pref)
    nk = kdim // bk
    dual = a2 is not None
    assert nk == 1 or out_dtype == F32, "a split contraction accumulates in the f32 output block"
    in_specs = [pl.BlockSpec((bm, bk), lambda j, i, k: (i, k)),
                pl.BlockSpec((None, bk, bn), lambda j, i, k: (layer, k, j))]
    args = [a, w]
    out_specs = [pl.BlockSpec((bm, bn), lambda j, i, k: (i, j))]
    out_shape = [jax.ShapeDtypeStruct((m, n), out_dtype)]
    if epilogue == "residual":
        in_specs.append(pl.BlockSpec((bm, bn), lambda j, i, k: (i, j)))
        args.append(residual)
    if dual:
        m2 = a2.shape[0]
        in_specs.append(pl.BlockSpec((m2, bk), lambda j, i, k: (0, k)))
        args.append(a2)
        if epilogue == "residual":
            in_specs.append(pl.BlockSpec((m2, bn), lambda j, i, k: (0, j)))
            args.append(residual2)
        out_specs.append(pl.BlockSpec((m2, bn), lambda j, i, k: (0, j)))
        out_shape.append(jax.ShapeDtypeStruct((m2, n), out_dtype))
    out = pl.pallas_call(
        functools.partial(_mm_kernel, nk=nk, epilogue=epilogue, dual=dual),
        grid=(n // bn, m // bm, nk),
        in_specs=in_specs,
        out_specs=out_specs,
        out_shape=out_shape,
        compiler_params=_cparams("parallel", "arbitrary" if dual else "parallel", "arbitrary"),
        name="matmul",
    )(*args)
    return tuple(out) if dual else out[0]


def _rope_tables(pos):
    half = ROT_DIM // 2
    inv = ROPE_THETA ** (-jnp.arange(half, dtype=F32) / half)
    ang = pos.astype(F32)[:, None] * inv[None, :]
    cos, sin = jnp.cos(ang), jnp.sin(ang)
    r = pos.shape[0]
    z16 = jnp.zeros((r, half), F32)
    zrest = jnp.zeros((r, HEAD_DIM - ROT_DIM), F32)
    c = jnp.concatenate([cos, cos, jnp.ones((r, HEAD_DIM - ROT_DIM), F32)], axis=1)
    sm = jnp.concatenate([-sin, z16, zrest], axis=1)
    sp = jnp.concatenate([z16, sin, zrest], axis=1)
    return c, sm, sp


def _rope_head(x, c, sm, sp):
    half = ROT_DIM // 2
    return x * c + pltpu.roll(x, HEAD_DIM - half, 1) * sm + pltpu.roll(x, half, 1) * sp


def _dsa_split_kernel(pa_ref, pb_ref, c_ref, sm_ref, sp_ref,
                      q_ref, kf_ref, kb_ref, vf_ref, vb_ref, qi_ref, kif_ref, kib_ref, wi_ref, qm_ref,
                      *, nh, nkv, ih, memw):
    c, sm, sp = c_ref[...], sm_ref[...], sp_ref[...]
    qw, kvw = nh * HEAD_DIM, nkv * HEAD_DIM
    for h in range(nh):
        sl = slice(h * HEAD_DIM, (h + 1) * HEAD_DIM)
        q_ref[:, sl] = _rope_head(pa_ref[:, sl], c, sm, sp).astype(q_ref.dtype)
    for h in range(nkv):
        sl = slice(h * HEAD_DIM, (h + 1) * HEAD_DIM)
        kr = _rope_head(pa_ref[:, qw + h * HEAD_DIM: qw + (h + 1) * HEAD_DIM], c, sm, sp)
        kf_ref[:, sl] = kr
        kb_ref[:, sl] = kr.astype(kb_ref.dtype)
    v = pa_ref[:, qw + kvw: qw + 2 * kvw]
    vf_ref[...] = v
    vb_ref[...] = v.astype(vb_ref.dtype)
    base = qw + 2 * kvw
    for h in range(ih):
        sl = slice(h * IDX_DIM, (h + 1) * IDX_DIM)
        qi_ref[:, sl] = _rope_head(pa_ref[:, base + h * IDX_DIM: base + (h + 1) * IDX_DIM], c, sm, sp).astype(qi_ref.dtype)
    ki = _rope_head(pb_ref[:, 0:IDX_DIM], c, sm, sp)
    kif_ref[...] = ki
    kib_ref[...] = ki.astype(kib_ref.dtype)
    qm_ref[...] = pb_ref[:, IDX_DIM:IDX_DIM + memw].astype(qm_ref.dtype)
    wi_ref[...] = pb_ref[:, IDX_DIM + memw:IDX_DIM + memw + 128] * ((ih * IDX_DIM) ** -0.5)


def _dsa_split(pa, pb, tabs, rows_per_seq, nh, nkv, ih, memw, act_dtype):
    m = pa.shape[0]
    bm = _pick_block(rows_per_seq, 128, 8)
    nt = rows_per_seq // bm
    qw, kvw = nh * HEAD_DIM, nkv * HEAD_DIM
    row = lambda w: pl.BlockSpec((bm, w), lambda i: (i, 0))
    tab = pl.BlockSpec((bm, HEAD_DIM), lambda i: (i % nt, 0))
    shapes = [(qw, act_dtype), (kvw, F32), (kvw, act_dtype), (kvw, F32), (kvw, act_dtype),
              (ih * IDX_DIM, act_dtype), (IDX_DIM, F32), (IDX_DIM, act_dtype), (128, F32), (memw, act_dtype)]
    return pl.pallas_call(
        functools.partial(_dsa_split_kernel, nh=nh, nkv=nkv, ih=ih, memw=memw),
        grid=(m // bm,),
        in_specs=[row(pa.shape[1]), row(pb.shape[1]), tab, tab, tab],
        out_specs=[row(w) for w, _ in shapes],
        out_shape=[jax.ShapeDtypeStruct((m, w), dt) for w, dt in shapes],
        compiler_params=_cparams("parallel"),
        name="dsa_split",
    )(pa, pb, *tabs)


def _moba_split_kernel(p_ref, c_ref, sm_ref, sp_ref, q_ref, kf_ref, kb_ref, vf_ref, vb_ref, qm_ref, km_ref,
                       *, nh, nkv, memw):
    c, sm, sp = c_ref[...], sm_ref[...], sp_ref[...]
    qw, kvw = nh * HEAD_DIM, nkv * HEAD_DIM
    for h in range(nh):
        sl = slice(h * HEAD_DIM, (h + 1) * HEAD_DIM)
        q_ref[:, sl] = _rope_head(p_ref[:, sl], c, sm, sp).astype(q_ref.dtype)
    rows = p_ref.shape[0]
    for h in range(nkv):
        sl = slice(h * HEAD_DIM, (h + 1) * HEAD_DIM)
        kr = _rope_head(p_ref[:, qw + h * HEAD_DIM: qw + (h + 1) * HEAD_DIM], c, sm, sp)
        kf_ref[:, sl] = kr
        kb_ref[:, sl] = kr.astype(kb_ref.dtype)
        km_ref[0, :, sl] = jnp.sum(kr, axis=0, keepdims=True) * (1.0 / rows)
    v = p_ref[:, qw + kvw: qw + 2 * kvw]
    vf_ref[...] = v
    vb_ref[...] = v.astype(vb_ref.dtype)
    qm_ref[...] = p_ref[:, qw + 2 * kvw: qw + 2 * kvw + memw].astype(qm_ref.dtype)


def _moba_split(p, tabs, rows_per_seq, nh, nkv, memw, act_dtype):
    m = p.shape[0]
    bm = _pick_block(rows_per_seq, MOBA_BLOCK, 8)
    nt = rows_per_seq // bm
    qw, kvw = nh * HEAD_DIM, nkv * HEAD_DIM
    row = lambda w: pl.BlockSpec((bm, w), lambda i: (i, 0))
    tab = pl.BlockSpec((bm, HEAD_DIM), lambda i: (i % nt, 0))
    shapes = [(qw, act_dtype), (kvw, F32), (kvw, act_dtype), (kvw, F32), (kvw, act_dtype), (memw, act_dtype)]
    return pl.pallas_call(
        functools.partial(_moba_split_kernel, nh=nh, nkv=nkv, memw=memw),
        grid=(m // bm,),
        in_specs=[row(p.shape[1]), tab, tab, tab],
        out_specs=[row(w) for w, _ in shapes] + [pl.BlockSpec((1, 1, kvw), lambda i: (i, 0, 0))],
        out_shape=[jax.ShapeDtypeStruct((m, w), dt) for w, dt in shapes]
        + [jax.ShapeDtypeStruct((m // bm, 1, kvw), F32)],
        compiler_params=_cparams("parallel"),
        name="moba_split",
    )(p, *tabs)


def _sortable_key(x):
    bits = pltpu.bitcast(x, I32)
    return jnp.where(bits < 0, bits ^ jnp.int32(0x7FFFFFFF), bits)


def _kth_largest_key(count_ge, shape, k):
    imin = jnp.int32(-2 ** 31)
    c0 = count_ge(jnp.zeros(shape, I32))
    thr = jnp.where(c0 >= k, jnp.int32(0), imin)

    def bit_body(it, thr):
        cand = thr + jnp.left_shift(jnp.int32(1), jnp.int32(30) - it)
        return jnp.where(count_ge(cand) >= k, cand, thr)

    return lax.fori_loop(0, 31, bit_body, thr)


def _lane_fold(x, acc, op):
    for w in range(x.shape[1] // 128):
        acc = op(acc, x[:, w * 128:(w + 1) * 128])
    return acc


def _paired_loop(n, body, carry):
    def pair(c2, carry):
        return body(2 * c2 + 1, body(2 * c2, carry))
    npair = n // 2
    carry = lax.fori_loop(0, npair, pair, carry)
    return lax.fori_loop(2 * npair, n, body, carry)


def _masked_attention(rows, n_dyn, logits_of, load_v, s_ref, own=None):
    mx = jnp.full((rows, 128), NEG, F32)
    if own is not None:
        slot, own_s, own_v = own
        s_ref[slot] = own_s
        mx = _lane_fold(own_s, mx, jnp.maximum)

    def pass_a(c, mx):
        s = logits_of(c)
        s_ref[c] = s
        return _lane_fold(s, mx, jnp.maximum)

    mx = _paired_loop(n_dyn, pass_a, mx)
    m = jnp.max(mx, axis=1, keepdims=True)

    def accumulate(c, vb, carry):
        lsum, acc = carry
        p = jnp.exp2(s_ref[c] - m)
        return _lane_fold(p, lsum, jnp.add), acc + jnp.dot(p.astype(BF16), vb, preferred_element_type=F32)

    carry = (jnp.zeros((rows, 128), F32), jnp.zeros((rows, HEAD_DIM), F32))
    if own is not None:
        carry = accumulate(slot, own_v, carry)
    lsum, acc = _paired_loop(n_dyn, lambda c, carry: accumulate(c, load_v(c), carry), carry)
    return acc / jnp.sum(lsum, axis=1, keepdims=True)


def _dsa_prompt_kernel(qi_ref, wi_ref, ki_ref, q_ref, k_ref, v_ref, o_ref, key_ref, bias_ref, s_ref,
                       *, topk, nkv, ih):
    i = pl.program_id(1)
    tq = ck = QTILE
    key_i = lax.broadcasted_iota(I32, (ck, tq), 0)
    qry_i = lax.broadcasted_iota(I32, (ck, tq), 1)
    wi_t = wi_ref[...].T

    def causal_ok(kc):
        return key_i <= qry_i + jnp.where(kc < i, ck, 0)

    def idx_body(kc, _):
        off = pl.multiple_of(kc * ck, ck)
        kic = ki_ref[0, pl.ds(off, ck), :]
        acc = jnp.zeros((ck, tq), F32)
        for h in range(ih):
            s = lax.dot_general(kic, qi_ref[:, h * IDX_DIM:(h + 1) * IDX_DIM], _CONTRACT_LAST,
                                preferred_element_type=F32)
            acc = acc + jnp.maximum(s, 0.0) * wi_t[h:h + 1, :]
        key_ref[kc] = _sortable_key(jnp.where(causal_ok(kc), acc, -jnp.inf))
        return 0

    lax.fori_loop(0, i + 1, idx_body, 0)

    def count_ge(cand):
        def body(kc, acc):
            ge = jnp.where(key_ref[kc] >= cand, 1.0, 0.0)
            return acc + jnp.sum(ge.reshape(ck // 8, 8, tq), axis=0)
        acc = lax.fori_loop(0, i + 1, body, jnp.zeros((8, tq), F32))
        return jnp.sum(acc, axis=0, keepdims=True)

    thr = _kth_largest_key(count_ge, (1, tq), topk)

    def bias_body(kc, _):
        sel = jnp.logical_and(key_ref[kc] >= thr, causal_ok(kc))
        bias_ref[kc] = jnp.where(sel, 0.0, NEG).T
        return 0

    lax.fori_loop(0, i + 1, bias_body, 0)

    for n in range(nkv):
        lanes = slice(n * HEAD_DIM, (n + 1) * HEAD_DIM)
        q3 = jnp.concatenate(
            [q_ref[:, (n * GROUP + g) * HEAD_DIM:(n * GROUP + g + 1) * HEAD_DIM] for g in range(GROUP)], axis=0)
        chunk = lambda ref, kc, lanes=lanes: ref[0, pl.ds(pl.multiple_of(kc * ck, ck), ck), lanes]

        def logits_of(kc, q3=q3, chunk=chunk):
            s = lax.dot_general(q3, chunk(k_ref, kc), _CONTRACT_LAST, preferred_element_type=F32)
            return s * SOFTMAX_C2 + jnp.concatenate([bias_ref[kc]] * GROUP, axis=0)

        out = _masked_attention(GROUP * tq, i + 1, logits_of, functools.partial(chunk, v_ref), s_ref)
        for g in range(GROUP):
            o_ref[:, (n * GROUP + g) * HEAD_DIM:(n * GROUP + g + 1) * HEAD_DIM] = (
                out[g * tq:(g + 1) * tq].astype(o_ref.dtype))


def _dsa_prompt(qi, wi, ki, q, k, v, bsz, t, nkv, ih, topk):
    nt = t // QTILE
    qw = nkv * GROUP * HEAD_DIM
    kvw = nkv * HEAD_DIM
    rows = lambda w: pl.BlockSpec((QTILE, w), lambda b, i: (b * nt + i, 0))
    seq = lambda w: pl.BlockSpec((1, t, w), lambda b, i: (b, 0, 0))
    return pl.pallas_call(
        functools.partial(_dsa_prompt_kernel, topk=topk, nkv=nkv, ih=ih),
        grid=(bsz, nt),
        in_specs=[rows(ih * IDX_DIM), rows(128), seq(IDX_DIM), rows(qw), seq(kvw), seq(kvw)],
        out_specs=rows(qw),
        out_shape=jax.ShapeDtypeStruct((bsz * t, qw), BF16),
        scratch_shapes=[pltpu.VMEM((nt, QTILE, QTILE), I32), pltpu.VMEM((nt, QTILE, QTILE), F32),
                        pltpu.VMEM((nt, GROUP * QTILE, QTILE), F32)],
        compiler_params=_cparams("parallel", "arbitrary"),
        name="dsa_prompt_attn",
    )(qi, wi, ki.reshape(bsz, t, IDX_DIM), q, k.reshape(bsz, t, kvw), v.reshape(bsz, t, kvw))


def _topb_select_bias(gate, valid, blk_iota, nblk):
    g = jnp.where(valid, gate, -jnp.inf)
    rank = jnp.zeros(g.shape, I32)
    for m in range(nblk):
        gm = g[:, m:m + 1]
        beats = jnp.logical_or(gm > g, jnp.logical_and(gm == g, blk_iota > m))
        rank = rank + beats.astype(I32)
    sel = jnp.logical_and(valid, rank < MOBA_TOPB)
    return jnp.where(sel, 0.0, NEG)


def _topb_select_rows(gate_t, valid_t, nblk):
    g = jnp.where(valid_t, gate_t, -jnp.inf)
    blk = lax.broadcasted_iota(I32, g.shape, 0)
    rank = jnp.zeros(g.shape, I32)
    for m in range(nblk):
        gm = g[m:m + 1, :]
        beats = jnp.logical_or(gm > g, jnp.logical_and(gm == g, blk > m))
        rank = rank + beats.astype(I32)
    return jnp.logical_and(valid_t, rank < MOBA_TOPB)


def _moba_prompt_kernel(q_ref, k_ref, v_ref, km_ref, o_ref, s_ref, *, nkv, nblk):
    j = pl.program_id(1)
    tq = blk = QTILE
    rows = GROUP * tq
    row = lax.broadcasted_iota(I32, (tq, blk), 0)
    col = lax.broadcasted_iota(I32, (tq, blk), 1)
    cb = jnp.where(col <= row, 0.0, NEG)
    causal_bias = jnp.concatenate([cb] * GROUP, axis=0)
    valid_t = lax.broadcasted_iota(I32, (nblk, rows), 0) < j
    lane = lax.broadcasted_iota(I32, (blk, HEAD_DIM), 1)
    km_pad = jnp.zeros((16 - nblk % 16, HEAD_DIM), F32)
    for n in range(nkv):
        lanes = slice(n * HEAD_DIM, (n + 1) * HEAD_DIM)
        q3 = jnp.concatenate(
            [q_ref[:, (n * GROUP + g) * HEAD_DIM:(n * GROUP + g + 1) * HEAD_DIM] for g in range(GROUP)], axis=0)
        km = jnp.concatenate([km_ref[0, :, lanes], km_pad], axis=0).astype(BF16)
        gate_t = lax.dot_general(km, q3, _CONTRACT_LAST, preferred_element_type=F32)[:nblk]
        selb_t = jnp.where(_topb_select_rows(gate_t, valid_t, nblk), 0.0, NEG)
        selb = jnp.concatenate([selb_t, jnp.zeros((HEAD_DIM - nblk, rows), F32)], axis=0).T
        q_aug = jnp.concatenate([q3, selb.astype(BF16)], axis=1)
        chunk = lambda ref, m, lanes=lanes: ref[0, pl.ds(pl.multiple_of(m * blk, blk), blk), lanes]

        def logits_of(m, q_aug=q_aug, chunk=chunk):
            onehot = jnp.where(lane == m, 1.0, 0.0).astype(BF16)
            k_aug = jnp.concatenate([chunk(k_ref, m), onehot], axis=1)
            return lax.dot_general(q_aug, k_aug, _CONTRACT_LAST, preferred_element_type=F32) * SOFTMAX_C2

        own_s = lax.dot_general(q3, chunk(k_ref, j), _CONTRACT_LAST, preferred_element_type=F32) * SOFTMAX_C2
        out = _masked_attention(rows, j, logits_of, functools.partial(chunk, v_ref), s_ref,
                                own=(j, own_s + causal_bias, chunk(v_ref, j)))
        for g in range(GROUP):
            o_ref[:, (n * GROUP + g) * HEAD_DIM:(n * GROUP + g + 1) * HEAD_DIM] = (
                out[g * tq:(g + 1) * tq].astype(o_ref.dtype))


def _moba_prompt(q, k, v, kmean, bsz, t, nkv):
    nt = t // QTILE
    qw = nkv * GROUP * HEAD_DIM
    kvw = nkv * HEAD_DIM
    rows = lambda w: pl.BlockSpec((QTILE, w), lambda b, i: (b * nt + i, 0))
    seq = lambda w: pl.BlockSpec((1, t, w), lambda b, i: (b, 0, 0))
    return pl.pallas_call(
        functools.partial(_moba_prompt_kernel, nkv=nkv, nblk=nt),
        grid=(bsz, nt),
        in_specs=[rows(qw), seq(kvw), seq(kvw), pl.BlockSpec((1, nt, kvw), lambda b, i: (b, 0, 0))],
        out_specs=rows(qw),
        out_shape=jax.ShapeDtypeStruct((bsz * t, qw), BF16),
        scratch_shapes=[pltpu.VMEM((nt, GROUP * QTILE, QTILE), F32)],
        compiler_params=_cparams("parallel", "arbitrary"),
        name="moba_prompt_attn",
    )(q, k.reshape(bsz, t, kvw), v.reshape(bsz, t, kvw), kmean.reshape(bsz, nt, kvw))


def _mem_attn_kernel(q_ref, mk_ref, mv_ref, o_ref, *, hd):
    scale = hd ** -0.5
    for c in range(MEM_HEADS):
        lanes = slice(c * hd, (c + 1) * hd)
        q = q_ref[0, :, lanes].astype(BF16)
        mk = mk_ref[0, :, lanes].astype(BF16)
        mv = mv_ref[0, :, lanes].astype(BF16)
        s = lax.dot_general(q, mk, _CONTRACT_LAST, preferred_element_type=F32) * scale
        p = jnp.exp(s - jnp.max(s, axis=1, keepdims=True))
        l = jnp.sum(p, axis=1, keepdims=True)
        o = jnp.dot(p.astype(BF16), mv, preferred_element_type=F32) / l
        o_ref[0, :, lanes] = o.astype(o_ref.dtype)


def _mem_attn(qm, mk, mv, out_dtype):
    bsz, t, w = qm.shape
    mlen = mk.shape[1]
    tq = _pick_block(t, 512, 8)
    return pl.pallas_call(
        functools.partial(_mem_attn_kernel, hd=w // MEM_HEADS),
        grid=(bsz, t // tq),
        in_specs=[pl.BlockSpec((1, tq, w), lambda b, i: (b, i, 0)),
                  pl.BlockSpec((1, mlen, w), lambda b, i: (b, 0, 0)),
                  pl.BlockSpec((1, mlen, w), lambda b, i: (b, 0, 0))],
        out_specs=pl.BlockSpec((1, tq, w), lambda b, i: (b, i, 0)),
        out_shape=jax.ShapeDtypeStruct((bsz, t, w), out_dtype),
        compiler_params=_cparams("parallel", "parallel"),
        name="mem_attn",
    )(qm, mk, mv)


SAMPLE_PP = 8


def _page_specs(shape_tail, pp, layer):
    nd = len(shape_tail)
    return [pl.BlockSpec((None, None) + shape_tail,
                         lambda b, p, pt, c=c: (layer, pt[b, p * pp + c]) + (0,) * nd)
            for c in range(pp)]


def _head_sum(w, ih):
    acc = w[0:SAMPLE_TPAD]
    for h in range(1, ih):
        acc = acc + w[h * SAMPLE_TPAD:(h + 1) * SAMPLE_TPAD]
    return acc


def _dsa_sample_index_kernel(pt_ref, qi_ref, wi_ref, *refs, pp, ih, page):
    page_refs, o_ref = refs[:pp], refs[pp]
    qi = qi_ref[0].astype(BF16)
    wi = wi_ref[0]
    for c in range(pp):
        kp = page_refs[c][...].astype(BF16)
        s = lax.dot_general(qi, kp, _CONTRACT_LAST, preferred_element_type=F32)
        o_ref[0, :, c * page:(c + 1) * page] = _head_sum(jnp.maximum(s, 0.0) * wi, ih)


def _dsa_sample_index(page_table, qi_ht, wi_ht, kidx_pool, layer, ih):
    bs, npg = page_table.shape
    page = kidx_pool.shape[2]
    pp = SAMPLE_PP
    r = ih * SAMPLE_TPAD
    grid_spec = pltpu.PrefetchScalarGridSpec(
        num_scalar_prefetch=1,
        grid=(bs, npg // pp),
        in_specs=[pl.BlockSpec((1, r, IDX_DIM), lambda b, p, pt: (b, 0, 0)),
                  pl.BlockSpec((1, r, page), lambda b, p, pt: (b, 0, 0))]
        + _page_specs((page, IDX_DIM), pp, layer),
        out_specs=pl.BlockSpec((1, SAMPLE_TPAD, page * pp), lambda b, p, pt: (b, 0, p)),
    )
    return pl.pallas_call(
        functools.partial(_dsa_sample_index_kernel, pp=pp, ih=ih, page=page),
        grid_spec=grid_spec,
        out_shape=jax.ShapeDtypeStruct((bs, SAMPLE_TPAD, npg * page), F32),
        compiler_params=_cparams("parallel", "arbitrary"),
        name="dsa_sample_index",
    )(page_table, qi_ht, wi_ht, *([kidx_pool] * pp))


def _dsa_sample_select_kernel(sc_ref, qi_ref, wi_ref, kin_ref, o_ref, *, topk, ih, ts, past):
    tail_w = kin_ref.shape[1]
    s = lax.dot_general(qi_ref[0].astype(BF16), kin_ref[0].astype(BF16), _CONTRACT_LAST,
                        preferred_element_type=F32)
    tail = _head_sum(jnp.maximum(s, 0.0) * wi_ref[0][:, :tail_w], ih)
    row = lax.broadcasted_iota(I32, (SAMPLE_TPAD, tail_w), 0)
    col = lax.broadcasted_iota(I32, (SAMPLE_TPAD, tail_w), 1)
    tail_ok = jnp.logical_and(col <= row, col < ts)
    full = jnp.concatenate([sc_ref[0], jnp.where(tail_ok, tail, -jnp.inf)], axis=1)
    keys = _sortable_key(full)
    width = past + tail_w
    col_f = lax.broadcasted_iota(I32, (SAMPLE_TPAD, width), 1)
    row_f = lax.broadcasted_iota(I32, (SAMPLE_TPAD, width), 0)
    visible = jnp.logical_or(col_f < past, jnp.logical_and(col_f - past <= row_f, col_f - past < ts))

    def count_ge(cand):
        return jnp.sum((keys >= cand).astype(F32), axis=1, keepdims=True)

    thr = _kth_largest_key(count_ge, (SAMPLE_TPAD, 1), topk)
    o_ref[0] = jnp.where(jnp.logical_and(keys >= thr, visible), 0.0, NEG)


def _dsa_sample_select(scores, qi_ht, wi_ht, ki_new, ih, ts, topk):
    bs, _, past = scores.shape
    tail_w = ki_new.shape[1]
    r = ih * SAMPLE_TPAD
    return pl.pallas_call(
        functools.partial(_dsa_sample_select_kernel, topk=topk, ih=ih, ts=ts, past=past),
        grid=(bs,),
        in_specs=[pl.BlockSpec((1, SAMPLE_TPAD, past), lambda b: (b, 0, 0)),
                  pl.BlockSpec((1, r, IDX_DIM), lambda b: (b, 0, 0)),
                  pl.BlockSpec((1, r, wi_ht.shape[2]), lambda b: (b, 0, 0)),
                  pl.BlockSpec((1, tail_w, IDX_DIM), lambda b: (b, 0, 0))],
        out_specs=pl.BlockSpec((1, SAMPLE_TPAD, past + tail_w), lambda b: (b, 0, 0)),
        out_shape=jax.ShapeDtypeStruct((bs, SAMPLE_TPAD, past + tail_w), F32),
        compiler_params=_cparams("parallel"),
        name="dsa_sample_select",
    )(scores, qi_ht, wi_ht, ki_new)


def _sample_flash_update(qbd, kblk, vblk, bias, m_ref, l_ref, acc_ref, nkv):
    rows_per_head = GROUP * SAMPLE_TPAD
    s = lax.dot_general(qbd, kblk, _CONTRACT_LAST, preferred_element_type=F32) * (HEAD_DIM ** -0.5) + bias
    m_old = m_ref[...]
    m_new = jnp.maximum(m_old, jnp.max(s, axis=1, keepdims=True))
    alpha = jnp.exp(m_old - m_new)
    p = jnp.exp(s - m_new)
    l_ref[...] = alpha * l_ref[...] + jnp.sum(p, axis=1, keepdims=True)
    m_ref[...] = m_new
    o_full = jnp.dot(p.astype(BF16), vblk, preferred_element_type=F32)
    o_diag = jnp.concatenate(
        [o_full[n * rows_per_head:(n + 1) * rows_per_head, n * HEAD_DIM:(n + 1) * HEAD_DIM] for n in range(nkv)],
        axis=0)
    acc_ref[...] = alpha * acc_ref[...] + o_diag


def _sample_flash_reset(m_ref, l_ref, acc_ref):
    m_ref[...] = jnp.full(m_ref.shape, NEG, F32)
    l_ref[...] = jnp.zeros(l_ref.shape, F32)
    acc_ref[...] = jnp.zeros(acc_ref.shape, F32)


def _heads_on_lanes(refs, nkv):
    keys = refs[0].shape[0] // nkv
    pages = [jnp.concatenate([r[pl.ds(n, keys, stride=nkv), :] for n in range(nkv)], axis=1).astype(BF16)
             for r in refs]
    return jnp.concatenate(pages, axis=0)


def _flat_pool(pool):
    return pool.reshape(pool.shape[0], pool.shape[1], pool.shape[2] * pool.shape[3], pool.shape[4])


def _tile_query_bias(b8, reps):
    return jnp.concatenate([b8] * reps, axis=0)


def _dsa_sample_attn_kernel(pt_ref, q_ref, bias_ref, tbias_ref, kn_ref, vn_ref, *refs, pp, nkv):
    k_refs, v_refs = refs[:pp], refs[pp:2 * pp]
    o_ref, m_ref, l_ref, acc_ref = refs[2 * pp:]
    p = pl.program_id(1)
    reps = nkv * GROUP

    @pl.when(p == 0)
    def _():
        _sample_flash_reset(m_ref, l_ref, acc_ref)

    _sample_flash_update(q_ref[0], _heads_on_lanes(k_refs, nkv), _heads_on_lanes(v_refs, nkv),
                         _tile_query_bias(bias_ref[0], reps), m_ref, l_ref, acc_ref, nkv)

    @pl.when(p == pl.num_programs(1) - 1)
    def _():
        _sample_flash_update(q_ref[0], kn_ref[0].astype(BF16), vn_ref[0].astype(BF16),
                             _tile_query_bias(tbias_ref[0], reps), m_ref, l_ref, acc_ref, nkv)
        o_ref[0] = acc_ref[...] / l_ref[...]


def _sample_attn_specs(nkv, tail_rows):
    rows = nkv * GROUP * SAMPLE_TPAD
    kvw = nkv * HEAD_DIM
    q_spec = pl.BlockSpec((1, rows, kvw), lambda b, p, pt: (b, 0, 0))
    tail_spec = pl.BlockSpec((1, tail_rows, kvw), lambda b, p, pt: (b, 0, 0))
    out_spec = pl.BlockSpec((1, rows, HEAD_DIM), lambda b, p, pt: (b, 0, 0))
    scratch = [pltpu.VMEM((rows, 1), F32), pltpu.VMEM((rows, 1), F32), pltpu.VMEM((rows, HEAD_DIM), F32)]
    return rows, q_spec, tail_spec, out_spec, scratch


def _dsa_sample_attn(page_table, qbd, bias, k_new, v_new, k_pool, v_pool, layer, nkv):
    bs, npg = page_table.shape
    page = k_pool.shape[2]
    pp = SAMPLE_PP
    rows, q_spec, tail_spec, out_spec, scratch = _sample_attn_specs(nkv, k_new.shape[1])
    grid_spec = pltpu.PrefetchScalarGridSpec(
        num_scalar_prefetch=1,
        grid=(bs, npg // pp),
        in_specs=[q_spec,
                  pl.BlockSpec((1, SAMPLE_TPAD, page * pp), lambda b, p, pt: (b, 0, p)),
                  pl.BlockSpec((1, SAMPLE_TPAD, page), lambda b, p, pt: (b, 0, npg)),
                  tail_spec, tail_spec]
        + _page_specs((page * nkv, HEAD_DIM), pp, layer) + _page_specs((page * nkv, HEAD_DIM), pp, layer),
        out_specs=out_spec,
        scratch_shapes=scratch,
    )
    return pl.pallas_call(
        functools.partial(_dsa_sample_attn_kernel, pp=pp, nkv=nkv),
        grid_spec=grid_spec,
        out_shape=jax.ShapeDtypeStruct((bs, rows, HEAD_DIM), F32),
        compiler_params=_cparams("parallel", "arbitrary"),
        name="dsa_sample_attn",
    )(page_table, qbd, bias, bias, k_new, v_new, *([_flat_pool(k_pool)] * pp), *([_flat_pool(v_pool)] * pp))


KMEAN_BLOCKS = 8


def _moba_kmean_kernel(pt_ref, *refs, nblk, ppb):
    k_refs, o_ref = refs[:nblk * ppb], refs[nblk * ppb]
    for blk in range(nblk):
        acc = jnp.sum(k_refs[blk * ppb][...], axis=0)
        for c in range(1, ppb):
            acc = acc + jnp.sum(k_refs[blk * ppb + c][...], axis=0)
        o_ref[0, blk] = acc * (1.0 / MOBA_BLOCK)


def _moba_sample_kmean(page_table, k_pool, layer, nkv):
    bs, npg = page_table.shape
    page = k_pool.shape[2]
    ppb = MOBA_BLOCK // page
    nb = npg // ppb
    nblk = KMEAN_BLOCKS
    grid_spec = pltpu.PrefetchScalarGridSpec(
        num_scalar_prefetch=1,
        grid=(bs, nb // nblk),
        in_specs=_page_specs((page, nkv, HEAD_DIM), nblk * ppb, layer),
        out_specs=pl.BlockSpec((1, nblk, nkv, HEAD_DIM), lambda b, p, pt: (b, p, 0, 0)),
    )
    return pl.pallas_call(
        functools.partial(_moba_kmean_kernel, nblk=nblk, ppb=ppb),
        grid_spec=grid_spec,
        out_shape=jax.ShapeDtypeStruct((bs, nb, nkv, HEAD_DIM), F32),
        compiler_params=_cparams("parallel", "arbitrary"),
        name="moba_sample_kmean",
    )(page_table, *([k_pool] * (nblk * ppb)))


def _moba_sample_attn_kernel(pt_ref, q_ref, km_ref, kn_ref, vn_ref, *refs, pp, nkv, page, nb, ts):
    k_refs, v_refs = refs[:pp], refs[pp:2 * pp]
    o_ref, m_ref, l_ref, acc_ref, sb_ref = refs[2 * pp:]
    p = pl.program_id(1)
    rows = nkv * GROUP * SAMPLE_TPAD
    ppb = MOBA_BLOCK // page
    blk_iota = lax.broadcasted_iota(I32, (rows, nb), 1)

    @pl.when(p == 0)
    def _():
        _sample_flash_reset(m_ref, l_ref, acc_ref)
        valid = blk_iota >= 0
        gate = lax.dot_general(q_ref[0], km_ref[0].astype(BF16), _CONTRACT_LAST, preferred_element_type=F32)
        sb_ref[...] = _topb_select_bias(gate, valid, blk_iota, nb)

    selb = sb_ref[...]
    cols = []
    for c in range(pp // ppb):
        blk = p * (pp // ppb) + c
        col = jnp.sum(jnp.where(blk_iota == blk, selb, 0.0), axis=1, keepdims=True)
        cols.append(jnp.broadcast_to(col, (rows, MOBA_BLOCK)))
    _sample_flash_update(q_ref[0], _heads_on_lanes(k_refs, nkv), _heads_on_lanes(v_refs, nkv),
                         jnp.concatenate(cols, axis=1), m_ref, l_ref, acc_ref, nkv)

    @pl.when(p == pl.num_programs(1) - 1)
    def _():
        tail_w = kn_ref.shape[1]
        row = lax.broadcasted_iota(I32, (rows, tail_w), 0) & (SAMPLE_TPAD - 1)
        col = lax.broadcasted_iota(I32, (rows, tail_w), 1)
        tb = jnp.where(jnp.logical_and(col <= row, col < ts), 0.0, NEG)
        _sample_flash_update(q_ref[0], kn_ref[0].astype(BF16), vn_ref[0].astype(BF16), tb,
                             m_ref, l_ref, acc_ref, nkv)
        o_ref[0] = acc_ref[...] / l_ref[...]


def _moba_sample_attn(page_table, qbd, kmean, k_new, v_new, k_pool, v_pool, layer, nkv, ts):
    bs, npg = page_table.shape
    page = k_pool.shape[2]
    kvw = nkv * HEAD_DIM
    pp = SAMPLE_PP
    nb = kmean.shape[1]
    rows, q_spec, tail_spec, out_spec, scratch = _sample_attn_specs(nkv, k_new.shape[1])
    grid_spec = pltpu.PrefetchScalarGridSpec(
        num_scalar_prefetch=1,
        grid=(bs, npg // pp),
        in_specs=[q_spec, pl.BlockSpec((1, nb, kvw), lambda b, p, pt: (b, 0, 0)), tail_spec, tail_spec]
        + _page_specs((page * nkv, HEAD_DIM), pp, layer) + _page_specs((page * nkv, HEAD_DIM), pp, layer),
        out_specs=out_spec,
        scratch_shapes=scratch + [pltpu.VMEM((rows, nb), F32)],
    )
    return pl.pallas_call(
        functools.partial(_moba_sample_attn_kernel, pp=pp, nkv=nkv, page=page, nb=nb, ts=ts),
        grid_spec=grid_spec,
        out_shape=jax.ShapeDtypeStruct((bs, rows, HEAD_DIM), F32),
        compiler_params=_cparams("parallel", "arbitrary"),
        name="moba_sample_attn",
    )(page_table, qbd, kmean, k_new, v_new, *([_flat_pool(k_pool)] * pp), *([_flat_pool(v_pool)] * pp))


def _heads_to_kv_major(q, bs, nkv):
    x = q.reshape(bs, SAMPLE_TPAD, nkv, GROUP, HEAD_DIM).transpose(0, 2, 3, 1, 4)
    eye = jnp.eye(nkv, dtype=q.dtype)
    bd = x[:, :, :, :, None, :] * eye[None, :, None, None, :, None]
    return bd.reshape(bs, nkv * GROUP * SAMPLE_TPAD, nkv * HEAD_DIM).astype(BF16)


def _kv_major_to_rows(o, bs, nkv):
    x = o.reshape(bs, nkv, GROUP, SAMPLE_TPAD, HEAD_DIM)
    return x.transpose(0, 3, 1, 2, 4).reshape(bs * SAMPLE_TPAD, nkv * GROUP * HEAD_DIM)


def _pad_rows(x, n):
    return jnp.pad(x, ((0, 0), (0, n - x.shape[1]), (0, 0)))


def _merge_mixers(h, mix, qm, mk, mv, bsz):
    m = h.shape[0]
    ma = _mem_attn(qm.reshape(bsz, m // bsz, -1), mk, mv, qm.dtype).reshape(m, -1)
    return jnp.concatenate([mix.astype(BF16), ma.astype(BF16)], axis=1)


def _finish_layer(hp, hs, merged_p, merged_s, layer, w_o, g_ffn, w_up, w_down):
    hp, hs = _matmul(merged_p, w_o, layer, F32, epilogue="residual", residual=hp, a2=merged_s, residual2=hs)
    up, us = _matmul(_rmsnorm(hp, g_ffn, BF16), w_up, layer, BF16, epilogue="relu2", a2=_rmsnorm(hs, g_ffn, BF16))
    return _matmul(up, w_down, layer, F32, epilogue="residual", residual=hp, a2=us, residual2=hs)


def kernel(x_prompt, x_sample, cache_dsa_k, cache_dsa_v, cache_dsa_kidx, cache_moba_k, cache_moba_v, cache_mem_k, cache_mem_v, page_table, mem_prompt, norm_mix, norm_mem, w_in_dsa, w_in_moba, w_mem_kv, w_out, norm_ffn, w_up, w_down, norm_final):
    bsz, t, d = x_prompt.shape
    bs, ts, _ = x_sample.shape
    depth = norm_mix.shape[0]
    nh = (3 * d) // (4 * HEAD_DIM)
    nkv = nh // GROUP
    qw, kvw = nh * HEAD_DIM, nkv * HEAD_DIM
    memw = d // 4
    mhd = memw // MEM_HEADS
    ih = d // 128
    npg = page_table.shape[1]
    page = cache_dsa_k.shape[2]
    past = npg * page
    mlen = mem_prompt.shape[1]
    tp = SAMPLE_TPAD

    hp = x_prompt.reshape(bsz * t, d)
    hs = jnp.pad(x_sample, ((0, 0), (0, tp - ts), (0, 0))).reshape(bs * tp, d)
    pos_p = jnp.arange(t, dtype=I32)
    pos_s = past + (jnp.arange(tp, dtype=I32) % ts)
    tabs_p = _rope_tables(pos_p)
    tabs_s = _rope_tables(jnp.tile(pos_s, bs))

    outs = {name: [] for name in ("pdk", "pdv", "pdki", "sdk", "sdv", "sdki", "pmk", "pmv", "smk", "smv", "mk", "mv")}
    mem_flat = mem_prompt.reshape(bsz * mlen, d)

    w_o, wu, wd, w_mem = w_out.astype(BF16), w_up.astype(BF16), w_down.astype(BF16), w_mem_kv.astype(BF16)
    w_dsa, w_moba = w_in_dsa.astype(BF16), w_in_moba.astype(BF16)
    c0 = qw + 2 * kvw + ih * IDX_DIM
    w_dsa_tail = jnp.concatenate(
        [w_in_dsa[:, :, c0 + ih:c0 + ih + IDX_DIM], w_in_dsa[:, :, c0 + ih + IDX_DIM:], w_in_dsa[:, :, c0:c0 + ih],
         jnp.zeros(w_in_dsa.shape[:2] + (128 - ih,), F32)], axis=2).astype(BF16)

    for i in range(depth):
        j = i // 2
        mkv = _matmul(_rmsnorm(mem_flat, norm_mem[i], BF16), w_mem, i, F32)
        mk_p = mkv[:, :memw].reshape(bsz, mlen, memw)
        mv_p = mkv[:, memw:].reshape(bsz, mlen, memw)
        outs["mk"].append(mk_p.reshape(bsz, mlen, MEM_HEADS, mhd))
        outs["mv"].append(mv_p.reshape(bsz, mlen, MEM_HEADS, mhd))
        hn_p = _rmsnorm(hp, norm_mix[i], BF16)
        hn_s = _rmsnorm(hs, norm_mix[i], BF16)
        mk_s = cache_mem_k[i].reshape(bs, mlen, memw)
        mv_s = cache_mem_v[i].reshape(bs, mlen, memw)
        if i % 2 == 0:
            pa, pa_s = _matmul(hn_p, w_dsa, j, F32, n=c0, a2=hn_s)
            pb, pb_s = _matmul(hn_p, w_dsa_tail, j, F32, a2=hn_s)
            q, kf, kb, vf, vb, qi, kif, kib, wi, qm = _dsa_split(pa, pb, tabs_p, t, nh, nkv, ih, memw, BF16)
            outs["pdk"].append(kf.reshape(bsz, t, nkv, HEAD_DIM))
            outs["pdv"].append(vf.reshape(bsz, t, nkv, HEAD_DIM))
            outs["pdki"].append(kif.reshape(bsz, t, IDX_DIM))
            mix_p = _dsa_prompt(qi, wi, kib, q, kb, vb, bsz, t, nkv, ih, min(DSA_TOPK, t // 4))
            q_s, kf, _, vf, _, qi_s, kif, _, wi_s, qm_s = _dsa_split(pa_s, pb_s, tabs_s, bs * tp, nh, nkv, ih, memw, F32)
            outs["sdk"].append(kf.reshape(bs, tp, nkv, HEAD_DIM)[:, :ts])
            outs["sdv"].append(vf.reshape(bs, tp, nkv, HEAD_DIM)[:, :ts])
            outs["sdki"].append(kif.reshape(bs, tp, IDX_DIM)[:, :ts])
            qi_ht = qi_s.reshape(bs, tp, ih, IDX_DIM).transpose(0, 2, 1, 3).reshape(bs, ih * tp, IDX_DIM)
            wi_ht = wi_s.reshape(bs, tp, 128)[:, :, :ih].transpose(0, 2, 1).reshape(bs, ih * tp, 1)
            wi_ht = jnp.broadcast_to(wi_ht, (bs, ih * tp, page))
            scores = _dsa_sample_index(page_table, qi_ht, wi_ht, cache_dsa_kidx, j, ih)
            ki_new = _pad_rows(kif.reshape(bs, tp, IDX_DIM), page)
            bias = _dsa_sample_select(scores, qi_ht, wi_ht, ki_new, ih, ts, min(DSA_TOPK, (past + ts) // 4))
            o_s = _dsa_sample_attn(page_table, _heads_to_kv_major(q_s, bs, nkv), bias,
                                   _pad_rows(kf.reshape(bs, tp, kvw), page), _pad_rows(vf.reshape(bs, tp, kvw), page),
                                   cache_dsa_k, cache_dsa_v, j, nkv)
            mix_s = _kv_major_to_rows(o_s, bs, nkv)
        else:
            p, p_s = _matmul(hn_p, w_moba, j, F32, a2=hn_s)
            q, kf, kb, vf, vb, qm, kmean = _moba_split(p, tabs_p, t, nh, nkv, memw, BF16)
            outs["pmk"].append(kf.reshape(bsz, t, nkv, HEAD_DIM))
            outs["pmv"].append(vf.reshape(bsz, t, nkv, HEAD_DIM))
            mix_p = _moba_prompt(q, kb, vb, kmean, bsz, t, nkv)
            q_s, kf, _, vf, _, qm_s, _ = _moba_split(p_s, tabs_s, bs * tp, nh, nkv, memw, F32)
            outs["smk"].append(kf.reshape(bs, tp, nkv, HEAD_DIM)[:, :ts])
            outs["smv"].append(vf.reshape(bs, tp, nkv, HEAD_DIM)[:, :ts])
            kmean_s = _moba_sample_kmean(page_table, cache_moba_k, j, nkv)
            kmean_s = kmean_s.reshape(bs, kmean_s.shape[1], kvw)
            o_s = _moba_sample_attn(page_table, _heads_to_kv_major(q_s, bs, nkv), kmean_s,
                                    _pad_rows(kf.reshape(bs, tp, kvw), page), _pad_rows(vf.reshape(bs, tp, kvw), page),
                                    cache_moba_k, cache_moba_v, j, nkv, ts)
            mix_s = _kv_major_to_rows(o_s, bs, nkv)
        hp, hs = _finish_layer(hp, hs, _merge_mixers(hp, mix_p, qm, mk_p, mv_p, bsz),
                               _merge_mixers(hs, mix_s, qm_s, mk_s, mv_s, bs), i, w_o, norm_ffn[i], wu, wd)

    y_prompt = _rmsnorm(hp, norm_final, F32).reshape(bsz, t, d)
    y_sample = _rmsnorm(hs, norm_final, F32).reshape(bs, tp, d)[:, :ts]
    st = jnp.stack
    return (y_prompt, y_sample, st(outs["pdk"]), st(outs["pdv"]), st(outs["pdki"]), st(outs["pmk"]), st(outs["pmv"]),
            st(outs["mk"]), st(outs["mv"]), st(outs["sdk"]), st(outs["sdv"]), st(outs["sdki"]),
            st(outs["smk"]), st(outs["smv"]))
```

```python
import functools

import jax
import jax.numpy as jnp
from jax import lax
from jax.experimental import pallas as pl
from jax.experimental.pallas import tpu as pltpu

F32 = jnp.float32
BF16 = jnp.bfloat16
I32 = jnp.int32

HEAD_DIM = 128
GROUP = 3
ROT_DIM = 32
ROPE_THETA = 500000.0
IDX_DIM = 128
MEM_HEADS = 4
DSA_TOPK = 256
MOBA_BLOCK = 256
MOBA_TOPB = 3
EPS = 1e-6
NEG = -1e30
LOG2E = 1.4426950408889634
SOFTMAX_C2 = (HEAD_DIM ** -0.5) * LOG2E
QTILE = 256
SAMPLE_TPAD = 8
VMEM_LIMIT = 60 * 1024 * 1024

_CONTRACT_LAST = (((1,), (1,)), ((), ()))


def _cparams(*sem):
    return pltpu.CompilerParams(dimension_semantics=sem, vmem_limit_bytes=VMEM_LIMIT)


def _pick_block(n, pref, align=128):
    if n <= pref:
        return n
    b = (pref // align) * align
    while b >= align:
        if n % b == 0:
            return b
        b -= align
    return n


def _rmsnorm_kernel(x_ref, g_ref, o_ref):
    x = x_ref[...]
    ms = jnp.mean(x * x, axis=-1, keepdims=True)
    o_ref[...] = ((x * lax.rsqrt(ms + EPS)) * g_ref[...]).astype(o_ref.dtype)


def _rmsnorm(x, g, out_dtype):
    m, d = x.shape
    bm = _pick_block(m, 256, 8)
    return pl.pallas_call(
        _rmsnorm_kernel,
        grid=(m // bm,),
        in_specs=[pl.BlockSpec((bm, d), lambda i: (i, 0)), pl.BlockSpec((1, d), lambda i: (0, 0))],
        out_specs=pl.BlockSpec((bm, d), lambda i: (i, 0)),
        out_shape=jax.ShapeDtypeStruct((m, d), out_dtype),
        compiler_params=_cparams("parallel"),
        name="rmsnorm",
    )(x, g.reshape(1, d).astype(F32))


def _mm_tile(a_ref, w, r_ref, o_ref, k, nk, epilogue):
    part = jnp.dot(a_ref[...], w, preferred_element_type=F32)

    def finish(acc):
        if epilogue == "relu2":
            r = jnp.maximum(acc, 0.0)
            acc = r * r
        elif epilogue == "residual":
            acc = r_ref[...] + acc
        o_ref[...] = acc.astype(o_ref.dtype)

    if nk == 1:
        finish(part)
        return

    @pl.when(k == 0)
    def _():
        o_ref[...] = part

    @pl.when(jnp.logical_and(k > 0, k < nk - 1))
    def _():
        o_ref[...] += part

    @pl.when(k == nk - 1)
    def _():
        finish(o_ref[...] + part)


def _mm_kernel(*refs, nk, epilogue, dual):
    refs = list(refs)
    a_ref, w_ref = refs.pop(0), refs.pop(0)
    r_ref = refs.pop(0) if epilogue == "residual" else None
    a2_ref = refs.pop(0) if dual else None
    r2_ref = refs.pop(0) if dual and epilogue == "residual" else None
    o_ref = refs.pop(0)
    o2_ref = refs.pop(0) if dual else None
    k = pl.program_id(2)
    w = w_ref[...]
    _mm_tile(a_ref, w, r_ref, o_ref, k, nk, epilogue)
    if dual:
        @pl.when(pl.program_id(1) == 0)
        def _():
            _mm_tile(a2_ref, w, r2_ref, o2_ref, k, nk, epilogue)


def _matmul(a, w, layer, out_dtype, epilogue=None, residual=None, n=None, a2=None, residual2=None,
            bm_pref=1024, bn_pref=1024, bk_pref=4096):
    m, kdim = a.shape
    n = w.shape[2] if n is None else n
    bm = _pick_block(m, bm_pref, 8)
    bn = _pick_block(n, bn_pref)
    bk = _pick_block(kdim, bk_pref)
    nk = kdim // bk
    dual = a2 is not None
    assert nk == 1 or out_dtype == F32, "a split contraction accumulates in the f32 output block"
    in_specs = [pl.BlockSpec((bm, bk), lambda j, i, k: (i, k)),
                pl.BlockSpec((None, bk, bn), lambda j, i, k: (layer, k, j))]
    args = [a, w]
    out_specs = [pl.BlockSpec((bm, bn), lambda j, i, k: (i, j))]
    out_shape = [jax.ShapeDtypeStruct((m, n), out_dtype)]
    if epilogue == "residual":
        in_specs.append(pl.BlockSpec((bm, bn), lambda j, i, k: (i, j)))
        args.append(residual)
    if dual:
        m2 = a2.shape[0]
        in_specs.append(pl.BlockSpec((m2, bk), lambda j, i, k: (0, k)))
        args.append(a2)
        if epilogue == "residual":
            in_specs.append(pl.BlockSpec((m2, bn), lambda j, i, k: (0, j)))
            args.append(residual2)
        out_specs.append(pl.BlockSpec((m2, bn), lambda j, i, k: (0, j)))
        out_shape.append(jax.ShapeDtypeStruct((m2, n), out_dtype))
    out = pl.pallas_call(
        functools.partial(_mm_kernel, nk=nk, epilogue=epilogue, dual=dual),
        grid=(n // bn, m // bm, nk),
        in_specs=in_specs,
        out_specs=out_specs,
        out_shape=out_shape,
        compiler_params=_cparams("parallel", "arbitrary" if dual else "parallel", "arbitrary"),
        name="matmul",
    )(*args)
    return tuple(out) if dual else out[0]


def _rope_tables(pos):
    half = ROT_DIM // 2
    inv = ROPE_THETA ** (-jnp.arange(half, dtype=F32) / half)
    ang = pos.astype(F32)[:, None] * inv[None, :]
    cos, sin = jnp.cos(ang), jnp.sin(ang)
    r = pos.shape[0]
    z16 = jnp.zeros((r, half), F32)
    zrest = jnp.zeros((r, HEAD_DIM - ROT_DIM), F32)
    c = jnp.concatenate([cos, cos, jnp.ones((r, HEAD_DIM - ROT_DIM), F32)], axis=1)
    sm = jnp.concatenate([-sin, z16, zrest], axis=1)
    sp = jnp.concatenate([z16, sin, zrest], axis=1)
    return c, sm, sp


def _rope_head(x, c, sm, sp):
    half = ROT_DIM // 2
    return x * c + pltpu.roll(x, HEAD_DIM - half, 1) * sm + pltpu.roll(x, half, 1) * sp


def _dsa_split_kernel(pa_ref, pb_ref, c_ref, sm_ref, sp_ref,
                      q_ref, kf_ref, kb_ref, vf_ref, vb_ref, qi_ref, kif_ref, kib_ref, wi_ref, qm_ref,
                      *, nh, nkv, ih, memw):
    c, sm, sp = c_ref[...], sm_ref[...], sp_ref[...]
    qw, kvw = nh * HEAD_DIM, nkv * HEAD_DIM
    for h in range(nh):
        sl = slice(h * HEAD_DIM, (h + 1) * HEAD_DIM)
        q_ref[:, sl] = _rope_head(pa_ref[:, sl], c, sm, sp).astype(q_ref.dtype)
    for h in range(nkv):
        sl = slice(h * HEAD_DIM, (h + 1) * HEAD_DIM)
        kr = _rope_head(pa_ref[:, qw + h * HEAD_DIM: qw + (h + 1) * HEAD_DIM], c, sm, sp)
        kf_ref[:, sl] = kr
        kb_ref[:, sl] = kr.astype(kb_ref.dtype)
    v = pa_ref[:, qw + kvw: qw + 2 * kvw]
    vf_ref[...] = v
    vb_ref[...] = v.astype(vb_ref.dtype)
    base = qw + 2 * kvw
    for h in range(ih):
        sl = slice(h * IDX_DIM, (h + 1) * IDX_DIM)
        qi_ref[:, sl] = _rope_head(pa_ref[:, base + h * IDX_DIM: base + (h + 1) * IDX_DIM], c, sm, sp).astype(qi_ref.dtype)
    ki = _rope_head(pb_ref[:, 0:IDX_DIM], c, sm, sp)
    kif_ref[...] = ki
    kib_ref[...] = ki.astype(kib_ref.dtype)
    qm_ref[...] = pb_ref[:, IDX_DIM:IDX_DIM + memw].astype(qm_ref.dtype)
    wi_ref[...] = pb_ref[:, IDX_DIM + memw:IDX_DIM + memw + 128] * ((ih * IDX_DIM) ** -0.5)


def _dsa_split(pa, pb, tabs, rows_per_seq, nh, nkv, ih, memw, act_dtype):
    m = pa.shape[0]
    bm = _pick_block(rows_per_seq, 128, 8)
    nt = rows_per_seq // bm
    qw, kvw = nh * HEAD_DIM, nkv * HEAD_DIM
    row = lambda w: pl.BlockSpec((bm, w), lambda i: (i, 0))
    tab = pl.BlockSpec((bm, HEAD_DIM), lambda i: (i % nt, 0))
    shapes = [(qw, act_dtype), (kvw, F32), (kvw, act_dtype), (kvw, F32), (kvw, act_dtype),
              (ih * IDX_DIM, act_dtype), (IDX_DIM, F32), (IDX_DIM, act_dtype), (128, F32), (memw, act_dtype)]
    return pl.pallas_call(
        functools.partial(_dsa_split_kernel, nh=nh, nkv=nkv, ih=ih, memw=memw),
        grid=(m // bm,),
        in_specs=[row(pa.shape[1]), row(pb.shape[1]), tab, tab, tab],
        out_specs=[row(w) for w, _ in shapes],
        out_shape=[jax.ShapeDtypeStruct((m, w), dt) for w, dt in shapes],
        compiler_params=_cparams("parallel"),
        name="dsa_split",
    )(pa, pb, *tabs)


def _moba_split_kernel(p_ref, c_ref, sm_ref, sp_ref, q_ref, kf_ref, kb_ref, vf_ref, vb_ref, qm_ref, km_ref,
                       *, nh, nkv, memw):
    c, sm, sp = c_ref[...], sm_ref[...], sp_ref[...]
    qw, kvw = nh * HEAD_DIM, nkv * HEAD_DIM
    for h in range(nh):
        sl = slice(h * HEAD_DIM, (h + 1) * HEAD_DIM)
        q_ref[:, sl] = _rope_head(p_ref[:, sl], c, sm, sp).astype(q_ref.dtype)
    rows = p_ref.shape[0]
    for h in range(nkv):
        sl = slice(h * HEAD_DIM, (h + 1) * HEAD_DIM)
        kr = _rope_head(p_ref[:, qw + h * HEAD_DIM: qw + (h + 1) * HEAD_DIM], c, sm, sp)
        kf_ref[:, sl] = kr
        kb_ref[:, sl] = kr.astype(kb_ref.dtype)
        km_ref[0, :, sl] = jnp.sum(kr, axis=0, keepdims=True) * (1.0 / rows)
    v = p_ref[:, qw + kvw: qw + 2 * kvw]
    vf_ref[...] = v
    vb_ref[...] = v.astype(vb_ref.dtype)
    qm_ref[...] = p_ref[:, qw + 2 * kvw: qw + 2 * kvw + memw].astype(qm_ref.dtype)


def _moba_split(p, tabs, rows_per_seq, nh, nkv, memw, act_dtype):
    m = p.shape[0]
    bm = _pick_block(rows_per_seq, MOBA_BLOCK, 8)
    nt = rows_per_seq // bm
    qw, kvw = nh * HEAD_DIM, nkv * HEAD_DIM
    row = lambda w: pl.BlockSpec((bm, w), lambda i: (i, 0))
    tab = pl.BlockSpec((bm, HEAD_DIM), lambda i: (i % nt, 0))
    shapes = [(qw, act_dtype), (kvw, F32), (kvw, act_dtype), (kvw, F32), (kvw, act_dtype), (memw, act_dtype)]
    return pl.pallas_call(
        functools.partial(_moba_split_kernel, nh=nh, nkv=nkv, memw=memw),
        grid=(m // bm,),
        in_specs=[row(p.shape[1]), tab, tab, tab],
        out_specs=[row(w) for w, _ in shapes] + [pl.BlockSpec((1, 1, kvw), lambda i: (i, 0, 0))],
        out_shape=[jax.ShapeDtypeStruct((m, w), dt) for w, dt in shapes]
        + [jax.ShapeDtypeStruct((m // bm, 1, kvw), F32)],
        compiler_params=_cparams("parallel"),
        name="moba_split",
    )(p, *tabs)


def _sortable_key(x):
    bits = pltpu.bitcast(x, I32)
    return jnp.where(bits < 0, bits ^ jnp.int32(0x7FFFFFFF), bits)


def _kth_largest_key(count_ge, shape, k):
    imin = jnp.int32(-2 ** 31)
    c0 = count_ge(jnp.zeros(shape, I32))
    thr = jnp.where(c0 >= k, jnp.int32(0), imin)

    def bit_body(it, thr):
        cand = thr + jnp.left_shift(jnp.int32(1), jnp.int32(30) - it)
        return jnp.where(count_ge(cand) >= k, cand, thr)

    return lax.fori_loop(0, 31, bit_body, thr)


def _lane_fold(x, acc, op):
    for w in range(x.shape[1] // 128):
        acc = op(acc, x[:, w * 128:(w + 1) * 128])
    return acc


def _paired_loop(n, body, carry):
    start = 0
    for width in (4, 2, 1):
        def group(g, carry, start=start, width=width):
            for u in range(width):
                carry = body(start + g * width + u, carry)
            return carry
        ngroups = (n - start) // width
        carry = lax.fori_loop(0, ngroups, group, carry)
        start = start + ngroups * width
    return carry


def _masked_attention(rows, n_dyn, logits_of, load_v, s_ref, own=None):
    mx = jnp.full((rows, 128), NEG, F32)
    if own is not None:
        slot, own_s, own_v = own
        s_ref[slot] = own_s
        mx = _lane_fold(own_s, mx, jnp.maximum)

    def pass_a(c, mx):
        s = logits_of(c)
        s_ref[c] = s
        return _lane_fold(s, mx, jnp.maximum)

    mx = _paired_loop(n_dyn, pass_a, mx)
    m = jnp.max(mx, axis=1, keepdims=True)

    def accumulate(c, vb, carry):
        lsum, acc = carry
        p = jnp.exp2(s_ref[c] - m)
        return _lane_fold(p, lsum, jnp.add), acc + jnp.dot(p.astype(BF16), vb, preferred_element_type=F32)

    carry = (jnp.zeros((rows, 128), F32), jnp.zeros((rows, HEAD_DIM), F32))
    if own is not None:
        carry = accumulate(slot, own_v, carry)
    lsum, acc = _paired_loop(n_dyn, lambda c, carry: accumulate(c, load_v(c), carry), carry)
    return acc / jnp.sum(lsum, axis=1, keepdims=True)


def _dsa_prompt_kernel(qi_ref, wi_ref, ki_ref, q_ref, k_ref, v_ref, o_ref, key_ref, bias_ref, s_ref,
                       *, topk, nkv, ih):
    i = pl.program_id(1)
    tq = ck = QTILE
    key_i = lax.broadcasted_iota(I32, (ck, tq), 0)
    qry_i = lax.broadcasted_iota(I32, (ck, tq), 1)
    wi_t = wi_ref[...].T

    def causal_ok(kc):
        return key_i <= qry_i + jnp.where(kc < i, ck, 0)

    def idx_body(kc, _):
        off = pl.multiple_of(kc * ck, ck)
        kic = ki_ref[0, pl.ds(off, ck), :]
        acc = jnp.zeros((ck, tq), F32)
        for h in range(ih):
            s = lax.dot_general(kic, qi_ref[:, h * IDX_DIM:(h + 1) * IDX_DIM], _CONTRACT_LAST,
                                preferred_element_type=F32)
            acc = acc + jnp.maximum(s, 0.0) * wi_t[h:h + 1, :]
        key_ref[kc] = _sortable_key(jnp.where(causal_ok(kc), acc, -jnp.inf))
        return 0

    lax.fori_loop(0, i + 1, idx_body, 0)

    def count_ge(cand):
        def body(kc, acc):
            ge = jnp.where(key_ref[kc] >= cand, 1.0, 0.0)
            return acc + jnp.sum(ge.reshape(ck // 8, 8, tq), axis=0)
        acc = lax.fori_loop(0, i + 1, body, jnp.zeros((8, tq), F32))
        return jnp.sum(acc, axis=0, keepdims=True)

    thr = _kth_largest_key(count_ge, (1, tq), topk)

    def bias_body(kc, _):
        sel = jnp.logical_and(key_ref[kc] >= thr, causal_ok(kc))
        bias_ref[kc] = jnp.where(sel, 0.0, NEG).T
        return 0

    lax.fori_loop(0, i + 1, bias_body, 0)

    for n in range(nkv):
        lanes = slice(n * HEAD_DIM, (n + 1) * HEAD_DIM)
        q3 = jnp.concatenate(
            [q_ref[:, (n * GROUP + g) * HEAD_DIM:(n * GROUP + g + 1) * HEAD_DIM] for g in range(GROUP)], axis=0)
        chunk = lambda ref, kc, lanes=lanes: ref[0, pl.ds(pl.multiple_of(kc * ck, ck), ck), lanes]

        def logits_of(kc, q3=q3, chunk=chunk):
            s = lax.dot_general(q3, chunk(k_ref, kc), _CONTRACT_LAST, preferred_element_type=F32)
            return s * SOFTMAX_C2 + jnp.concatenate([bias_ref[kc]] * GROUP, axis=0)

        out = _masked_attention(GROUP * tq, i + 1, logits_of, functools.partial(chunk, v_ref), s_ref)
        for g in range(GROUP):
            o_ref[:, (n * GROUP + g) * HEAD_DIM:(n * GROUP + g + 1) * HEAD_DIM] = (
                out[g * tq:(g + 1) * tq].astype(o_ref.dtype))


def _dsa_prompt(qi, wi, ki, q, k, v, bsz, t, nkv, ih, topk):
    nt = t // QTILE
    qw = nkv * GROUP * HEAD_DIM
    kvw = nkv * HEAD_DIM
    rows = lambda w: pl.BlockSpec((QTILE, w), lambda b, i: (b * nt + i, 0))
    seq = lambda w: pl.BlockSpec((1, t, w), lambda b, i: (b, 0, 0))
    return pl.pallas_call(
        functools.partial(_dsa_prompt_kernel, topk=topk, nkv=nkv, ih=ih),
        grid=(bsz, nt),
        in_specs=[rows(ih * IDX_DIM), rows(128), seq(IDX_DIM), rows(qw), seq(kvw), seq(kvw)],
        out_specs=rows(qw),
        out_shape=jax.ShapeDtypeStruct((bsz * t, qw), BF16),
        scratch_shapes=[pltpu.VMEM((nt, QTILE, QTILE), I32), pltpu.VMEM((nt, QTILE, QTILE), F32),
                        pltpu.VMEM((nt, GROUP * QTILE, QTILE), F32)],
        compiler_params=_cparams("parallel", "arbitrary"),
        name="dsa_prompt_attn",
    )(qi, wi, ki.reshape(bsz, t, IDX_DIM), q, k.reshape(bsz, t, kvw), v.reshape(bsz, t, kvw))


def _topb_select_bias(gate, valid, blk_iota, nblk):
    g = jnp.where(valid, gate, -jnp.inf)
    rank = jnp.zeros(g.shape, I32)
    for m in range(nblk):
        gm = g[:, m:m + 1]
        beats = jnp.logical_or(gm > g, jnp.logical_and(gm == g, blk_iota > m))
        rank = rank + beats.astype(I32)
    sel = jnp.logical_and(valid, rank < MOBA_TOPB)
    return jnp.where(sel, 0.0, NEG)


def _topb_select_rows(gate_t, valid_t, nblk):
    g = jnp.where(valid_t, gate_t, -jnp.inf)
    blk = lax.broadcasted_iota(I32, g.shape, 0)
    rank = jnp.zeros(g.shape, I32)
    for m in range(nblk):
        gm = g[m:m + 1, :]
        beats = jnp.logical_or(gm > g, jnp.logical_and(gm == g, blk > m))
        rank = rank + beats.astype(I32)
    return jnp.logical_and(valid_t, rank < MOBA_TOPB)


def _moba_prompt_kernel(q_ref, k_ref, v_ref, km_ref, o_ref, s_ref, *, nkv, nblk):
    j = pl.program_id(1)
    tq = blk = QTILE
    rows = GROUP * tq
    row = lax.broadcasted_iota(I32, (tq, blk), 0)
    col = lax.broadcasted_iota(I32, (tq, blk), 1)
    cb = jnp.where(col <= row, 0.0, NEG)
    causal_bias = jnp.concatenate([cb] * GROUP, axis=0)
    valid_t = lax.broadcasted_iota(I32, (nblk, rows), 0) < j
    lane = lax.broadcasted_iota(I32, (blk, HEAD_DIM), 1)
    km_pad = jnp.zeros((16 - nblk % 16, HEAD_DIM), F32)
    for n in range(nkv):
        lanes = slice(n * HEAD_DIM, (n + 1) * HEAD_DIM)
        q3 = jnp.concatenate(
            [q_ref[:, (n * GROUP + g) * HEAD_DIM:(n * GROUP + g + 1) * HEAD_DIM] for g in range(GROUP)], axis=0)
        km = jnp.concatenate([km_ref[0, :, lanes], km_pad], axis=0).astype(BF16)
        gate_t = lax.dot_general(km, q3, _CONTRACT_LAST, preferred_element_type=F32)[:nblk]
        selb_t = jnp.where(_topb_select_rows(gate_t, valid_t, nblk), 0.0, NEG)
        selb = jnp.concatenate([selb_t, jnp.zeros((HEAD_DIM - nblk, rows), F32)], axis=0).T
        q_aug = jnp.concatenate([q3, selb.astype(BF16)], axis=1)
        chunk = lambda ref, m, lanes=lanes: ref[0, pl.ds(pl.multiple_of(m * blk, blk), blk), lanes]

        def logits_of(m, q_aug=q_aug, chunk=chunk):
            onehot = jnp.where(lane == m, 1.0, 0.0).astype(BF16)
            k_aug = jnp.concatenate([chunk(k_ref, m), onehot], axis=1)
            return lax.dot_general(q_aug, k_aug, _CONTRACT_LAST, preferred_element_type=F32) * SOFTMAX_C2

        own_s = lax.dot_general(q3, chunk(k_ref, j), _CONTRACT_LAST, preferred_element_type=F32) * SOFTMAX_C2
        out = _masked_attention(rows, j, logits_of, functools.partial(chunk, v_ref), s_ref,
                                own=(j, own_s + causal_bias, chunk(v_ref, j)))
        for g in range(GROUP):
            o_ref[:, (n * GROUP + g) * HEAD_DIM:(n * GROUP + g + 1) * HEAD_DIM] = (
                out[g * tq:(g + 1) * tq].astype(o_ref.dtype))


def _moba_prompt(q, k, v, kmean, bsz, t, nkv):
    nt = t // QTILE
    qw = nkv * GROUP * HEAD_DIM
    kvw = nkv * HEAD_DIM
    rows = lambda w: pl.BlockSpec((QTILE, w), lambda b, i: (b * nt + i, 0))
    seq = lambda w: pl.BlockSpec((1, t, w), lambda b, i: (b, 0, 0))
    return pl.pallas_call(
        functools.partial(_moba_prompt_kernel, nkv=nkv, nblk=nt),
        grid=(bsz, nt),
        in_specs=[rows(qw), seq(kvw), seq(kvw), pl.BlockSpec((1, nt, kvw), lambda b, i: (b, 0, 0))],
        out_specs=rows(qw),
        out_shape=jax.ShapeDtypeStruct((bsz * t, qw), BF16),
        scratch_shapes=[pltpu.VMEM((nt, GROUP * QTILE, QTILE), F32)],
        compiler_params=_cparams("parallel", "arbitrary"),
        name="moba_prompt_attn",
    )(q, k.reshape(bsz, t, kvw), v.reshape(bsz, t, kvw), kmean.reshape(bsz, nt, kvw))


def _mem_attn_kernel(q_ref, mk_ref, mv_ref, o_ref, *, hd):
    scale = hd ** -0.5
    for c in range(MEM_HEADS):
        lanes = slice(c * hd, (c + 1) * hd)
        q = q_ref[0, :, lanes].astype(BF16)
        mk = mk_ref[0, :, lanes].astype(BF16)
        mv = mv_ref[0, :, lanes].astype(BF16)
        s = lax.dot_general(q, mk, _CONTRACT_LAST, preferred_element_type=F32) * scale
        p = jnp.exp(s - jnp.max(s, axis=1, keepdims=True))
        l = jnp.sum(p, axis=1, keepdims=True)
        o = jnp.dot(p.astype(BF16), mv, preferred_element_type=F32) / l
        o_ref[0, :, lanes] = o.astype(o_ref.dtype)


def _mem_attn(qm, mk, mv, out_dtype):
    bsz, t, w = qm.shape
    mlen = mk.shape[1]
    tq = _pick_block(t, 512, 8)
    return pl.pallas_call(
        functools.partial(_mem_attn_kernel, hd=w // MEM_HEADS),
        grid=(bsz, t // tq),
        in_specs=[pl.BlockSpec((1, tq, w), lambda b, i: (b, i, 0)),
                  pl.BlockSpec((1, mlen, w), lambda b, i: (b, 0, 0)),
                  pl.BlockSpec((1, mlen, w), lambda b, i: (b, 0, 0))],
        out_specs=pl.BlockSpec((1, tq, w), lambda b, i: (b, i, 0)),
        out_shape=jax.ShapeDtypeStruct((bsz, t, w), out_dtype),
        compiler_params=_cparams("parallel", "parallel"),
        name="mem_attn",
    )(qm, mk, mv)


SAMPLE_PP = 8


def _page_specs(shape_tail, pp, layer):
    nd = len(shape_tail)
    return [pl.BlockSpec((None, None) + shape_tail,
                         lambda b, p, pt, c=c: (layer, pt[b, p * pp + c]) + (0,) * nd)
            for c in range(pp)]


def _head_sum(w, ih):
    acc = w[0:SAMPLE_TPAD]
    for h in range(1, ih):
        acc = acc + w[h * SAMPLE_TPAD:(h + 1) * SAMPLE_TPAD]
    return acc


def _dsa_sample_index_kernel(pt_ref, qi_ref, wi_ref, *refs, pp, ih, page):
    page_refs, o_ref = refs[:pp], refs[pp]
    qi = qi_ref[0].astype(BF16)
    wi = wi_ref[0]
    for c in range(pp):
        kp = page_refs[c][...].astype(BF16)
        s = lax.dot_general(qi, kp, _CONTRACT_LAST, preferred_element_type=F32)
        o_ref[0, :, c * page:(c + 1) * page] = _head_sum(jnp.maximum(s, 0.0) * wi, ih)


def _dsa_sample_index(page_table, qi_ht, wi_ht, kidx_pool, layer, ih):
    bs, npg = page_table.shape
    page = kidx_pool.shape[2]
    pp = SAMPLE_PP
    r = ih * SAMPLE_TPAD
    grid_spec = pltpu.PrefetchScalarGridSpec(
        num_scalar_prefetch=1,
        grid=(bs, npg // pp),
        in_specs=[pl.BlockSpec((1, r, IDX_DIM), lambda b, p, pt: (b, 0, 0)),
                  pl.BlockSpec((1, r, page), lambda b, p, pt: (b, 0, 0))]
        + _page_specs((page, IDX_DIM), pp, layer),
        out_specs=pl.BlockSpec((1, SAMPLE_TPAD, page * pp), lambda b, p, pt: (b, 0, p)),
    )
    return pl.pallas_call(
        functools.partial(_dsa_sample_index_kernel, pp=pp, ih=ih, page=page),
        grid_spec=grid_spec,
        out_shape=jax.ShapeDtypeStruct((bs, SAMPLE_TPAD, npg * page), F32),
        compiler_params=_cparams("parallel", "arbitrary"),
        name="dsa_sample_index",
    )(page_table, qi_ht, wi_ht, *([kidx_pool] * pp))


def _dsa_sample_select_kernel(sc_ref, qi_ref, wi_ref, kin_ref, o_ref, *, topk, ih, ts, past):
    tail_w = kin_ref.shape[1]
    s = lax.dot_general(qi_ref[0].astype(BF16), kin_ref[0].astype(BF16), _CONTRACT_LAST,
                        preferred_element_type=F32)
    tail = _head_sum(jnp.maximum(s, 0.0) * wi_ref[0][:, :tail_w], ih)
    row = lax.broadcasted_iota(I32, (SAMPLE_TPAD, tail_w), 0)
    col = lax.broadcasted_iota(I32, (SAMPLE_TPAD, tail_w), 1)
    tail_ok = jnp.logical_and(col <= row, col < ts)
    full = jnp.concatenate([sc_ref[0], jnp.where(tail_ok, tail, -jnp.inf)], axis=1)
    keys = _sortable_key(full)
    width = past + tail_w
    col_f = lax.broadcasted_iota(I32, (SAMPLE_TPAD, width), 1)
    row_f = lax.broadcasted_iota(I32, (SAMPLE_TPAD, width), 0)
    visible = jnp.logical_or(col_f < past, jnp.logical_and(col_f - past <= row_f, col_f - past < ts))

    def count_ge(cand):
        return jnp.sum((keys >= cand).astype(F32), axis=1, keepdims=True)

    thr = _kth_largest_key(count_ge, (SAMPLE_TPAD, 1), topk)
    o_ref[0] = jnp.where(jnp.logical_and(keys >= thr, visible), 0.0, NEG)


def _dsa_sample_select(scores, qi_ht, wi_ht, ki_new, ih, ts, topk):
    bs, _, past = scores.shape
    tail_w = ki_new.shape[1]
    r = ih * SAMPLE_TPAD
    return pl.pallas_call(
        functools.partial(_dsa_sample_select_kernel, topk=topk, ih=ih, ts=ts, past=past),
        grid=(bs,),
        in_specs=[pl.BlockSpec((1, SAMPLE_TPAD, past), lambda b: (b, 0, 0)),
                  pl.BlockSpec((1, r, IDX_DIM), lambda b: (b, 0, 0)),
                  pl.BlockSpec((1, r, wi_ht.shape[2]), lambda b: (b, 0, 0)),
                  pl.BlockSpec((1, tail_w, IDX_DIM), lambda b: (b, 0, 0))],
        out_specs=pl.BlockSpec((1, SAMPLE_TPAD, past + tail_w), lambda b: (b, 0, 0)),
        out_shape=jax.ShapeDtypeStruct((bs, SAMPLE_TPAD, past + tail_w), F32),
        compiler_params=_cparams("parallel"),
        name="dsa_sample_select",
    )(scores, qi_ht, wi_ht, ki_new)


def _sample_flash_update(qbd, kblk, vblk, bias, m_ref, l_ref, acc_ref, nkv):
    rows_per_head = GROUP * SAMPLE_TPAD
    s = lax.dot_general(qbd, kblk, _CONTRACT_LAST, preferred_element_type=F32) * (HEAD_DIM ** -0.5) + bias
    m_old = m_ref[...]
    m_new = jnp.maximum(m_old, jnp.max(s, axis=1, keepdims=True))
    alpha = jnp.exp(m_old - m_new)
    p = jnp.exp(s - m_new)
    l_ref[...] = alpha * l_ref[...] + jnp.sum(p, axis=1, keepdims=True)
    m_ref[...] = m_new
    o_full = jnp.dot(p.astype(BF16), vblk, preferred_element_type=F32)
    o_diag = jnp.concatenate(
        [o_full[n * rows_per_head:(n + 1) * rows_per_head, n * HEAD_DIM:(n + 1) * HEAD_DIM] for n in range(nkv)],
        axis=0)
    acc_ref[...] = alpha * acc_ref[...] + o_diag


def _sample_flash_reset(m_ref, l_ref, acc_ref):
    m_ref[...] = jnp.full(m_ref.shape, NEG, F32)
    l_ref[...] = jnp.zeros(l_ref.shape, F32)
    acc_ref[...] = jnp.zeros(acc_ref.shape, F32)


def _heads_on_lanes(refs, nkv):
    keys = refs[0].shape[0] // nkv
    pages = [jnp.concatenate([r[pl.ds(n, keys, stride=nkv), :] for n in range(nkv)], axis=1).astype(BF16)
             for r in refs]
    return jnp.concatenate(pages, axis=0)


def _flat_pool(pool):
    return pool.reshape(pool.shape[0], pool.shape[1], pool.shape[2] * pool.shape[3], pool.shape[4])


def _tile_query_bias(b8, reps):
    return jnp.concatenate([b8] * reps, axis=0)


def _dsa_sample_attn_kernel(pt_ref, q_ref, bias_ref, tbias_ref, kn_ref, vn_ref, *refs, pp, nkv):
    k_refs, v_refs = refs[:pp], refs[pp:2 * pp]
    o_ref, m_ref, l_ref, acc_ref = refs[2 * pp:]
    p = pl.program_id(1)
    reps = nkv * GROUP

    @pl.when(p == 0)
    def _():
        _sample_flash_reset(m_ref, l_ref, acc_ref)

    _sample_flash_update(q_ref[0], _heads_on_lanes(k_refs, nkv), _heads_on_lanes(v_refs, nkv),
                         _tile_query_bias(bias_ref[0], reps), m_ref, l_ref, acc_ref, nkv)

    @pl.when(p == pl.num_programs(1) - 1)
    def _():
        _sample_flash_update(q_ref[0], kn_ref[0].astype(BF16), vn_ref[0].astype(BF16),
                             _tile_query_bias(tbias_ref[0], reps), m_ref, l_ref, acc_ref, nkv)
        o_ref[0] = acc_ref[...] / l_ref[...]


def _sample_attn_specs(nkv, tail_rows):
    rows = nkv * GROUP * SAMPLE_TPAD
    kvw = nkv * HEAD_DIM
    q_spec = pl.BlockSpec((1, rows, kvw), lambda b, p, pt: (b, 0, 0))
    tail_spec = pl.BlockSpec((1, tail_rows, kvw), lambda b, p, pt: (b, 0, 0))
    out_spec = pl.BlockSpec((1, rows, HEAD_DIM), lambda b, p, pt: (b, 0, 0))
    scratch = [pltpu.VMEM((rows, 1), F32), pltpu.VMEM((rows, 1), F32), pltpu.VMEM((rows, HEAD_DIM), F32)]
    return rows, q_spec, tail_spec, out_spec, scratch


def _dsa_sample_attn(page_table, qbd, bias, k_new, v_new, k_pool, v_pool, layer, nkv):
    bs, npg = page_table.shape
    page = k_pool.shape[2]
    pp = SAMPLE_PP
    rows, q_spec, tail_spec, out_spec, scratch = _sample_attn_specs(nkv, k_new.shape[1])
    grid_spec = pltpu.PrefetchScalarGridSpec(
        num_scalar_prefetch=1,
        grid=(bs, npg // pp),
        in_specs=[q_spec,
                  pl.BlockSpec((1, SAMPLE_TPAD, page * pp), lambda b, p, pt: (b, 0, p)),
                  pl.BlockSpec((1, SAMPLE_TPAD, page), lambda b, p, pt: (b, 0, npg)),
                  tail_spec, tail_spec]
        + _page_specs((page * nkv, HEAD_DIM), pp, layer) + _page_specs((page * nkv, HEAD_DIM), pp, layer),
        out_specs=out_spec,
        scratch_shapes=scratch,
    )
    return pl.pallas_call(
        functools.partial(_dsa_sample_attn_kernel, pp=pp, nkv=nkv),
        grid_spec=grid_spec,
        out_shape=jax.ShapeDtypeStruct((bs, rows, HEAD_DIM), F32),
        compiler_params=_cparams("parallel", "arbitrary"),
        name="dsa_sample_attn",
    )(page_table, qbd, bias, bias, k_new, v_new, *([_flat_pool(k_pool)] * pp), *([_flat_pool(v_pool)] * pp))


KMEAN_BLOCKS = 8


def _moba_kmean_kernel(pt_ref, *refs, nblk, ppb):
    k_refs, o_ref = refs[:nblk * ppb], refs[nblk * ppb]
    for blk in range(nblk):
        acc = jnp.sum(k_refs[blk * ppb][...], axis=0)
        for c in range(1, ppb):
            acc = acc + jnp.sum(k_refs[blk * ppb + c][...], axis=0)
        o_ref[0, blk] = acc * (1.0 / MOBA_BLOCK)


def _moba_sample_kmean(page_table, k_pool, layer, nkv):
    bs, npg = page_table.shape
    page = k_pool.shape[2]
    ppb = MOBA_BLOCK // page
    nb = npg // ppb
    nblk = KMEAN_BLOCKS
    grid_spec = pltpu.PrefetchScalarGridSpec(
        num_scalar_prefetch=1,
        grid=(bs, nb // nblk),
        in_specs=_page_specs((page, nkv, HEAD_DIM), nblk * ppb, layer),
        out_specs=pl.BlockSpec((1, nblk, nkv, HEAD_DIM), lambda b, p, pt: (b, p, 0, 0)),
    )
    return pl.pallas_call(
        functools.partial(_moba_kmean_kernel, nblk=nblk, ppb=ppb),
        grid_spec=grid_spec,
        out_shape=jax.ShapeDtypeStruct((bs, nb, nkv, HEAD_DIM), F32),
        compiler_params=_cparams("parallel", "arbitrary"),
        name="moba_sample_kmean",
    )(page_table, *([k_pool] * (nblk * ppb)))


def _moba_sample_attn_kernel(pt_ref, q_ref, km_ref, kn_ref, vn_ref, *refs, pp, nkv, page, nb, ts):
    k_refs, v_refs = refs[:pp], refs[pp:2 * pp]
    o_ref, m_ref, l_ref, acc_ref, sb_ref = refs[2 * pp:]
    p = pl.program_id(1)
    rows = nkv * GROUP * SAMPLE_TPAD
    ppb = MOBA_BLOCK // page
    blk_iota = lax.broadcasted_iota(I32, (rows, nb), 1)

    @pl.when(p == 0)
    def _():
        _sample_flash_reset(m_ref, l_ref, acc_ref)
        valid = blk_iota >= 0
        gate = lax.dot_general(q_ref[0], km_ref[0].astype(BF16), _CONTRACT_LAST, preferred_element_type=F32)
        sb_ref[...] = _topb_select_bias(gate, valid, blk_iota, nb)

    selb = sb_ref[...]
    cols = []
    for c in range(pp // ppb):
        blk = p * (pp // ppb) + c
        col = jnp.sum(jnp.where(blk_iota == blk, selb, 0.0), axis=1, keepdims=True)
        cols.append(jnp.broadcast_to(col, (rows, MOBA_BLOCK)))
    _sample_flash_update(q_ref[0], _heads_on_lanes(k_refs, nkv), _heads_on_lanes(v_refs, nkv),
                         jnp.concatenate(cols, axis=1), m_ref, l_ref, acc_ref, nkv)

    @pl.when(p == pl.num_programs(1) - 1)
    def _():
        tail_w = kn_ref.shape[1]
        row = lax.broadcasted_iota(I32, (rows, tail_w), 0) & (SAMPLE_TPAD - 1)
        col = lax.broadcasted_iota(I32, (rows, tail_w), 1)
        tb = jnp.where(jnp.logical_and(col <= row, col < ts), 0.0, NEG)
        _sample_flash_update(q_ref[0], kn_ref[0].astype(BF16), vn_ref[0].astype(BF16), tb,
                             m_ref, l_ref, acc_ref, nkv)
        o_ref[0] = acc_ref[...] / l_ref[...]


def _moba_sample_attn(page_table, qbd, kmean, k_new, v_new, k_pool, v_pool, layer, nkv, ts):
    bs, npg = page_table.shape
    page = k_pool.shape[2]
    kvw = nkv * HEAD_DIM
    pp = SAMPLE_PP
    nb = kmean.shape[1]
    rows, q_spec, tail_spec, out_spec, scratch = _sample_attn_specs(nkv, k_new.shape[1])
    grid_spec = pltpu.PrefetchScalarGridSpec(
        num_scalar_prefetch=1,
        grid=(bs, npg // pp),
        in_specs=[q_spec, pl.BlockSpec((1, nb, kvw), lambda b, p, pt: (b, 0, 0)), tail_spec, tail_spec]
        + _page_specs((page * nkv, HEAD_DIM), pp, layer) + _page_specs((page * nkv, HEAD_DIM), pp, layer),
        out_specs=out_spec,
        scratch_shapes=scratch + [pltpu.VMEM((rows, nb), F32)],
    )
    return pl.pallas_call(
        functools.partial(_moba_sample_attn_kernel, pp=pp, nkv=nkv, page=page, nb=nb, ts=ts),
        grid_spec=grid_spec,
        out_shape=jax.ShapeDtypeStruct((bs, rows, HEAD_DIM), F32),
        compiler_params=_cparams("parallel", "arbitrary"),
        name="moba_sample_attn",
    )(page_table, qbd, kmean, k_new, v_new, *([_flat_pool(k_pool)] * pp), *([_flat_pool(v_pool)] * pp))


def _heads_to_kv_major(q, bs, nkv):
    x = q.reshape(bs, SAMPLE_TPAD, nkv, GROUP, HEAD_DIM).transpose(0, 2, 3, 1, 4)
    eye = jnp.eye(nkv, dtype=q.dtype)
    bd = x[:, :, :, :, None, :] * eye[None, :, None, None, :, None]
    return bd.reshape(bs, nkv * GROUP * SAMPLE_TPAD, nkv * HEAD_DIM).astype(BF16)


def _kv_major_to_rows(o, bs, nkv):
    x = o.reshape(bs, nkv, GROUP, SAMPLE_TPAD, HEAD_DIM)
    return x.transpose(0, 3, 1, 2, 4).reshape(bs * SAMPLE_TPAD, nkv * GROUP * HEAD_DIM)


def _pad_rows(x, n):
    return jnp.pad(x, ((0, 0), (0, n - x.shape[1]), (0, 0)))


def _merge_mixers(h, mix, qm, mk, mv, bsz):
    m = h.shape[0]
    ma = _mem_attn(qm.reshape(bsz, m // bsz, -1), mk, mv, qm.dtype).reshape(m, -1)
    return jnp.concatenate([mix.astype(BF16), ma.astype(BF16)], axis=1)


def _finish_layer(hp, hs, merged_p, merged_s, layer, w_o, g_ffn, w_up, w_down):
    hp = _matmul(merged_p, w_o, layer, F32, epilogue="residual", residual=hp)
    hs = _matmul(merged_s, w_o, layer, F32, epilogue="residual", residual=hs)
    up, us = _matmul(_rmsnorm(hp, g_ffn, BF16), w_up, layer, BF16, epilogue="relu2", a2=_rmsnorm(hs, g_ffn, BF16))
    hp = _matmul(up, w_down, layer, F32, epilogue="residual", residual=hp)
    hs = _matmul(us, w_down, layer, F32, epilogue="residual", residual=hs)
    return hp, hs


def kernel(x_prompt, x_sample, cache_dsa_k, cache_dsa_v, cache_dsa_kidx, cache_moba_k, cache_moba_v, cache_mem_k, cache_mem_v, page_table, mem_prompt, norm_mix, norm_mem, w_in_dsa, w_in_moba, w_mem_kv, w_out, norm_ffn, w_up, w_down, norm_final):
    bsz, t, d = x_prompt.shape
    bs, ts, _ = x_sample.shape
    depth = norm_mix.shape[0]
    nh = (3 * d) // (4 * HEAD_DIM)
    nkv = nh // GROUP
    qw, kvw = nh * HEAD_DIM, nkv * HEAD_DIM
    memw = d // 4
    mhd = memw // MEM_HEADS
    ih = d // 128
    npg = page_table.shape[1]
    page = cache_dsa_k.shape[2]
    past = npg * page
    mlen = mem_prompt.shape[1]
    tp = SAMPLE_TPAD

    hp = x_prompt.reshape(bsz * t, d)
    hs = jnp.pad(x_sample, ((0, 0), (0, tp - ts), (0, 0))).reshape(bs * tp, d)
    pos_p = jnp.arange(t, dtype=I32)
    pos_s = past + (jnp.arange(tp, dtype=I32) % ts)
    tabs_p = _rope_tables(pos_p)
    tabs_s = _rope_tables(jnp.tile(pos_s, bs))

    outs = {name: [] for name in ("pdk", "pdv", "pdki", "sdk", "sdv", "sdki", "pmk", "pmv", "smk", "smv", "mk", "mv")}
    mem_flat = mem_prompt.reshape(bsz * mlen, d)

    w_o, wu, wd, w_mem = w_out.astype(BF16), w_up.astype(BF16), w_down.astype(BF16), w_mem_kv.astype(BF16)
    w_dsa, w_moba = w_in_dsa.astype(BF16), w_in_moba.astype(BF16)
    c0 = qw + 2 * kvw + ih * IDX_DIM
    w_dsa_tail = jnp.concatenate(
        [w_in_dsa[:, :, c0 + ih:c0 + ih + IDX_DIM], w_in_dsa[:, :, c0 + ih + IDX_DIM:], w_in_dsa[:, :, c0:c0 + ih],
         jnp.zeros(w_in_dsa.shape[:2] + (128 - ih,), F32)], axis=2).astype(BF16)

    for i in range(depth):
        j = i // 2
        mkv = _matmul(_rmsnorm(mem_flat, norm_mem[i], BF16), w_mem, i, F32)
        mk_p = mkv[:, :memw].reshape(bsz, mlen, memw)
        mv_p = mkv[:, memw:].reshape(bsz, mlen, memw)
        outs["mk"].append(mk_p.reshape(bsz, mlen, MEM_HEADS, mhd))
        outs["mv"].append(mv_p.reshape(bsz, mlen, MEM_HEADS, mhd))
        hn_p = _rmsnorm(hp, norm_mix[i], BF16)
        hn_s = _rmsnorm(hs, norm_mix[i], BF16)
        mk_s = cache_mem_k[i].reshape(bs, mlen, memw)
        mv_s = cache_mem_v[i].reshape(bs, mlen, memw)
        if i % 2 == 0:
            pa, pa_s = _matmul(hn_p, w_dsa, j, F32, n=c0, a2=hn_s)
            pb, pb_s = _matmul(hn_p, w_dsa_tail, j, F32, a2=hn_s)
            q, kf, kb, vf, vb, qi, kif, kib, wi, qm = _dsa_split(pa, pb, tabs_p, t, nh, nkv, ih, memw, BF16)
            outs["pdk"].append(kf.reshape(bsz, t, nkv, HEAD_DIM))
            outs["pdv"].append(vf.reshape(bsz, t, nkv, HEAD_DIM))
            outs["pdki"].append(kif.reshape(bsz, t, IDX_DIM))
            mix_p = _dsa_prompt(qi, wi, kib, q, kb, vb, bsz, t, nkv, ih, min(DSA_TOPK, t // 4))
            q_s, kf, _, vf, _, qi_s, kif, _, wi_s, qm_s = _dsa_split(pa_s, pb_s, tabs_s, bs * tp, nh, nkv, ih, memw, F32)
            outs["sdk"].append(kf.reshape(bs, tp, nkv, HEAD_DIM)[:, :ts])
            outs["sdv"].append(vf.reshape(bs, tp, nkv, HEAD_DIM)[:, :ts])
            outs["sdki"].append(kif.reshape(bs, tp, IDX_DIM)[:, :ts])
            qi_ht = qi_s.reshape(bs, tp, ih, IDX_DIM).transpose(0, 2, 1, 3).reshape(bs, ih * tp, IDX_DIM)
            wi_ht = wi_s.reshape(bs, tp, 128)[:, :, :ih].transpose(0, 2, 1).reshape(bs, ih * tp, 1)
            wi_ht = jnp.broadcast_to(wi_ht, (bs, ih * tp, page))
            scores = _dsa_sample_index(page_table, qi_ht, wi_ht, cache_dsa_kidx, j, ih)
            ki_new = _pad_rows(kif.reshape(bs, tp, IDX_DIM), page)
            bias = _dsa_sample_select(scores, qi_ht, wi_ht, ki_new, ih, ts, min(DSA_TOPK, (past + ts) // 4))
            o_s = _dsa_sample_attn(page_table, _heads_to_kv_major(q_s, bs, nkv), bias,
                                   _pad_rows(kf.reshape(bs, tp, kvw), page), _pad_rows(vf.reshape(bs, tp, kvw), page),
                                   cache_dsa_k, cache_dsa_v, j, nkv)
            mix_s = _kv_major_to_rows(o_s, bs, nkv)
        else:
            p, p_s = _matmul(hn_p, w_moba, j, F32, a2=hn_s)
            q, kf, kb, vf, vb, qm, kmean = _moba_split(p, tabs_p, t, nh, nkv, memw, BF16)
            outs["pmk"].append(kf.reshape(bsz, t, nkv, HEAD_DIM))
            outs["pmv"].append(vf.reshape(bsz, t, nkv, HEAD_DIM))
            mix_p = _moba_prompt(q, kb, vb, kmean, bsz, t, nkv)
            q_s, kf, _, vf, _, qm_s, _ = _moba_split(p_s, tabs_s, bs * tp, nh, nkv, memw, F32)
            outs["smk"].append(kf.reshape(bs, tp, nkv, HEAD_DIM)[:, :ts])
            outs["smv"].append(vf.reshape(bs, tp, nkv, HEAD_DIM)[:, :ts])
            kmean_s = _moba_sample_kmean(page_table, cache_moba_k, j, nkv)
            kmean_s = kmean_s.reshape(bs, kmean_s.shape[1], kvw)
            o_s = _moba_sample_attn(page_table, _heads_to_kv_major(q_s, bs, nkv), kmean_s,
                                    _pad_rows(kf.reshape(bs, tp, kvw), page), _pad_rows(vf.reshape(bs, tp, kvw), page),
                                    cache_moba_k, cache_moba_v, j, nkv, ts)
            mix_s = _kv_major_to_rows(o_s, bs, nkv)
        hp, hs = _finish_layer(hp, hs, _merge_mixers(hp, mix_p, qm, mk_p, mv_p, bsz),
                               _merge_mixers(hs, mix_s, qm_s, mk_s, mv_s, bs), i, w_o, norm_ffn[i], wu, wd)

    y_prompt = _rmsnorm(hp, norm_final, F32).reshape(bsz, t, d)
    y_sample = _rmsnorm(hs, norm_final, F32).reshape(bs, tp, d)[:, :ts]
    st = jnp.stack
    return (y_prompt, y_sample, st(outs["pdk"]), st(outs["pdv"]), st(outs["pdki"]), st(outs["pmk"]), st(outs["pmv"]),
            st(outs["mk"]), st(outs["mv"]), st(outs["sdk"]), st(outs["sdv"]), st(outs["sdki"]),
            st(outs["smk"]), st(outs["smv"]))
```

```python
import functools

import jax
import jax.numpy as jnp
from jax import lax
from jax.experimental import pallas as pl
from jax.experimental.pallas import tpu as pltpu

F32 = jnp.float32
BF16 = jnp.bfloat16
I32 = jnp.int32

HEAD_DIM = 128
GROUP = 3
ROT_DIM = 32
ROPE_THETA = 500000.0
IDX_DIM = 128
MEM_HEADS = 4
DSA_TOPK = 256
MOBA_BLOCK = 256
MOBA_TOPB = 3
EPS = 1e-6
NEG = -1e30
LOG2E = 1.4426950408889634
SOFTMAX_C2 = (HEAD_DIM ** -0.5) * LOG2E
QTILE = 256
SAMPLE_TPAD = 8
VMEM_LIMIT = 60 * 1024 * 1024

_CONTRACT_LAST = (((1,), (1,)), ((), ()))


def _cparams(*sem):
    return pltpu.CompilerParams(dimension_semantics=sem, vmem_limit_bytes=VMEM_LIMIT)


def _pick_block(n, pref, align=128):
    if n <= pref:
        return n
    b = (pref // align) * align
    while b >= align:
        if n % b == 0:
            return b
        b -= align
    return n


def _rmsnorm_kernel(x_ref, g_ref, o_ref):
    x = x_ref[...]
    ms = jnp.mean(x * x, axis=-1, keepdims=True)
    o_ref[...] = ((x * lax.rsqrt(ms + EPS)) * g_ref[...]).astype(o_ref.dtype)


def _rmsnorm(x, g, out_dtype):
    m, d = x.shape
    bm = _pick_block(m, 256, 8)
    return pl.pallas_call(
        _rmsnorm_kernel,
        grid=(m // bm,),
        in_specs=[pl.BlockSpec((bm, d), lambda i: (i, 0)), pl.BlockSpec((1, d), lambda i: (0, 0))],
        out_specs=pl.BlockSpec((bm, d), lambda i: (i, 0)),
        out_shape=jax.ShapeDtypeStruct((m, d), out_dtype),
        compiler_params=_cparams("parallel"),
        name="rmsnorm",
    )(x, g.reshape(1, d).astype(F32))


def _mm_tile(a_ref, w, r_ref, o_ref, k, nk, epilogue):
    part = jnp.dot(a_ref[...], w, preferred_element_type=F32)

    def finish(acc):
        if epilogue == "relu2":
            r = jnp.maximum(acc, 0.0)
            acc = r * r
        elif epilogue == "residual":
            acc = r_ref[...] + acc
        o_ref[...] = acc.astype(o_ref.dtype)

    if nk == 1:
        finish(part)
        return

    @pl.when(k == 0)
    def _():
        o_ref[...] = part

    @pl.when(jnp.logical_and(k > 0, k < nk - 1))
    def _():
        o_ref[...] += part

    @pl.when(k == nk - 1)
    def _():
        finish(o_ref[...] + part)


def _mm_kernel(*refs, nk, epilogue, dual):
    refs = list(refs)
    a_ref, w_ref = refs.pop(0), refs.pop(0)
    r_ref = refs.pop(0) if epilogue == "residual" else None
    a2_ref = refs.pop(0) if dual else None
    r2_ref = refs.pop(0) if dual and epilogue == "residual" else None
    o_ref = refs.pop(0)
    o2_ref = refs.pop(0) if dual else None
    k = pl.program_id(2)
    w = w_ref[...]
    _mm_tile(a_ref, w, r_ref, o_ref, k, nk, epilogue)
    if dual:
        @pl.when(pl.program_id(1) == 0)
        def _():
            _mm_tile(a2_ref, w, r2_ref, o2_ref, k, nk, epilogue)


def _matmul(a, w, layer, out_dtype, epilogue=None, residual=None, n=None, a2=None, residual2=None,
            bm_pref=1024, bn_pref=1024, bk_pref=4096):
    m, kdim = a.shape
    n = w.shape[2] if n is None else n
    bm = _pick_block(m, bm_pref, 8)
    bn = _pick_block(n, bn_pref)
    bk = _pick_block(kdim, bk_pref)
    nk = kdim // bk
    dual = a2 is not None
    assert nk == 1 or out_dtype == F32, "a split contraction accumulates in the f32 output block"
    in_specs = [pl.BlockSpec((bm, bk), lambda j, i, k: (i, k)),
                pl.BlockSpec((None, bk, bn), lambda j, i, k: (layer, k, j))]
    args = [a, w]
    out_specs = [pl.BlockSpec((bm, bn), lambda j, i, k: (i, j))]
    out_shape = [jax.ShapeDtypeStruct((m, n), out_dtype)]
    if epilogue == "residual":
        in_specs.append(pl.BlockSpec((bm, bn), lambda j, i, k: (i, j)))
        args.append(residual)
    if dual:
        m2 = a2.shape[0]
        in_specs.append(pl.BlockSpec((m2, bk), lambda j, i, k: (0, k)))
        args.append(a2)
        if epilogue == "residual":
            in_specs.append(pl.BlockSpec((m2, bn), lambda j, i, k: (0, j)))
            args.append(residual2)
        out_specs.append(pl.BlockSpec((m2, bn), lambda j, i, k: (0, j)))
        out_shape.append(jax.ShapeDtypeStruct((m2, n), out_dtype))
    out = pl.pallas_call(
        functools.partial(_mm_kernel, nk=nk, epilogue=epilogue, dual=dual),
        grid=(n // bn, m // bm, nk),
        in_specs=in_specs,
        out_specs=out_specs,
        out_shape=out_shape,
        compiler_params=_cparams("parallel", "arbitrary" if dual else "parallel", "arbitrary"),
        name="matmul",
    )(*args)
    return tuple(out) if dual else out[0]


def _rope_tables(pos):
    half = ROT_DIM // 2
    inv = ROPE_THETA ** (-jnp.arange(half, dtype=F32) / half)
    ang = pos.astype(F32)[:, None] * inv[None, :]
    cos, sin = jnp.cos(ang), jnp.sin(ang)
    r = pos.shape[0]
    z16 = jnp.zeros((r, half), F32)
    zrest = jnp.zeros((r, HEAD_DIM - ROT_DIM), F32)
    c = jnp.concatenate([cos, cos, jnp.ones((r, HEAD_DIM - ROT_DIM), F32)], axis=1)
    sm = jnp.concatenate([-sin, z16, zrest], axis=1)
    sp = jnp.concatenate([z16, sin, zrest], axis=1)
    return c, sm, sp


def _rope_head(x, c, sm, sp):
    half = ROT_DIM // 2
    return x * c + pltpu.roll(x, HEAD_DIM - half, 1) * sm + pltpu.roll(x, half, 1) * sp


def _dsa_split_kernel(pa_ref, pb_ref, c_ref, sm_ref, sp_ref,
                      q_ref, kf_ref, kb_ref, vf_ref, vb_ref, qi_ref, kif_ref, kib_ref, wi_ref, qm_ref,
                      *, nh, nkv, ih, memw):
    c, sm, sp = c_ref[...], sm_ref[...], sp_ref[...]
    qw, kvw = nh * HEAD_DIM, nkv * HEAD_DIM
    for h in range(nh):
        sl = slice(h * HEAD_DIM, (h + 1) * HEAD_DIM)
        q_ref[:, sl] = _rope_head(pa_ref[:, sl], c, sm, sp).astype(q_ref.dtype)
    for h in range(nkv):
        sl = slice(h * HEAD_DIM, (h + 1) * HEAD_DIM)
        kr = _rope_head(pa_ref[:, qw + h * HEAD_DIM: qw + (h + 1) * HEAD_DIM], c, sm, sp)
        kf_ref[:, sl] = kr
        kb_ref[:, sl] = kr.astype(kb_ref.dtype)
    v = pa_ref[:, qw + kvw: qw + 2 * kvw]
    vf_ref[...] = v
    vb_ref[...] = v.astype(vb_ref.dtype)
    base = qw + 2 * kvw
    for h in range(ih):
        sl = slice(h * IDX_DIM, (h + 1) * IDX_DIM)
        qi_ref[:, sl] = _rope_head(pa_ref[:, base + h * IDX_DIM: base + (h + 1) * IDX_DIM], c, sm, sp).astype(qi_ref.dtype)
    ki = _rope_head(pb_ref[:, 0:IDX_DIM], c, sm, sp)
    kif_ref[...] = ki
    kib_ref[...] = ki.astype(kib_ref.dtype)
    qm_ref[...] = pb_ref[:, IDX_DIM:IDX_DIM + memw].astype(qm_ref.dtype)
    wi_ref[...] = pb_ref[:, IDX_DIM + memw:IDX_DIM + memw + 128] * ((ih * IDX_DIM) ** -0.5)


def _dsa_split(pa, pb, tabs, rows_per_seq, nh, nkv, ih, memw, act_dtype):
    m = pa.shape[0]
    bm = _pick_block(rows_per_seq, 128, 8)
    nt = rows_per_seq // bm
    qw, kvw = nh * HEAD_DIM, nkv * HEAD_DIM
    row = lambda w: pl.BlockSpec((bm, w), lambda i: (i, 0))
    tab = pl.BlockSpec((bm, HEAD_DIM), lambda i: (i % nt, 0))
    shapes = [(qw, act_dtype), (kvw, F32), (kvw, act_dtype), (kvw, F32), (kvw, act_dtype),
              (ih * IDX_DIM, act_dtype), (IDX_DIM, F32), (IDX_DIM, act_dtype), (128, F32), (memw, act_dtype)]
    return pl.pallas_call(
        functools.partial(_dsa_split_kernel, nh=nh, nkv=nkv, ih=ih, memw=memw),
        grid=(m // bm,),
        in_specs=[row(pa.shape[1]), row(pb.shape[1]), tab, tab, tab],
        out_specs=[row(w) for w, _ in shapes],
        out_shape=[jax.ShapeDtypeStruct((m, w), dt) for w, dt in shapes],
        compiler_params=_cparams("parallel"),
        name="dsa_split",
    )(pa, pb, *tabs)


def _moba_split_kernel(p_ref, c_ref, sm_ref, sp_ref, q_ref, kf_ref, kb_ref, vf_ref, vb_ref, qm_ref, km_ref,
                       *, nh, nkv, memw):
    c, sm, sp = c_ref[...], sm_ref[...], sp_ref[...]
    qw, kvw = nh * HEAD_DIM, nkv * HEAD_DIM
    for h in range(nh):
        sl = slice(h * HEAD_DIM, (h + 1) * HEAD_DIM)
        q_ref[:, sl] = _rope_head(p_ref[:, sl], c, sm, sp).astype(q_ref.dtype)
    rows = p_ref.shape[0]
    for h in range(nkv):
        sl = slice(h * HEAD_DIM, (h + 1) * HEAD_DIM)
        kr = _rope_head(p_ref[:, qw + h * HEAD_DIM: qw + (h + 1) * HEAD_DIM], c, sm, sp)
        kf_ref[:, sl] = kr
        kb_ref[:, sl] = kr.astype(kb_ref.dtype)
        km_ref[0, :, sl] = jnp.sum(kr, axis=0, keepdims=True) * (1.0 / rows)
    v = p_ref[:, qw + kvw: qw + 2 * kvw]
    vf_ref[...] = v
    vb_ref[...] = v.astype(vb_ref.dtype)
    qm_ref[...] = p_ref[:, qw + 2 * kvw: qw + 2 * kvw + memw].astype(qm_ref.dtype)


def _moba_split(p, tabs, rows_per_seq, nh, nkv, memw, act_dtype):
    m = p.shape[0]
    bm = _pick_block(rows_per_seq, MOBA_BLOCK, 8)
    nt = rows_per_seq // bm
    qw, kvw = nh * HEAD_DIM, nkv * HEAD_DIM
    row = lambda w: pl.BlockSpec((bm, w), lambda i: (i, 0))
    tab = pl.BlockSpec((bm, HEAD_DIM), lambda i: (i % nt, 0))
    shapes = [(qw, act_dtype), (kvw, F32), (kvw, act_dtype), (kvw, F32), (kvw, act_dtype), (memw, act_dtype)]
    return pl.pallas_call(
        functools.partial(_moba_split_kernel, nh=nh, nkv=nkv, memw=memw),
        grid=(m // bm,),
        in_specs=[row(p.shape[1]), tab, tab, tab],
        out_specs=[row(w) for w, _ in shapes] + [pl.BlockSpec((1, 1, kvw), lambda i: (i, 0, 0))],
        out_shape=[jax.ShapeDtypeStruct((m, w), dt) for w, dt in shapes]
        + [jax.ShapeDtypeStruct((m // bm, 1, kvw), F32)],
        compiler_params=_cparams("parallel"),
        name="moba_split",
    )(p, *tabs)


def _sortable_key(x):
    bits = pltpu.bitcast(x, I32)
    return jnp.where(bits < 0, bits ^ jnp.int32(0x7FFFFFFF), bits)


def _kth_largest_key(count_ge, shape, k):
    imin = jnp.int32(-2 ** 31)
    c0 = count_ge(jnp.zeros(shape, I32))
    thr = jnp.where(c0 >= k, jnp.int32(0), imin)

    def bit_body(it, thr):
        cand = thr + jnp.left_shift(jnp.int32(1), jnp.int32(30) - it)
        return jnp.where(count_ge(cand) >= k, cand, thr)

    return lax.fori_loop(0, 31, bit_body, thr)


def _lane_fold(x, acc, op):
    for w in range(x.shape[1] // 128):
        acc = op(acc, x[:, w * 128:(w + 1) * 128])
    return acc


def _chunk_loop(n, body):
    def pair(c2, _):
        body(2 * c2)
        body(2 * c2 + 1)
        return 0

    def single(c, _):
        body(c)
        return 0

    npair = n // 2
    lax.fori_loop(0, npair, pair, 0)
    lax.fori_loop(2 * npair, n, single, 0)


def _masked_attention(nheads, n_dyn, chunk_logits, load_v, mx_ref, lsum_ref, acc_ref, own=None):
    rows = mx_ref.shape[1]
    for n in range(nheads):
        mx = jnp.full((rows, 128), NEG, F32)
        if own is not None:
            mx = _lane_fold(own[0][n], mx, jnp.maximum)
        mx_ref[n] = mx

    def pass_a(c):
        of_head = chunk_logits(c)
        for n in range(nheads):
            mx_ref[n] = _lane_fold(of_head(n), mx_ref[n], jnp.maximum)

    _chunk_loop(n_dyn, pass_a)
    for n in range(nheads):
        mx_ref[n] = jnp.broadcast_to(jnp.max(mx_ref[n], axis=1, keepdims=True), (rows, 128))

    def accumulate(n, s, vb, first=False):
        mb = mx_ref[n]
        p = jnp.exp2(s - jnp.concatenate([mb] * (s.shape[1] // 128), axis=1))
        pv = jnp.dot(p.astype(BF16), vb, preferred_element_type=F32)
        if first:
            lsum_ref[n] = _lane_fold(p, jnp.zeros((rows, 128), F32), jnp.add)
            acc_ref[n] = pv
        else:
            lsum_ref[n] = _lane_fold(p, lsum_ref[n], jnp.add)
            acc_ref[n] = acc_ref[n] + pv

    for n in range(nheads):
        if own is not None:
            accumulate(n, own[0][n], own[1](n), first=True)
        else:
            lsum_ref[n] = jnp.zeros((rows, 128), F32)
            acc_ref[n] = jnp.zeros((rows, HEAD_DIM), F32)

    def pass_b(c):
        of_head = chunk_logits(c)
        for n in range(nheads):
            accumulate(n, of_head(n), load_v(c, n))

    _chunk_loop(n_dyn, pass_b)


def _write_heads(o_ref, lsum_ref, acc_ref, nkv, tq):
    for n in range(nkv):
        out = acc_ref[n] / jnp.sum(lsum_ref[n], axis=1, keepdims=True)
        for g in range(GROUP):
            o_ref[:, (n * GROUP + g) * HEAD_DIM:(n * GROUP + g + 1) * HEAD_DIM] = (
                out[g * tq:(g + 1) * tq].astype(o_ref.dtype))


def _attn_state_scratch(nkv):
    return [pltpu.VMEM((nkv, GROUP * QTILE, 128), F32)] * 3


def _dsa_prompt_kernel(qi_ref, wi_ref, ki_ref, q_ref, k_ref, v_ref, o_ref, key_ref, bias_ref, q3_ref,
                       mx_ref, lsum_ref, acc_ref, *, topk, nkv, ih):
    i = pl.program_id(1)
    tq = ck = QTILE
    key_i = lax.broadcasted_iota(I32, (ck, tq), 0)
    qry_i = lax.broadcasted_iota(I32, (ck, tq), 1)
    wi_t = wi_ref[...].T

    def causal_ok(kc):
        return key_i <= qry_i + jnp.where(kc < i, ck, 0)

    def idx_body(kc, _):
        off = pl.multiple_of(kc * ck, ck)
        kic = ki_ref[0, pl.ds(off, ck), :]
        acc = jnp.zeros((ck, tq), F32)
        for h in range(ih):
            s = lax.dot_general(kic, qi_ref[:, h * IDX_DIM:(h + 1) * IDX_DIM], _CONTRACT_LAST,
                                preferred_element_type=F32)
            acc = acc + jnp.maximum(s, 0.0) * wi_t[h:h + 1, :]
        key_ref[kc] = _sortable_key(jnp.where(causal_ok(kc), acc, -jnp.inf))
        return 0

    lax.fori_loop(0, i + 1, idx_body, 0)

    def count_ge(cand):
        def body(kc, acc):
            ge = jnp.where(key_ref[kc] >= cand, 1.0, 0.0)
            return acc + jnp.sum(ge.reshape(ck // 8, 8, tq), axis=0)
        acc = lax.fori_loop(0, i + 1, body, jnp.zeros((8, tq), F32))
        return jnp.sum(acc, axis=0, keepdims=True)

    thr = _kth_largest_key(count_ge, (1, tq), topk)

    def bias_body(kc, _):
        sel = jnp.logical_and(key_ref[kc] >= thr, causal_ok(kc))
        bias_ref[kc] = jnp.where(sel, 0.0, NEG).T
        return 0

    lax.fori_loop(0, i + 1, bias_body, 0)

    for n in range(nkv):
        q3_ref[n] = jnp.concatenate(
            [q_ref[:, (n * GROUP + g) * HEAD_DIM:(n * GROUP + g + 1) * HEAD_DIM] for g in range(GROUP)], axis=0)

    def chunk(ref, kc, n):
        return ref[0, pl.ds(pl.multiple_of(kc * ck, ck), ck), n * HEAD_DIM:(n + 1) * HEAD_DIM]

    def chunk_logits(kc):
        bias = jnp.concatenate([bias_ref[kc]] * GROUP, axis=0)

        def of_head(n):
            s = lax.dot_general(q3_ref[n], chunk(k_ref, kc, n), _CONTRACT_LAST, preferred_element_type=F32)
            return s * SOFTMAX_C2 + bias
        return of_head

    _masked_attention(nkv, i + 1, chunk_logits, functools.partial(chunk, v_ref), mx_ref, lsum_ref, acc_ref)
    _write_heads(o_ref, lsum_ref, acc_ref, nkv, tq)


def _dsa_prompt(qi, wi, ki, q, k, v, bsz, t, nkv, ih, topk):
    nt = t // QTILE
    qw = nkv * GROUP * HEAD_DIM
    kvw = nkv * HEAD_DIM
    rows = lambda w: pl.BlockSpec((QTILE, w), lambda b, i: (b * nt + i, 0))
    seq = lambda w: pl.BlockSpec((1, t, w), lambda b, i: (b, 0, 0))
    return pl.pallas_call(
        functools.partial(_dsa_prompt_kernel, topk=topk, nkv=nkv, ih=ih),
        grid=(bsz, nt),
        in_specs=[rows(ih * IDX_DIM), rows(128), seq(IDX_DIM), rows(qw), seq(kvw), seq(kvw)],
        out_specs=rows(qw),
        out_shape=jax.ShapeDtypeStruct((bsz * t, qw), BF16),
        scratch_shapes=[pltpu.VMEM((nt, QTILE, QTILE), I32), pltpu.VMEM((nt, QTILE, QTILE), F32),
                        pltpu.VMEM((nkv, GROUP * QTILE, HEAD_DIM), BF16)] + _attn_state_scratch(nkv),
        compiler_params=_cparams("parallel", "arbitrary"),
        name="dsa_prompt_attn",
    )(qi, wi, ki.reshape(bsz, t, IDX_DIM), q, k.reshape(bsz, t, kvw), v.reshape(bsz, t, kvw))


def _topb_select_bias(gate, valid, blk_iota, nblk):
    g = jnp.where(valid, gate, -jnp.inf)
    rank = jnp.zeros(g.shape, I32)
    for m in range(nblk):
        gm = g[:, m:m + 1]
        beats = jnp.logical_or(gm > g, jnp.logical_and(gm == g, blk_iota > m))
        rank = rank + beats.astype(I32)
    sel = jnp.logical_and(valid, rank < MOBA_TOPB)
    return jnp.where(sel, 0.0, NEG)


def _topb_select_rows(gate_t, valid_t, nblk):
    g = jnp.where(valid_t, gate_t, -jnp.inf)
    blk = lax.broadcasted_iota(I32, g.shape, 0)
    rank = jnp.zeros(g.shape, I32)
    for m in range(nblk):
        gm = g[m:m + 1, :]
        beats = jnp.logical_or(gm > g, jnp.logical_and(gm == g, blk > m))
        rank = rank + beats.astype(I32)
    return jnp.logical_and(valid_t, rank < MOBA_TOPB)


def _moba_prompt_kernel(q_ref, k_ref, v_ref, km_ref, o_ref, qa_ref, so_ref, mx_ref, lsum_ref, acc_ref, *, nkv, nblk):
    j = pl.program_id(1)
    tq = blk = QTILE
    rows = GROUP * tq
    row = lax.broadcasted_iota(I32, (tq, blk), 0)
    col = lax.broadcasted_iota(I32, (tq, blk), 1)
    cb = jnp.where(col <= row, 0.0, NEG)
    causal_bias = jnp.concatenate([cb] * GROUP, axis=0)
    valid_t = lax.broadcasted_iota(I32, (nblk, rows), 0) < j
    lane = lax.broadcasted_iota(I32, (blk, HEAD_DIM), 1)
    km_pad = jnp.zeros((16 - nblk % 16, HEAD_DIM), F32)

    def chunk(ref, m, n):
        return ref[0, pl.ds(pl.multiple_of(m * blk, blk), blk), n * HEAD_DIM:(n + 1) * HEAD_DIM]

    for n in range(nkv):
        lanes = slice(n * HEAD_DIM, (n + 1) * HEAD_DIM)
        q3 = jnp.concatenate(
            [q_ref[:, (n * GROUP + g) * HEAD_DIM:(n * GROUP + g + 1) * HEAD_DIM] for g in range(GROUP)], axis=0)
        km = jnp.concatenate([km_ref[0, :, lanes], km_pad], axis=0).astype(BF16)
        gate_t = lax.dot_general(km, q3, _CONTRACT_LAST, preferred_element_type=F32)[:nblk]
        selb_t = jnp.where(_topb_select_rows(gate_t, valid_t, nblk), 0.0, NEG)
        selb = jnp.concatenate([selb_t, jnp.zeros((HEAD_DIM - nblk, rows), F32)], axis=0).T
        qa_ref[n] = jnp.concatenate([q3, selb.astype(BF16)], axis=1)
        own_s = lax.dot_general(q3, chunk(k_ref, j, n), _CONTRACT_LAST, preferred_element_type=F32)
        so_ref[n] = own_s * SOFTMAX_C2 + causal_bias

    def chunk_logits(m):
        onehot = jnp.where(lane == m, 1.0, 0.0).astype(BF16)

        def of_head(n):
            k_aug = jnp.concatenate([chunk(k_ref, m, n), onehot], axis=1)
            return lax.dot_general(qa_ref[n], k_aug, _CONTRACT_LAST, preferred_element_type=F32) * SOFTMAX_C2
        return of_head

    _masked_attention(nkv, j, chunk_logits, functools.partial(chunk, v_ref), mx_ref, lsum_ref, acc_ref,
                      own=(so_ref, lambda n: chunk(v_ref, j, n)))
    _write_heads(o_ref, lsum_ref, acc_ref, nkv, tq)


def _moba_prompt(q, k, v, kmean, bsz, t, nkv):
    nt = t // QTILE
    qw = nkv * GROUP * HEAD_DIM
    kvw = nkv * HEAD_DIM
    rows = lambda w: pl.BlockSpec((QTILE, w), lambda b, i: (b * nt + i, 0))
    seq = lambda w: pl.BlockSpec((1, t, w), lambda b, i: (b, 0, 0))
    return pl.pallas_call(
        functools.partial(_moba_prompt_kernel, nkv=nkv, nblk=nt),
        grid=(bsz, nt),
        in_specs=[rows(qw), seq(kvw), seq(kvw), pl.BlockSpec((1, nt, kvw), lambda b, i: (b, 0, 0))],
        out_specs=rows(qw),
        out_shape=jax.ShapeDtypeStruct((bsz * t, qw), BF16),
        scratch_shapes=[pltpu.VMEM((nkv, GROUP * QTILE, 2 * HEAD_DIM), BF16),
                        pltpu.VMEM((nkv, GROUP * QTILE, QTILE), F32)] + _attn_state_scratch(nkv),
        compiler_params=_cparams("parallel", "arbitrary"),
        name="moba_prompt_attn",
    )(q, k.reshape(bsz, t, kvw), v.reshape(bsz, t, kvw), kmean.reshape(bsz, nt, kvw))


def _mem_attn_kernel(q_ref, mk_ref, mv_ref, o_ref, *, hd):
    scale = hd ** -0.5
    for c in range(MEM_HEADS):
        lanes = slice(c * hd, (c + 1) * hd)
        q = q_ref[0, :, lanes].astype(BF16)
        mk = mk_ref[0, :, lanes].astype(BF16)
        mv = mv_ref[0, :, lanes].astype(BF16)
        s = lax.dot_general(q, mk, _CONTRACT_LAST, preferred_element_type=F32) * scale
        p = jnp.exp(s - jnp.max(s, axis=1, keepdims=True))
        l = jnp.sum(p, axis=1, keepdims=True)
        o = jnp.dot(p.astype(BF16), mv, preferred_element_type=F32) / l
        o_ref[0, :, lanes] = o.astype(o_ref.dtype)


def _mem_attn(qm, mk, mv, out_dtype):
    bsz, t, w = qm.shape
    mlen = mk.shape[1]
    tq = _pick_block(t, 512, 8)
    return pl.pallas_call(
        functools.partial(_mem_attn_kernel, hd=w // MEM_HEADS),
        grid=(bsz, t // tq),
        in_specs=[pl.BlockSpec((1, tq, w), lambda b, i: (b, i, 0)),
                  pl.BlockSpec((1, mlen, w), lambda b, i: (b, 0, 0)),
                  pl.BlockSpec((1, mlen, w), lambda b, i: (b, 0, 0))],
        out_specs=pl.BlockSpec((1, tq, w), lambda b, i: (b, i, 0)),
        out_shape=jax.ShapeDtypeStruct((bsz, t, w), out_dtype),
        compiler_params=_cparams("parallel", "parallel"),
        name="mem_attn",
    )(qm, mk, mv)


SAMPLE_PP = 16


def _page_specs(shape_tail, pp, layer):
    nd = len(shape_tail)
    return [pl.BlockSpec((None, None) + shape_tail,
                         lambda b, p, pt, c=c: (layer, pt[b, p * pp + c]) + (0,) * nd)
            for c in range(pp)]


def _head_sum(w, ih):
    acc = w[0:SAMPLE_TPAD]
    for h in range(1, ih):
        acc = acc + w[h * SAMPLE_TPAD:(h + 1) * SAMPLE_TPAD]
    return acc


def _dsa_sample_index_kernel(pt_ref, qi_ref, wi_ref, *refs, pp, ih, page):
    page_refs, o_ref = refs[:pp], refs[pp]
    qi = qi_ref[0].astype(BF16)
    wi = wi_ref[0]
    for c in range(pp):
        kp = page_refs[c][...].astype(BF16)
        s = lax.dot_general(qi, kp, _CONTRACT_LAST, preferred_element_type=F32)
        o_ref[0, :, c * page:(c + 1) * page] = _head_sum(jnp.maximum(s, 0.0) * wi, ih)


def _dsa_sample_index(page_table, qi_ht, wi_ht, kidx_pool, layer, ih):
    bs, npg = page_table.shape
    page = kidx_pool.shape[2]
    pp = SAMPLE_PP
    r = ih * SAMPLE_TPAD
    grid_spec = pltpu.PrefetchScalarGridSpec(
        num_scalar_prefetch=1,
        grid=(bs, npg // pp),
        in_specs=[pl.BlockSpec((1, r, IDX_DIM), lambda b, p, pt: (b, 0, 0)),
                  pl.BlockSpec((1, r, page), lambda b, p, pt: (b, 0, 0))]
        + _page_specs((page, IDX_DIM), pp, layer),
        out_specs=pl.BlockSpec((1, SAMPLE_TPAD, page * pp), lambda b, p, pt: (b, 0, p)),
    )
    return pl.pallas_call(
        functools.partial(_dsa_sample_index_kernel, pp=pp, ih=ih, page=page),
        grid_spec=grid_spec,
        out_shape=jax.ShapeDtypeStruct((bs, SAMPLE_TPAD, npg * page), F32),
        compiler_params=_cparams("parallel", "arbitrary"),
        name="dsa_sample_index",
    )(page_table, qi_ht, wi_ht, *([kidx_pool] * pp))


def _dsa_sample_select_kernel(sc_ref, qi_ref, wi_ref, kin_ref, o_ref, *, topk, ih, ts, past):
    tail_w = kin_ref.shape[1]
    s = lax.dot_general(qi_ref[0].astype(BF16), kin_ref[0].astype(BF16), _CONTRACT_LAST,
                        preferred_element_type=F32)
    tail = _head_sum(jnp.maximum(s, 0.0) * wi_ref[0][:, :tail_w], ih)
    row = lax.broadcasted_iota(I32, (SAMPLE_TPAD, tail_w), 0)
    col = lax.broadcasted_iota(I32, (SAMPLE_TPAD, tail_w), 1)
    tail_ok = jnp.logical_and(col <= row, col < ts)
    full = jnp.concatenate([sc_ref[0], jnp.where(tail_ok, tail, -jnp.inf)], axis=1)
    keys = _sortable_key(full)
    width = past + tail_w
    col_f = lax.broadcasted_iota(I32, (SAMPLE_TPAD, width), 1)
    row_f = lax.broadcasted_iota(I32, (SAMPLE_TPAD, width), 0)
    visible = jnp.logical_or(col_f < past, jnp.logical_and(col_f - past <= row_f, col_f - past < ts))

    def count_ge(cand):
        return jnp.sum((keys >= cand).astype(F32), axis=1, keepdims=True)

    thr = _kth_largest_key(count_ge, (SAMPLE_TPAD, 1), topk)
    o_ref[0] = jnp.where(jnp.logical_and(keys >= thr, visible), 0.0, NEG)


def _dsa_sample_select(scores, qi_ht, wi_ht, ki_new, ih, ts, topk):
    bs, _, past = scores.shape
    tail_w = ki_new.shape[1]
    r = ih * SAMPLE_TPAD
    return pl.pallas_call(
        functools.partial(_dsa_sample_select_kernel, topk=topk, ih=ih, ts=ts, past=past),
        grid=(bs,),
        in_specs=[pl.BlockSpec((1, SAMPLE_TPAD, past), lambda b: (b, 0, 0)),
                  pl.BlockSpec((1, r, IDX_DIM), lambda b: (b, 0, 0)),
                  pl.BlockSpec((1, r, wi_ht.shape[2]), lambda b: (b, 0, 0)),
                  pl.BlockSpec((1, tail_w, IDX_DIM), lambda b: (b, 0, 0))],
        out_specs=pl.BlockSpec((1, SAMPLE_TPAD, past + tail_w), lambda b: (b, 0, 0)),
        out_shape=jax.ShapeDtypeStruct((bs, SAMPLE_TPAD, past + tail_w), F32),
        compiler_params=_cparams("parallel"),
        name="dsa_sample_select",
    )(scores, qi_ht, wi_ht, ki_new)


def _sample_flash_update(qbd, kblk, vblk, bias, m_ref, l_ref, acc_ref, nkv):
    rows_per_head = GROUP * SAMPLE_TPAD
    s = lax.dot_general(qbd, kblk, _CONTRACT_LAST, preferred_element_type=F32) * (HEAD_DIM ** -0.5) + bias
    m_old = m_ref[...]
    m_new = jnp.maximum(m_old, jnp.max(s, axis=1, keepdims=True))
    alpha = jnp.exp(m_old - m_new)
    p = jnp.exp(s - m_new)
    l_ref[...] = alpha * l_ref[...] + jnp.sum(p, axis=1, keepdims=True)
    m_ref[...] = m_new
    o_full = jnp.dot(p.astype(BF16), vblk, preferred_element_type=F32)
    o_diag = jnp.concatenate(
        [o_full[n * rows_per_head:(n + 1) * rows_per_head, n * HEAD_DIM:(n + 1) * HEAD_DIM] for n in range(nkv)],
        axis=0)
    acc_ref[...] = alpha * acc_ref[...] + o_diag


def _sample_flash_reset(m_ref, l_ref, acc_ref):
    m_ref[...] = jnp.full(m_ref.shape, NEG, F32)
    l_ref[...] = jnp.zeros(l_ref.shape, F32)
    acc_ref[...] = jnp.zeros(acc_ref.shape, F32)


def _heads_on_lanes(refs, nkv):
    keys = refs[0].shape[0] // nkv
    pages = [jnp.concatenate([r[pl.ds(n, keys, stride=nkv), :] for n in range(nkv)], axis=1).astype(BF16)
             for r in refs]
    return jnp.concatenate(pages, axis=0)


def _flat_pool(pool):
    return pool.reshape(pool.shape[0], pool.shape[1], pool.shape[2] * pool.shape[3], pool.shape[4])


def _tile_query_bias(b8, reps):
    return jnp.concatenate([b8] * reps, axis=0)


def _dsa_sample_attn_kernel(pt_ref, q_ref, bias_ref, tbias_ref, kn_ref, vn_ref, *refs, pp, nkv):
    k_refs, v_refs = refs[:pp], refs[pp:2 * pp]
    o_ref, m_ref, l_ref, acc_ref = refs[2 * pp:]
    p = pl.program_id(1)
    reps = nkv * GROUP

    @pl.when(p == 0)
    def _():
        _sample_flash_reset(m_ref, l_ref, acc_ref)

    _sample_flash_update(q_ref[0], _heads_on_lanes(k_refs, nkv), _heads_on_lanes(v_refs, nkv),
                         _tile_query_bias(bias_ref[0], reps), m_ref, l_ref, acc_ref, nkv)

    @pl.when(p == pl.num_programs(1) - 1)
    def _():
        _sample_flash_update(q_ref[0], kn_ref[0].astype(BF16), vn_ref[0].astype(BF16),
                             _tile_query_bias(tbias_ref[0], reps), m_ref, l_ref, acc_ref, nkv)
        o_ref[0] = acc_ref[...] / l_ref[...]


def _sample_attn_specs(nkv, tail_rows):
    rows = nkv * GROUP * SAMPLE_TPAD
    kvw = nkv * HEAD_DIM
    q_spec = pl.BlockSpec((1, rows, kvw), lambda b, p, pt: (b, 0, 0))
    tail_spec = pl.BlockSpec((1, tail_rows, kvw), lambda b, p, pt: (b, 0, 0))
    out_spec = pl.BlockSpec((1, rows, HEAD_DIM), lambda b, p, pt: (b, 0, 0))
    scratch = [pltpu.VMEM((rows, 1), F32), pltpu.VMEM((rows, 1), F32), pltpu.VMEM((rows, HEAD_DIM), F32)]
    return rows, q_spec, tail_spec, out_spec, scratch


def _dsa_sample_attn(page_table, qbd, bias, k_new, v_new, k_pool, v_pool, layer, nkv):
    bs, npg = page_table.shape
    page = k_pool.shape[2]
    pp = SAMPLE_PP
    rows, q_spec, tail_spec, out_spec, scratch = _sample_attn_specs(nkv, k_new.shape[1])
    grid_spec = pltpu.PrefetchScalarGridSpec(
        num_scalar_prefetch=1,
        grid=(bs, npg // pp),
        in_specs=[q_spec,
                  pl.BlockSpec((1, SAMPLE_TPAD, page * pp), lambda b, p, pt: (b, 0, p)),
                  pl.BlockSpec((1, SAMPLE_TPAD, page), lambda b, p, pt: (b, 0, npg)),
                  tail_spec, tail_spec]
        + _page_specs((page * nkv, HEAD_DIM), pp, layer) + _page_specs((page * nkv, HEAD_DIM), pp, layer),
        out_specs=out_spec,
        scratch_shapes=scratch,
    )
    return pl.pallas_call(
        functools.partial(_dsa_sample_attn_kernel, pp=pp, nkv=nkv),
        grid_spec=grid_spec,
        out_shape=jax.ShapeDtypeStruct((bs, rows, HEAD_DIM), F32),
        compiler_params=_cparams("parallel", "arbitrary"),
        name="dsa_sample_attn",
    )(page_table, qbd, bias, bias, k_new, v_new, *([_flat_pool(k_pool)] * pp), *([_flat_pool(v_pool)] * pp))


KMEAN_BLOCKS = 8


def _moba_kmean_kernel(pt_ref, *refs, nblk, ppb):
    k_refs, o_ref = refs[:nblk * ppb], refs[nblk * ppb]
    for blk in range(nblk):
        acc = jnp.sum(k_refs[blk * ppb][...], axis=0)
        for c in range(1, ppb):
            acc = acc + jnp.sum(k_refs[blk * ppb + c][...], axis=0)
        o_ref[0, blk] = acc * (1.0 / MOBA_BLOCK)


def _moba_sample_kmean(page_table, k_pool, layer, nkv):
    bs, npg = page_table.shape
    page = k_pool.shape[2]
    ppb = MOBA_BLOCK // page
    nb = npg // ppb
    nblk = KMEAN_BLOCKS
    grid_spec = pltpu.PrefetchScalarGridSpec(
        num_scalar_prefetch=1,
        grid=(bs, nb // nblk),
        in_specs=_page_specs((page, nkv, HEAD_DIM), nblk * ppb, layer),
        out_specs=pl.BlockSpec((1, nblk, nkv, HEAD_DIM), lambda b, p, pt: (b, p, 0, 0)),
    )
    return pl.pallas_call(
        functools.partial(_moba_kmean_kernel, nblk=nblk, ppb=ppb),
        grid_spec=grid_spec,
        out_shape=jax.ShapeDtypeStruct((bs, nb, nkv, HEAD_DIM), F32),
        compiler_params=_cparams("parallel", "arbitrary"),
        name="moba_sample_kmean",
    )(page_table, *([k_pool] * (nblk * ppb)))


def _moba_sample_attn_kernel(pt_ref, q_ref, km_ref, kn_ref, vn_ref, *refs, pp, nkv, page, nb, ts):
    k_refs, v_refs = refs[:pp], refs[pp:2 * pp]
    o_ref, m_ref, l_ref, acc_ref, sb_ref = refs[2 * pp:]
    p = pl.program_id(1)
    rows = nkv * GROUP * SAMPLE_TPAD
    ppb = MOBA_BLOCK // page
    blk_iota = lax.broadcasted_iota(I32, (rows, nb), 1)

    @pl.when(p == 0)
    def _():
        _sample_flash_reset(m_ref, l_ref, acc_ref)
        valid = blk_iota >= 0
        gate = lax.dot_general(q_ref[0], km_ref[0].astype(BF16), _CONTRACT_LAST, preferred_element_type=F32)
        sb_ref[...] = _topb_select_bias(gate, valid, blk_iota, nb)

    selb = sb_ref[...]
    cols = []
    for c in range(pp // ppb):
        blk = p * (pp // ppb) + c
        col = jnp.sum(jnp.where(blk_iota == blk, selb, 0.0), axis=1, keepdims=True)
        cols.append(jnp.broadcast_to(col, (rows, MOBA_BLOCK)))
    _sample_flash_update(q_ref[0], _heads_on_lanes(k_refs, nkv), _heads_on_lanes(v_refs, nkv),
                         jnp.concatenate(cols, axis=1), m_ref, l_ref, acc_ref, nkv)

    @pl.when(p == pl.num_programs(1) - 1)
    def _():
        tail_w = kn_ref.shape[1]
        row = lax.broadcasted_iota(I32, (rows, tail_w), 0) & (SAMPLE_TPAD - 1)
        col = lax.broadcasted_iota(I32, (rows, tail_w), 1)
        tb = jnp.where(jnp.logical_and(col <= row, col < ts), 0.0, NEG)
        _sample_flash_update(q_ref[0], kn_ref[0].astype(BF16), vn_ref[0].astype(BF16), tb,
                             m_ref, l_ref, acc_ref, nkv)
        o_ref[0] = acc_ref[...] / l_ref[...]


def _moba_sample_attn(page_table, qbd, kmean, k_new, v_new, k_pool, v_pool, layer, nkv, ts):
    bs, npg = page_table.shape
    page = k_pool.shape[2]
    kvw = nkv * HEAD_DIM
    pp = SAMPLE_PP
    nb = kmean.shape[1]
    rows, q_spec, tail_spec, out_spec, scratch = _sample_attn_specs(nkv, k_new.shape[1])
    grid_spec = pltpu.PrefetchScalarGridSpec(
        num_scalar_prefetch=1,
        grid=(bs, npg // pp),
        in_specs=[q_spec, pl.BlockSpec((1, nb, kvw), lambda b, p, pt: (b, 0, 0)), tail_spec, tail_spec]
        + _page_specs((page * nkv, HEAD_DIM), pp, layer) + _page_specs((page * nkv, HEAD_DIM), pp, layer),
        out_specs=out_spec,
        scratch_shapes=scratch + [pltpu.VMEM((rows, nb), F32)],
    )
    return pl.pallas_call(
        functools.partial(_moba_sample_attn_kernel, pp=pp, nkv=nkv, page=page, nb=nb, ts=ts),
        grid_spec=grid_spec,
        out_shape=jax.ShapeDtypeStruct((bs, rows, HEAD_DIM), F32),
        compiler_params=_cparams("parallel", "arbitrary"),
        name="moba_sample_attn",
    )(page_table, qbd, kmean, k_new, v_new, *([_flat_pool(k_pool)] * pp), *([_flat_pool(v_pool)] * pp))


def _heads_to_kv_major(q, bs, nkv):
    x = q.reshape(bs, SAMPLE_TPAD, nkv, GROUP, HEAD_DIM).transpose(0, 2, 3, 1, 4)
    eye = jnp.eye(nkv, dtype=q.dtype)
    bd = x[:, :, :, :, None, :] * eye[None, :, None, None, :, None]
    return bd.reshape(bs, nkv * GROUP * SAMPLE_TPAD, nkv * HEAD_DIM).astype(BF16)


def _kv_major_to_rows(o, bs, nkv):
    x = o.reshape(bs, nkv, GROUP, SAMPLE_TPAD, HEAD_DIM)
    return x.transpose(0, 3, 1, 2, 4).reshape(bs * SAMPLE_TPAD, nkv * GROUP * HEAD_DIM)


def _pad_rows(x, n):
    return jnp.pad(x, ((0, 0), (0, n - x.shape[1]), (0, 0)))


def _merge_mixers(h, mix, qm, mk, mv, bsz):
    m = h.shape[0]
    ma = _mem_attn(qm.reshape(bsz, m // bsz, -1), mk, mv, qm.dtype).reshape(m, -1)
    return jnp.concatenate([mix.astype(BF16), ma.astype(BF16)], axis=1)


def _finish_layer(hp, hs, merged_p, merged_s, layer, w_o, g_ffn, w_up, w_down):
    hp = _matmul(merged_p, w_o, layer, F32, epilogue="residual", residual=hp)
    hs = _matmul(merged_s, w_o, layer, F32, epilogue="residual", residual=hs)
    up, us = _matmul(_rmsnorm(hp, g_ffn, BF16), w_up, layer, BF16, epilogue="relu2", a2=_rmsnorm(hs, g_ffn, BF16))
    hp = _matmul(up, w_down, layer, F32, epilogue="residual", residual=hp)
    hs = _matmul(us, w_down, layer, F32, epilogue="residual", residual=hs)
    return hp, hs


def kernel(x_prompt, x_sample, cache_dsa_k, cache_dsa_v, cache_dsa_kidx, cache_moba_k, cache_moba_v, cache_mem_k, cache_mem_v, page_table, mem_prompt, norm_mix, norm_mem, w_in_dsa, w_in_moba, w_mem_kv, w_out, norm_ffn, w_up, w_down, norm_final):
    bsz, t, d = x_prompt.shape
    bs, ts, _ = x_sample.shape
    depth = norm_mix.shape[0]
    nh = (3 * d) // (4 * HEAD_DIM)
    nkv = nh // GROUP
    qw, kvw = nh * HEAD_DIM, nkv * HEAD_DIM
    memw = d // 4
    mhd = memw // MEM_HEADS
    ih = d // 128
    npg = page_table.shape[1]
    page = cache_dsa_k.shape[2]
    past = npg * page
    mlen = mem_prompt.shape[1]
    tp = SAMPLE_TPAD

    hp = x_prompt.reshape(bsz * t, d)
    hs = jnp.pad(x_sample, ((0, 0), (0, tp - ts), (0, 0))).reshape(bs * tp, d)
    pos_p = jnp.arange(t, dtype=I32)
    pos_s = past + (jnp.arange(tp, dtype=I32) % ts)
    tabs_p = _rope_tables(pos_p)
    tabs_s = _rope_tables(jnp.tile(pos_s, bs))

    outs = {name: [] for name in ("pdk", "pdv", "pdki", "sdk", "sdv", "sdki", "pmk", "pmv", "smk", "smv", "mk", "mv")}
    mem_flat = mem_prompt.reshape(bsz * mlen, d)

    w_o, wu, wd, w_mem = w_out.astype(BF16), w_up.astype(BF16), w_down.astype(BF16), w_mem_kv.astype(BF16)
    w_dsa, w_moba = w_in_dsa.astype(BF16), w_in_moba.astype(BF16)
    c0 = qw + 2 * kvw + ih * IDX_DIM
    w_dsa_tail = jnp.concatenate(
        [w_in_dsa[:, :, c0 + ih:c0 + ih + IDX_DIM], w_in_dsa[:, :, c0 + ih + IDX_DIM:], w_in_dsa[:, :, c0:c0 + ih],
         jnp.zeros(w_in_dsa.shape[:2] + (128 - ih,), F32)], axis=2).astype(BF16)

    for i in range(depth):
        j = i // 2
        mkv = _matmul(_rmsnorm(mem_flat, norm_mem[i], BF16), w_mem, i, F32)
        mk_p = mkv[:, :memw].reshape(bsz, mlen, memw)
        mv_p = mkv[:, memw:].reshape(bsz, mlen, memw)
        outs["mk"].append(mk_p.reshape(bsz, mlen, MEM_HEADS, mhd))
        outs["mv"].append(mv_p.reshape(bsz, mlen, MEM_HEADS, mhd))
        hn_p = _rmsnorm(hp, norm_mix[i], BF16)
        hn_s = _rmsnorm(hs, norm_mix[i], BF16)
        mk_s = cache_mem_k[i].reshape(bs, mlen, memw)
        mv_s = cache_mem_v[i].reshape(bs, mlen, memw)
        if i % 2 == 0:
            pa, pa_s = _matmul(hn_p, w_dsa, j, F32, n=c0, a2=hn_s)
            pb, pb_s = _matmul(hn_p, w_dsa_tail, j, F32, a2=hn_s)
            q, kf, kb, vf, vb, qi, kif, kib, wi, qm = _dsa_split(pa, pb, tabs_p, t, nh, nkv, ih, memw, BF16)
            outs["pdk"].append(kf.reshape(bsz, t, nkv, HEAD_DIM))
            outs["pdv"].append(vf.reshape(bsz, t, nkv, HEAD_DIM))
            outs["pdki"].append(kif.reshape(bsz, t, IDX_DIM))
            mix_p = _dsa_prompt(qi, wi, kib, q, kb, vb, bsz, t, nkv, ih, min(DSA_TOPK, t // 4))
            q_s, kf, _, vf, _, qi_s, kif, _, wi_s, qm_s = _dsa_split(pa_s, pb_s, tabs_s, bs * tp, nh, nkv, ih, memw, F32)
            outs["sdk"].append(kf.reshape(bs, tp, nkv, HEAD_DIM)[:, :ts])
            outs["sdv"].append(vf.reshape(bs, tp, nkv, HEAD_DIM)[:, :ts])
            outs["sdki"].append(kif.reshape(bs, tp, IDX_DIM)[:, :ts])
            qi_ht = qi_s.reshape(bs, tp, ih, IDX_DIM).transpose(0, 2, 1, 3).reshape(bs, ih * tp, IDX_DIM)
            wi_ht = wi_s.reshape(bs, tp, 128)[:, :, :ih].transpose(0, 2, 1).reshape(bs, ih * tp, 1)
            wi_ht = jnp.broadcast_to(wi_ht, (bs, ih * tp, page))
            scores = _dsa_sample_index(page_table, qi_ht, wi_ht, cache_dsa_kidx, j, ih)
            ki_new = _pad_rows(kif.reshape(bs, tp, IDX_DIM), page)
            bias = _dsa_sample_select(scores, qi_ht, wi_ht, ki_new, ih, ts, min(DSA_TOPK, (past + ts) // 4))
            o_s = _dsa_sample_attn(page_table, _heads_to_kv_major(q_s, bs, nkv), bias,
                                   _pad_rows(kf.reshape(bs, tp, kvw), page), _pad_rows(vf.reshape(bs, tp, kvw), page),
                                   cache_dsa_k, cache_dsa_v, j, nkv)
            mix_s = _kv_major_to_rows(o_s, bs, nkv)
        else:
            p, p_s = _matmul(hn_p, w_moba, j, F32, a2=hn_s)
            q, kf, kb, vf, vb, qm, kmean = _moba_split(p, tabs_p, t, nh, nkv, memw, BF16)
            outs["pmk"].append(kf.reshape(bsz, t, nkv, HEAD_DIM))
            outs["pmv"].append(vf.reshape(bsz, t, nkv, HEAD_DIM))
            mix_p = _moba_prompt(q, kb, vb, kmean, bsz, t, nkv)
            q_s, kf, _, vf, _, qm_s, _ = _moba_split(p_s, tabs_s, bs * tp, nh, nkv, memw, F32)
            outs["smk"].append(kf.reshape(bs, tp, nkv, HEAD_DIM)[:, :ts])
            outs["smv"].append(vf.reshape(bs, tp, nkv, HEAD_DIM)[:, :ts])
            kmean_s = _moba_sample_kmean(page_table, cache_moba_k, j, nkv)
            kmean_s = kmean_s.reshape(bs, kmean_s.shape[1], kvw)
            o_s = _moba_sample_attn(page_table, _heads_to_kv_major(q_s, bs, nkv), kmean_s,
                                    _pad_rows(kf.reshape(bs, tp, kvw), page), _pad_rows(vf.reshape(bs, tp, kvw), page),
                                    cache_moba_k, cache_moba_v, j, nkv, ts)
            mix_s = _kv_major_to_rows(o_s, bs, nkv)
        hp, hs = _finish_layer(hp, hs, _merge_mixers(hp, mix_p, qm, mk_p, mv_p, bsz),
                               _merge_mixers(hs, mix_s, qm_s, mk_s, mv_s, bs), i, w_o, norm_ffn[i], wu, wd)

    y_prompt = _rmsnorm(hp, norm_final, F32).reshape(bsz, t, d)
    y_sample = _rmsnorm(hs, norm_final, F32).reshape(bs, tp, d)[:, :ts]
    st = jnp.stack
    return (y_prompt, y_sample, st(outs["pdk"]), st(outs["pdv"]), st(outs["pdki"]), st(outs["pmk"]), st(outs["pmv"]),
            st(outs["mk"]), st(outs["mv"]), st(outs["sdk"]), st(outs["sdv"]), st(outs["sdki"]),
            st(outs["smk"]), st(outs["smv"]))
```

```python
import functools

import jax
import jax.numpy as jnp
from jax import lax
from jax.experimental import pallas as pl
from jax.experimental.pallas import tpu as pltpu

F32 = jnp.float32
BF16 = jnp.bfloat16
I32 = jnp.int32

HEAD_DIM = 128
GROUP = 3
ROT_DIM = 32
ROPE_THETA = 500000.0
IDX_DIM = 128
MEM_HEADS = 4
DSA_TOPK = 256
MOBA_BLOCK = 256
MOBA_TOPB = 3
EPS = 1e-6
NEG = -1e30
LOG2E = 1.4426950408889634
SOFTMAX_C2 = (HEAD_DIM ** -0.5) * LOG2E
QTILE = 256
SAMPLE_TPAD = 8
VMEM_LIMIT = 60 * 1024 * 1024

_CONTRACT_LAST = (((1,), (1,)), ((), ()))


def _cparams(*sem):
    return pltpu.CompilerParams(dimension_semantics=sem, vmem_limit_bytes=VMEM_LIMIT)


def _pick_block(n, pref, align=128):
    if n <= pref:
        return n
    b = (pref // align) * align
    while b >= align:
        if n % b == 0:
            return b
        b -= align
    return n


def _rmsnorm_kernel(x_ref, g_ref, o_ref):
    x = x_ref[...]
    ms = jnp.mean(x * x, axis=-1, keepdims=True)
    o_ref[...] = ((x * lax.rsqrt(ms + EPS)) * g_ref[...]).astype(o_ref.dtype)


def _rmsnorm(x, g, out_dtype):
    m, d = x.shape
    bm = _pick_block(m, 256, 8)
    return pl.pallas_call(
        _rmsnorm_kernel,
        grid=(m // bm,),
        in_specs=[pl.BlockSpec((bm, d), lambda i: (i, 0)), pl.BlockSpec((1, d), lambda i: (0, 0))],
        out_specs=pl.BlockSpec((bm, d), lambda i: (i, 0)),
        out_shape=jax.ShapeDtypeStruct((m, d), out_dtype),
        compiler_params=_cparams("parallel"),
        name="rmsnorm",
    )(x, g.reshape(1, d).astype(F32))


def _mm_tile(a_refs, w, r_ref, o_ref, k, nk, epilogue):
    part, off = None, 0
    for a_ref in a_refs:
        ka = a_ref.shape[1]
        d = jnp.dot(a_ref[...], w[off:off + ka], preferred_element_type=F32)
        part = d if part is None else part + d
        off += ka

    def finish(acc):
        if epilogue == "relu2":
            r = jnp.maximum(acc, 0.0)
            acc = r * r
        elif epilogue == "residual":
            acc = r_ref[...] + acc
        o_ref[...] = acc.astype(o_ref.dtype)

    if nk == 1:
        finish(part)
        return

    @pl.when(k == 0)
    def _():
        o_ref[...] = part

    @pl.when(jnp.logical_and(k > 0, k < nk - 1))
    def _():
        o_ref[...] += part

    @pl.when(k == nk - 1)
    def _():
        finish(o_ref[...] + part)


def _mm_kernel(*refs, nk, epilogue, dual, nparts):
    refs = list(refs)
    a_refs = [refs.pop(0) for _ in range(nparts)]
    w_ref = refs.pop(0)
    r_ref = refs.pop(0) if epilogue == "residual" else None
    a2_ref = refs.pop(0) if dual else None
    r2_ref = refs.pop(0) if dual and epilogue == "residual" else None
    o_ref = refs.pop(0)
    o2_ref = refs.pop(0) if dual else None
    k = pl.program_id(2)
    w = w_ref[...]
    _mm_tile(a_refs, w, r_ref, o_ref, k, nk, epilogue)
    if dual:
        @pl.when(pl.program_id(1) == 0)
        def _():
            _mm_tile([a2_ref], w, r2_ref, o2_ref, k, nk, epilogue)


def _matmul(a, w, layer, out_dtype, epilogue=None, residual=None, n=None, a2=None, residual2=None,
            bm_pref=1024, bn_pref=1024, bk_pref=4096):
    a_parts = a if isinstance(a, tuple) else (a,)
    m = a_parts[0].shape[0]
    kdim = sum(p.shape[1] for p in a_parts)
    n = w.shape[2] if n is None else n
    bm = _pick_block(m, bm_pref, 8)
    bn = _pick_block(n, bn_pref)
    bk = _pick_block(kdim, bk_pref)
    nk = kdim // bk
    dual = a2 is not None
    assert nk == 1 or out_dtype == F32, "a split contraction accumulates in the f32 output block"
    assert nk == 1 or len(a_parts) == 1
    if len(a_parts) == 1:
        in_specs = [pl.BlockSpec((bm, bk), lambda j, i, k: (i, k))]
    else:
        in_specs = [pl.BlockSpec((bm, p.shape[1]), lambda j, i, k: (i, 0)) for p in a_parts]
    in_specs.append(pl.BlockSpec((None, bk, bn), lambda j, i, k: (layer, k, j)))
    args = list(a_parts) + [w]
    out_specs = [pl.BlockSpec((bm, bn), lambda j, i, k: (i, j))]
    out_shape = [jax.ShapeDtypeStruct((m, n), out_dtype)]
    if epilogue == "residual":
        in_specs.append(pl.BlockSpec((bm, bn), lambda j, i, k: (i, j)))
        args.append(residual)
    if dual:
        m2 = a2.shape[0]
        in_specs.append(pl.BlockSpec((m2, bk), lambda j, i, k: (0, k)))
        args.append(a2)
        if epilogue == "residual":
            in_specs.append(pl.BlockSpec((m2, bn), lambda j, i, k: (0, j)))
            args.append(residual2)
        out_specs.append(pl.BlockSpec((m2, bn), lambda j, i, k: (0, j)))
        out_shape.append(jax.ShapeDtypeStruct((m2, n), out_dtype))
    out = pl.pallas_call(
        functools.partial(_mm_kernel, nk=nk, epilogue=epilogue, dual=dual, nparts=len(a_parts)),
        grid=(n // bn, m // bm, nk),
        in_specs=in_specs,
        out_specs=out_specs,
        out_shape=out_shape,
        compiler_params=_cparams("parallel", "arbitrary" if dual else "parallel", "arbitrary"),
        name="matmul",
    )(*args)
    return tuple(out) if dual else out[0]


def _rope_tables(pos):
    half = ROT_DIM // 2
    inv = ROPE_THETA ** (-jnp.arange(half, dtype=F32) / half)
    ang = pos.astype(F32)[:, None] * inv[None, :]
    cos, sin = jnp.cos(ang), jnp.sin(ang)
    r = pos.shape[0]
    z16 = jnp.zeros((r, half), F32)
    zrest = jnp.zeros((r, HEAD_DIM - ROT_DIM), F32)
    c = jnp.concatenate([cos, cos, jnp.ones((r, HEAD_DIM - ROT_DIM), F32)], axis=1)
    sm = jnp.concatenate([-sin, z16, zrest], axis=1)
    sp = jnp.concatenate([z16, sin, zrest], axis=1)
    return c, sm, sp


def _rope_head(x, c, sm, sp):
    half = ROT_DIM // 2
    return x * c + pltpu.roll(x, HEAD_DIM - half, 1) * sm + pltpu.roll(x, half, 1) * sp


def _dsa_split_kernel(pa_ref, pb_ref, c_ref, sm_ref, sp_ref,
                      q_ref, kf_ref, kb_ref, vf_ref, vb_ref, qi_ref, kif_ref, kib_ref, wi_ref, qm_ref,
                      *, nh, nkv, ih, memw):
    c, sm, sp = c_ref[...], sm_ref[...], sp_ref[...]
    qw, kvw = nh * HEAD_DIM, nkv * HEAD_DIM
    for h in range(nh):
        sl = slice(h * HEAD_DIM, (h + 1) * HEAD_DIM)
        q_ref[:, sl] = _rope_head(pa_ref[:, sl], c, sm, sp).astype(q_ref.dtype)
    for h in range(nkv):
        sl = slice(h * HEAD_DIM, (h + 1) * HEAD_DIM)
        kr = _rope_head(pa_ref[:, qw + h * HEAD_DIM: qw + (h + 1) * HEAD_DIM], c, sm, sp)
        kf_ref[:, sl] = kr
        kb_ref[:, sl] = kr.astype(kb_ref.dtype)
    v = pa_ref[:, qw + kvw: qw + 2 * kvw]
    vf_ref[...] = v
    vb_ref[...] = v.astype(vb_ref.dtype)
    base = qw + 2 * kvw
    for h in range(ih):
        sl = slice(h * IDX_DIM, (h + 1) * IDX_DIM)
        qi_ref[:, sl] = _rope_head(pa_ref[:, base + h * IDX_DIM: base + (h + 1) * IDX_DIM], c, sm, sp).astype(qi_ref.dtype)
    ki = _rope_head(pb_ref[:, 0:IDX_DIM], c, sm, sp)
    kif_ref[...] = ki
    kib_ref[...] = ki.astype(kib_ref.dtype)
    qm_ref[...] = pb_ref[:, IDX_DIM:IDX_DIM + memw].astype(qm_ref.dtype)
    wi_ref[...] = pb_ref[:, IDX_DIM + memw:IDX_DIM + memw + 128] * ((ih * IDX_DIM) ** -0.5)


def _dsa_split(pa, pb, tabs, rows_per_seq, nh, nkv, ih, memw, act_dtype):
    m = pa.shape[0]
    bm = _pick_block(rows_per_seq, 128, 8)
    nt = rows_per_seq // bm
    qw, kvw = nh * HEAD_DIM, nkv * HEAD_DIM
    row = lambda w: pl.BlockSpec((bm, w), lambda i: (i, 0))
    tab = pl.BlockSpec((bm, HEAD_DIM), lambda i: (i % nt, 0))
    shapes = [(qw, act_dtype), (kvw, F32), (kvw, act_dtype), (kvw, F32), (kvw, act_dtype),
              (ih * IDX_DIM, act_dtype), (IDX_DIM, F32), (IDX_DIM, act_dtype), (128, F32), (memw, act_dtype)]
    return pl.pallas_call(
        functools.partial(_dsa_split_kernel, nh=nh, nkv=nkv, ih=ih, memw=memw),
        grid=(m // bm,),
        in_specs=[row(pa.shape[1]), row(pb.shape[1]), tab, tab, tab],
        out_specs=[row(w) for w, _ in shapes],
        out_shape=[jax.ShapeDtypeStruct((m, w), dt) for w, dt in shapes],
        compiler_params=_cparams("parallel"),
        name="dsa_split",
    )(pa, pb, *tabs)


def _moba_split_kernel(p_ref, c_ref, sm_ref, sp_ref, q_ref, kf_ref, kb_ref, vf_ref, vb_ref, qm_ref, km_ref,
                       *, nh, nkv, memw):
    c, sm, sp = c_ref[...], sm_ref[...], sp_ref[...]
    qw, kvw = nh * HEAD_DIM, nkv * HEAD_DIM
    for h in range(nh):
        sl = slice(h * HEAD_DIM, (h + 1) * HEAD_DIM)
        q_ref[:, sl] = _rope_head(p_ref[:, sl], c, sm, sp).astype(q_ref.dtype)
    rows = p_ref.shape[0]
    for h in range(nkv):
        sl = slice(h * HEAD_DIM, (h + 1) * HEAD_DIM)
        kr = _rope_head(p_ref[:, qw + h * HEAD_DIM: qw + (h + 1) * HEAD_DIM], c, sm, sp)
        kf_ref[:, sl] = kr
        kb_ref[:, sl] = kr.astype(kb_ref.dtype)
        km_ref[0, :, sl] = jnp.sum(kr, axis=0, keepdims=True) * (1.0 / rows)
    v = p_ref[:, qw + kvw: qw + 2 * kvw]
    vf_ref[...] = v
    vb_ref[...] = v.astype(vb_ref.dtype)
    qm_ref[...] = p_ref[:, qw + 2 * kvw: qw + 2 * kvw + memw].astype(qm_ref.dtype)


def _moba_split(p, tabs, rows_per_seq, nh, nkv, memw, act_dtype):
    m = p.shape[0]
    bm = _pick_block(rows_per_seq, MOBA_BLOCK, 8)
    nt = rows_per_seq // bm
    qw, kvw = nh * HEAD_DIM, nkv * HEAD_DIM
    row = lambda w: pl.BlockSpec((bm, w), lambda i: (i, 0))
    tab = pl.BlockSpec((bm, HEAD_DIM), lambda i: (i % nt, 0))
    shapes = [(qw, act_dtype), (kvw, F32), (kvw, act_dtype), (kvw, F32), (kvw, act_dtype), (memw, act_dtype)]
    return pl.pallas_call(
        functools.partial(_moba_split_kernel, nh=nh, nkv=nkv, memw=memw),
        grid=(m // bm,),
        in_specs=[row(p.shape[1]), tab, tab, tab],
        out_specs=[row(w) for w, _ in shapes] + [pl.BlockSpec((1, 1, kvw), lambda i: (i, 0, 0))],
        out_shape=[jax.ShapeDtypeStruct((m, w), dt) for w, dt in shapes]
        + [jax.ShapeDtypeStruct((m // bm, 1, kvw), F32)],
        compiler_params=_cparams("parallel"),
        name="moba_split",
    )(p, *tabs)


def _sortable_key(x):
    bits = pltpu.bitcast(x, I32)
    return jnp.where(bits < 0, bits ^ jnp.int32(0x7FFFFFFF), bits)


def _kth_largest_key(count_ge, shape, k):
    imin = jnp.int32(-2 ** 31)
    c0 = count_ge(jnp.zeros(shape, I32))
    thr = jnp.where(c0 >= k, jnp.int32(0), imin)

    def bit_body(it, thr):
        cand = thr + jnp.left_shift(jnp.int32(1), jnp.int32(30) - it)
        return jnp.where(count_ge(cand) >= k, cand, thr)

    return lax.fori_loop(0, 31, bit_body, thr)


def _lane_fold(x, acc, op):
    for w in range(x.shape[1] // 128):
        acc = op(acc, x[:, w * 128:(w + 1) * 128])
    return acc


def _chunk_loop(n, body):
    def pair(c2, _):
        body(2 * c2)
        body(2 * c2 + 1)
        return 0

    def single(c, _):
        body(c)
        return 0

    npair = n // 2
    lax.fori_loop(0, npair, pair, 0)
    lax.fori_loop(2 * npair, n, single, 0)


def _masked_attention(nheads, n_dyn, chunk_logits, load_v, mx_ref, lsum_ref, acc_ref, own=None):
    rows = mx_ref.shape[1]
    for n in range(nheads):
        mx = jnp.full((rows, 128), NEG, F32)
        if own is not None:
            mx = _lane_fold(own[0][n], mx, jnp.maximum)
        mx_ref[n] = mx

    def pass_a(c):
        of_head = chunk_logits(c)
        for n in range(nheads):
            mx_ref[n] = _lane_fold(of_head(n), mx_ref[n], jnp.maximum)

    _chunk_loop(n_dyn, pass_a)
    for n in range(nheads):
        mx_ref[n] = jnp.broadcast_to(jnp.max(mx_ref[n], axis=1, keepdims=True), (rows, 128))

    def accumulate(n, s, vb, first=False):
        mb = mx_ref[n]
        p = jnp.exp2(s - jnp.concatenate([mb] * (s.shape[1] // 128), axis=1))
        pv = jnp.dot(p.astype(BF16), vb, preferred_element_type=F32)
        if first:
            lsum_ref[n] = _lane_fold(p, jnp.zeros((rows, 128), F32), jnp.add)
            acc_ref[n] = pv
        else:
            lsum_ref[n] = _lane_fold(p, lsum_ref[n], jnp.add)
            acc_ref[n] = acc_ref[n] + pv

    for n in range(nheads):
        if own is not None:
            accumulate(n, own[0][n], own[1](n), first=True)
        else:
            lsum_ref[n] = jnp.zeros((rows, 128), F32)
            acc_ref[n] = jnp.zeros((rows, HEAD_DIM), F32)

    def pass_b(c):
        of_head = chunk_logits(c)
        for n in range(nheads):
            accumulate(n, of_head(n), load_v(c, n))

    _chunk_loop(n_dyn, pass_b)


def _write_heads(o_ref, lsum_ref, acc_ref, nkv, tq):
    for n in range(nkv):
        out = acc_ref[n] / jnp.sum(lsum_ref[n], axis=1, keepdims=True)
        for g in range(GROUP):
            o_ref[:, (n * GROUP + g) * HEAD_DIM:(n * GROUP + g + 1) * HEAD_DIM] = (
                out[g * tq:(g + 1) * tq].astype(o_ref.dtype))


def _attn_state_scratch(nkv):
    return [pltpu.VMEM((nkv, GROUP * QTILE, 128), F32)] * 3


def _dsa_prompt_kernel(qi_ref, wi_ref, ki_ref, q_ref, k_ref, v_ref, o_ref, key_ref, bias_ref, q3_ref,
                       mx_ref, lsum_ref, acc_ref, *, topk, nkv, ih):
    i = pl.program_id(1)
    tq = ck = QTILE
    key_i = lax.broadcasted_iota(I32, (ck, tq), 0)
    qry_i = lax.broadcasted_iota(I32, (ck, tq), 1)
    wi_t = wi_ref[...].T

    def causal_ok(kc):
        return key_i <= qry_i + jnp.where(kc < i, ck, 0)

    def idx_body(kc, _):
        off = pl.multiple_of(kc * ck, ck)
        kic = ki_ref[0, pl.ds(off, ck), :]
        acc = jnp.zeros((ck, tq), F32)
        for h in range(ih):
            s = lax.dot_general(kic, qi_ref[:, h * IDX_DIM:(h + 1) * IDX_DIM], _CONTRACT_LAST,
                                preferred_element_type=F32)
            acc = acc + jnp.maximum(s, 0.0) * wi_t[h:h + 1, :]
        key_ref[kc] = _sortable_key(jnp.where(causal_ok(kc), acc, -jnp.inf))
        return 0

    lax.fori_loop(0, i + 1, idx_body, 0)

    def count_ge(cand):
        def body(kc, acc):
            ge = jnp.where(key_ref[kc] >= cand, 1.0, 0.0)
            return acc + jnp.sum(ge.reshape(ck // 8, 8, tq), axis=0)
        acc = lax.fori_loop(0, i + 1, body, jnp.zeros((8, tq), F32))
        return jnp.sum(acc, axis=0, keepdims=True)

    thr = _kth_largest_key(count_ge, (1, tq), topk)
    need = topk - count_ge(thr + 1)
    incl_prefix = (lax.broadcasted_iota(I32, (ck, ck), 1) <= lax.broadcasted_iota(I32, (ck, ck), 0)).astype(BF16)

    def bias_body(kc, seen):
        key = key_ref[kc]
        tied = key == thr
        tied_f = jnp.where(tied, 1.0, 0.0)
        rank = seen + jnp.dot(incl_prefix, tied_f.astype(BF16), preferred_element_type=F32)
        sel = jnp.logical_or(key > thr, jnp.logical_and(tied, rank <= need))
        bias_ref[kc] = jnp.where(jnp.logical_and(sel, causal_ok(kc)), 0.0, NEG).T
        return seen + jnp.sum(jnp.sum(tied_f.reshape(ck // 8, 8, tq), axis=0), axis=0, keepdims=True)

    lax.fori_loop(0, i + 1, bias_body, jnp.zeros((1, tq), F32))

    for n in range(nkv):
        q3_ref[n] = jnp.concatenate(
            [q_ref[:, (n * GROUP + g) * HEAD_DIM:(n * GROUP + g + 1) * HEAD_DIM] for g in range(GROUP)], axis=0)

    def chunk(ref, kc, n):
        return ref[0, pl.ds(pl.multiple_of(kc * ck, ck), ck), n * HEAD_DIM:(n + 1) * HEAD_DIM]

    def chunk_logits(kc):
        bias = jnp.concatenate([bias_ref[kc]] * GROUP, axis=0)

        def of_head(n):
            s = lax.dot_general(q3_ref[n], chunk(k_ref, kc, n), _CONTRACT_LAST, preferred_element_type=F32)
            return s * SOFTMAX_C2 + bias
        return of_head

    _masked_attention(nkv, i + 1, chunk_logits, functools.partial(chunk, v_ref), mx_ref, lsum_ref, acc_ref)
    _write_heads(o_ref, lsum_ref, acc_ref, nkv, tq)


def _dsa_prompt(qi, wi, ki, q, k, v, bsz, t, nkv, ih, topk):
    nt = t // QTILE
    qw = nkv * GROUP * HEAD_DIM
    kvw = nkv * HEAD_DIM
    rows = lambda w: pl.BlockSpec((QTILE, w), lambda b, i: (b * nt + i, 0))
    seq = lambda w: pl.BlockSpec((1, t, w), lambda b, i: (b, 0, 0))
    return pl.pallas_call(
        functools.partial(_dsa_prompt_kernel, topk=topk, nkv=nkv, ih=ih),
        grid=(bsz, nt),
        in_specs=[rows(ih * IDX_DIM), rows(128), seq(IDX_DIM), rows(qw), seq(kvw), seq(kvw)],
        out_specs=rows(qw),
        out_shape=jax.ShapeDtypeStruct((bsz * t, qw), BF16),
        scratch_shapes=[pltpu.VMEM((nt, QTILE, QTILE), I32), pltpu.VMEM((nt, QTILE, QTILE), F32),
                        pltpu.VMEM((nkv, GROUP * QTILE, HEAD_DIM), BF16)] + _attn_state_scratch(nkv),
        compiler_params=_cparams("parallel", "arbitrary"),
        name="dsa_prompt_attn",
    )(qi, wi, ki.reshape(bsz, t, IDX_DIM), q, k.reshape(bsz, t, kvw), v.reshape(bsz, t, kvw))


def _topb_select_bias(gate, valid, blk_iota, nblk):
    g = jnp.where(valid, gate, -jnp.inf)
    rank = jnp.zeros(g.shape, I32)
    for m in range(nblk):
        gm = g[:, m:m + 1]
        beats = jnp.logical_or(gm > g, jnp.logical_and(gm == g, blk_iota > m))
        rank = rank + beats.astype(I32)
    sel = jnp.logical_and(valid, rank < MOBA_TOPB)
    return jnp.where(sel, 0.0, NEG)


def _topb_select_rows(gate_t, valid_t, nblk):
    g = jnp.where(valid_t, gate_t, -jnp.inf)
    blk = lax.broadcasted_iota(I32, g.shape, 0)
    rank = jnp.zeros(g.shape, I32)
    for m in range(nblk):
        gm = g[m:m + 1, :]
        beats = jnp.logical_or(gm > g, jnp.logical_and(gm == g, blk > m))
        rank = rank + beats.astype(I32)
    return jnp.logical_and(valid_t, rank < MOBA_TOPB)


def _moba_prompt_kernel(q_ref, k_ref, v_ref, km_ref, o_ref, qa_ref, so_ref, mx_ref, lsum_ref, acc_ref, *, nkv, nblk):
    j = pl.program_id(1)
    tq = blk = QTILE
    rows = GROUP * tq
    row = lax.broadcasted_iota(I32, (tq, blk), 0)
    col = lax.broadcasted_iota(I32, (tq, blk), 1)
    cb = jnp.where(col <= row, 0.0, NEG)
    causal_bias = jnp.concatenate([cb] * GROUP, axis=0)
    valid_t = lax.broadcasted_iota(I32, (nblk, rows), 0) < j
    lane = lax.broadcasted_iota(I32, (blk, HEAD_DIM), 1)
    km_pad = jnp.zeros((16 - nblk % 16, HEAD_DIM), F32)

    def chunk(ref, m, n):
        return ref[0, pl.ds(pl.multiple_of(m * blk, blk), blk), n * HEAD_DIM:(n + 1) * HEAD_DIM]

    for n in range(nkv):
        lanes = slice(n * HEAD_DIM, (n + 1) * HEAD_DIM)
        q3 = jnp.concatenate(
            [q_ref[:, (n * GROUP + g) * HEAD_DIM:(n * GROUP + g + 1) * HEAD_DIM] for g in range(GROUP)], axis=0)
        km = jnp.concatenate([km_ref[0, :, lanes], km_pad], axis=0).astype(BF16)
        gate_t = lax.dot_general(km, q3, _CONTRACT_LAST, preferred_element_type=F32)[:nblk]
        selb_t = jnp.where(_topb_select_rows(gate_t, valid_t, nblk), 0.0, NEG)
        selb = jnp.concatenate([selb_t, jnp.zeros((HEAD_DIM - nblk, rows), F32)], axis=0).T
        qa_ref[n] = jnp.concatenate([q3, selb.astype(BF16)], axis=1)
        own_s = lax.dot_general(q3, chunk(k_ref, j, n), _CONTRACT_LAST, preferred_element_type=F32)
        so_ref[n] = own_s * SOFTMAX_C2 + causal_bias

    def chunk_logits(m):
        onehot = jnp.where(lane == m, 1.0, 0.0).astype(BF16)

        def of_head(n):
            k_aug = jnp.concatenate([chunk(k_ref, m, n), onehot], axis=1)
            return lax.dot_general(qa_ref[n], k_aug, _CONTRACT_LAST, preferred_element_type=F32) * SOFTMAX_C2
        return of_head

    _masked_attention(nkv, j, chunk_logits, functools.partial(chunk, v_ref), mx_ref, lsum_ref, acc_ref,
                      own=(so_ref, lambda n: chunk(v_ref, j, n)))
    _write_heads(o_ref, lsum_ref, acc_ref, nkv, tq)


def _moba_prompt(q, k, v, kmean, bsz, t, nkv):
    nt = t // QTILE
    qw = nkv * GROUP * HEAD_DIM
    kvw = nkv * HEAD_DIM
    rows = lambda w: pl.BlockSpec((QTILE, w), lambda b, i: (b * nt + i, 0))
    seq = lambda w: pl.BlockSpec((1, t, w), lambda b, i: (b, 0, 0))
    return pl.pallas_call(
        functools.partial(_moba_prompt_kernel, nkv=nkv, nblk=nt),
        grid=(bsz, nt),
        in_specs=[rows(qw), seq(kvw), seq(kvw), pl.BlockSpec((1, nt, kvw), lambda b, i: (b, 0, 0))],
        out_specs=rows(qw),
        out_shape=jax.ShapeDtypeStruct((bsz * t, qw), BF16),
        scratch_shapes=[pltpu.VMEM((nkv, GROUP * QTILE, 2 * HEAD_DIM), BF16),
                        pltpu.VMEM((nkv, GROUP * QTILE, QTILE), F32)] + _attn_state_scratch(nkv),
        compiler_params=_cparams("parallel", "arbitrary"),
        name="moba_prompt_attn",
    )(q, k.reshape(bsz, t, kvw), v.reshape(bsz, t, kvw), kmean.reshape(bsz, nt, kvw))


def _mem_attn_kernel(q_ref, mk_ref, mv_ref, o_ref, *, hd):
    scale = hd ** -0.5
    for c in range(MEM_HEADS):
        lanes = slice(c * hd, (c + 1) * hd)
        q = q_ref[0, :, lanes].astype(BF16)
        mk = mk_ref[0, :, lanes].astype(BF16)
        mv = mv_ref[0, :, lanes].astype(BF16)
        s = lax.dot_general(q, mk, _CONTRACT_LAST, preferred_element_type=F32) * scale
        p = jnp.exp(s - jnp.max(s, axis=1, keepdims=True))
        l = jnp.sum(p, axis=1, keepdims=True)
        o = jnp.dot(p.astype(BF16), mv, preferred_element_type=F32) / l
        o_ref[0, :, lanes] = o.astype(o_ref.dtype)


def _mem_attn(qm, mk, mv, out_dtype):
    bsz, t, w = qm.shape
    mlen = mk.shape[1]
    tq = _pick_block(t, 512, 8)
    return pl.pallas_call(
        functools.partial(_mem_attn_kernel, hd=w // MEM_HEADS),
        grid=(bsz, t // tq),
        in_specs=[pl.BlockSpec((1, tq, w), lambda b, i: (b, i, 0)),
                  pl.BlockSpec((1, mlen, w), lambda b, i: (b, 0, 0)),
                  pl.BlockSpec((1, mlen, w), lambda b, i: (b, 0, 0))],
        out_specs=pl.BlockSpec((1, tq, w), lambda b, i: (b, i, 0)),
        out_shape=jax.ShapeDtypeStruct((bsz, t, w), out_dtype),
        compiler_params=_cparams("parallel", "parallel"),
        name="mem_attn",
    )(qm, mk, mv)


SAMPLE_PP = 16


def _page_specs(shape_tail, pp, layer):
    nd = len(shape_tail)
    return [pl.BlockSpec((None, None) + shape_tail,
                         lambda b, p, pt, c=c: (layer, pt[b, p * pp + c]) + (0,) * nd)
            for c in range(pp)]


def _head_sum(w, ih):
    acc = w[0:SAMPLE_TPAD]
    for h in range(1, ih):
        acc = acc + w[h * SAMPLE_TPAD:(h + 1) * SAMPLE_TPAD]
    return acc


def _dsa_sample_index_kernel(pt_ref, qi_ref, wi_ref, *refs, pp, ih, page):
    page_refs, o_ref = refs[:pp], refs[pp]
    qi = qi_ref[0].astype(BF16)
    wi = wi_ref[0]
    for c in range(pp):
        kp = page_refs[c][...].astype(BF16)
        s = lax.dot_general(qi, kp, _CONTRACT_LAST, preferred_element_type=F32)
        o_ref[0, :, c * page:(c + 1) * page] = _head_sum(jnp.maximum(s, 0.0) * wi, ih)


def _dsa_sample_index(page_table, qi_ht, wi_ht, kidx_pool, layer, ih):
    bs, npg = page_table.shape
    page = kidx_pool.shape[2]
    pp = SAMPLE_PP
    r = ih * SAMPLE_TPAD
    grid_spec = pltpu.PrefetchScalarGridSpec(
        num_scalar_prefetch=1,
        grid=(bs, npg // pp),
        in_specs=[pl.BlockSpec((1, r, IDX_DIM), lambda b, p, pt: (b, 0, 0)),
                  pl.BlockSpec((1, r, page), lambda b, p, pt: (b, 0, 0))]
        + _page_specs((page, IDX_DIM), pp, layer),
        out_specs=pl.BlockSpec((1, SAMPLE_TPAD, page * pp), lambda b, p, pt: (b, 0, p)),
    )
    return pl.pallas_call(
        functools.partial(_dsa_sample_index_kernel, pp=pp, ih=ih, page=page),
        grid_spec=grid_spec,
        out_shape=jax.ShapeDtypeStruct((bs, SAMPLE_TPAD, npg * page), F32),
        compiler_params=_cparams("parallel", "arbitrary"),
        name="dsa_sample_index",
    )(page_table, qi_ht, wi_ht, *([kidx_pool] * pp))


def _dsa_sample_select_kernel(sc_ref, qi_ref, wi_ref, kin_ref, o_ref, *, topk, ih, ts, past):
    tail_w = kin_ref.shape[1]
    s = lax.dot_general(qi_ref[0].astype(BF16), kin_ref[0].astype(BF16), _CONTRACT_LAST,
                        preferred_element_type=F32)
    tail = _head_sum(jnp.maximum(s, 0.0) * wi_ref[0][:, :tail_w], ih)
    row = lax.broadcasted_iota(I32, (SAMPLE_TPAD, tail_w), 0)
    col = lax.broadcasted_iota(I32, (SAMPLE_TPAD, tail_w), 1)
    tail_ok = jnp.logical_and(col <= row, col < ts)
    full = jnp.concatenate([sc_ref[0], jnp.where(tail_ok, tail, -jnp.inf)], axis=1)
    keys = _sortable_key(full)
    width = past + tail_w
    col_f = lax.broadcasted_iota(I32, (SAMPLE_TPAD, width), 1)
    row_f = lax.broadcasted_iota(I32, (SAMPLE_TPAD, width), 0)
    visible = jnp.logical_or(col_f < past, jnp.logical_and(col_f - past <= row_f, col_f - past < ts))

    def count_ge(cand):
        return jnp.sum((keys >= cand).astype(F32), axis=1, keepdims=True)

    thr = _kth_largest_key(count_ge, (SAMPLE_TPAD, 1), topk)
    need = topk - count_ge(thr + 1)
    tied = keys == thr
    incl_prefix = (lax.broadcasted_iota(I32, (128, 128), 0) <= lax.broadcasted_iota(I32, (128, 128), 1)).astype(BF16)
    seen = jnp.zeros((SAMPLE_TPAD, 1), F32)
    ranks = []
    for g in range(width // 128):
        tied_g = jnp.where(tied[:, g * 128:(g + 1) * 128], 1.0, 0.0).astype(BF16)
        pref = jnp.dot(tied_g, incl_prefix, preferred_element_type=F32)
        ranks.append(seen + pref)
        seen = seen + pref[:, 127:128]
    rank = jnp.concatenate(ranks, axis=1)
    sel = jnp.logical_or(keys > thr, jnp.logical_and(tied, rank <= need))
    o_ref[0] = jnp.where(jnp.logical_and(sel, visible), 0.0, NEG)


def _dsa_sample_select(scores, qi_ht, wi_ht, ki_new, ih, ts, topk):
    bs, _, past = scores.shape
    tail_w = ki_new.shape[1]
    r = ih * SAMPLE_TPAD
    return pl.pallas_call(
        functools.partial(_dsa_sample_select_kernel, topk=topk, ih=ih, ts=ts, past=past),
        grid=(bs,),
        in_specs=[pl.BlockSpec((1, SAMPLE_TPAD, past), lambda b: (b, 0, 0)),
                  pl.BlockSpec((1, r, IDX_DIM), lambda b: (b, 0, 0)),
                  pl.BlockSpec((1, r, wi_ht.shape[2]), lambda b: (b, 0, 0)),
                  pl.BlockSpec((1, tail_w, IDX_DIM), lambda b: (b, 0, 0))],
        out_specs=pl.BlockSpec((1, SAMPLE_TPAD, past + tail_w), lambda b: (b, 0, 0)),
        out_shape=jax.ShapeDtypeStruct((bs, SAMPLE_TPAD, past + tail_w), F32),
        compiler_params=_cparams("parallel"),
        name="dsa_sample_select",
    )(scores, qi_ht, wi_ht, ki_new)


def _sample_flash_update(qbd, kblk, vblk, bias, m_ref, l_ref, acc_ref, nkv):
    rows_per_head = GROUP * SAMPLE_TPAD
    s = lax.dot_general(qbd, kblk, _CONTRACT_LAST, preferred_element_type=F32) * (HEAD_DIM ** -0.5) + bias
    m_old = m_ref[...]
    m_new = jnp.maximum(m_old, jnp.max(s, axis=1, keepdims=True))
    alpha = jnp.exp(m_old - m_new)
    p = jnp.exp(s - m_new)
    l_ref[...] = alpha * l_ref[...] + jnp.sum(p, axis=1, keepdims=True)
    m_ref[...] = m_new
    o_full = jnp.dot(p.astype(BF16), vblk, preferred_element_type=F32)
    o_diag = jnp.concatenate(
        [o_full[n * rows_per_head:(n + 1) * rows_per_head, n * HEAD_DIM:(n + 1) * HEAD_DIM] for n in range(nkv)],
        axis=0)
    acc_ref[...] = alpha * acc_ref[...] + o_diag


def _sample_flash_reset(m_ref, l_ref, acc_ref):
    m_ref[...] = jnp.full(m_ref.shape, NEG, F32)
    l_ref[...] = jnp.zeros(l_ref.shape, F32)
    acc_ref[...] = jnp.zeros(acc_ref.shape, F32)


def _heads_on_lanes(refs, nkv):
    keys = refs[0].shape[0] // nkv
    pages = [jnp.concatenate([r[pl.ds(n, keys, stride=nkv), :] for n in range(nkv)], axis=1).astype(BF16)
             for r in refs]
    return jnp.concatenate(pages, axis=0)


def _flat_pool(pool):
    return pool.reshape(pool.shape[0], pool.shape[1], pool.shape[2] * pool.shape[3], pool.shape[4])


def _tile_query_bias(b8, reps):
    return jnp.concatenate([b8] * reps, axis=0)


def _dsa_sample_attn_kernel(pt_ref, q_ref, bias_ref, tbias_ref, kn_ref, vn_ref, *refs, pp, nkv):
    k_refs, v_refs = refs[:pp], refs[pp:2 * pp]
    o_ref, m_ref, l_ref, acc_ref = refs[2 * pp:]
    p = pl.program_id(1)
    reps = nkv * GROUP

    @pl.when(p == 0)
    def _():
        _sample_flash_reset(m_ref, l_ref, acc_ref)

    _sample_flash_update(q_ref[0], _heads_on_lanes(k_refs, nkv), _heads_on_lanes(v_refs, nkv),
                         _tile_query_bias(bias_ref[0], reps), m_ref, l_ref, acc_ref, nkv)

    @pl.when(p == pl.num_programs(1) - 1)
    def _():
        _sample_flash_update(q_ref[0], kn_ref[0].astype(BF16), vn_ref[0].astype(BF16),
                             _tile_query_bias(tbias_ref[0], reps), m_ref, l_ref, acc_ref, nkv)
        o_ref[0] = acc_ref[...] / l_ref[...]


def _sample_attn_specs(nkv, tail_rows):
    rows = nkv * GROUP * SAMPLE_TPAD
    kvw = nkv * HEAD_DIM
    q_spec = pl.BlockSpec((1, rows, kvw), lambda b, p, pt: (b, 0, 0))
    tail_spec = pl.BlockSpec((1, tail_rows, kvw), lambda b, p, pt: (b, 0, 0))
    out_spec = pl.BlockSpec((1, rows, HEAD_DIM), lambda b, p, pt: (b, 0, 0))
    scratch = [pltpu.VMEM((rows, 1), F32), pltpu.VMEM((rows, 1), F32), pltpu.VMEM((rows, HEAD_DIM), F32)]
    return rows, q_spec, tail_spec, out_spec, scratch


def _dsa_sample_attn(page_table, qbd, bias, k_new, v_new, k_pool, v_pool, layer, nkv):
    bs, npg = page_table.shape
    page = k_pool.shape[2]
    pp = SAMPLE_PP
    rows, q_spec, tail_spec, out_spec, scratch = _sample_attn_specs(nkv, k_new.shape[1])
    grid_spec = pltpu.PrefetchScalarGridSpec(
        num_scalar_prefetch=1,
        grid=(bs, npg // pp),
        in_specs=[q_spec,
                  pl.BlockSpec((1, SAMPLE_TPAD, page * pp), lambda b, p, pt: (b, 0, p)),
                  pl.BlockSpec((1, SAMPLE_TPAD, page), lambda b, p, pt: (b, 0, npg)),
                  tail_spec, tail_spec]
        + _page_specs((page * nkv, HEAD_DIM), pp, layer) + _page_specs((page * nkv, HEAD_DIM), pp, layer),
        out_specs=out_spec,
        scratch_shapes=scratch,
    )
    return pl.pallas_call(
        functools.partial(_dsa_sample_attn_kernel, pp=pp, nkv=nkv),
        grid_spec=grid_spec,
        out_shape=jax.ShapeDtypeStruct((bs, rows, HEAD_DIM), F32),
        compiler_params=_cparams("parallel", "arbitrary"),
        name="dsa_sample_attn",
    )(page_table, qbd, bias, bias, k_new, v_new, *([_flat_pool(k_pool)] * pp), *([_flat_pool(v_pool)] * pp))


KMEAN_BLOCKS = 8


def _moba_kmean_kernel(pt_ref, *refs, nblk, ppb):
    k_refs, o_ref = refs[:nblk * ppb], refs[nblk * ppb]
    for blk in range(nblk):
        acc = jnp.sum(k_refs[blk * ppb][...], axis=0)
        for c in range(1, ppb):
            acc = acc + jnp.sum(k_refs[blk * ppb + c][...], axis=0)
        o_ref[0, blk] = acc * (1.0 / MOBA_BLOCK)


def _moba_sample_kmean(page_table, k_pool, layer, nkv):
    bs, npg = page_table.shape
    page = k_pool.shape[2]
    ppb = MOBA_BLOCK // page
    nb = npg // ppb
    nblk = KMEAN_BLOCKS
    grid_spec = pltpu.PrefetchScalarGridSpec(
        num_scalar_prefetch=1,
        grid=(bs, nb // nblk),
        in_specs=_page_specs((page, nkv, HEAD_DIM), nblk * ppb, layer),
        out_specs=pl.BlockSpec((1, nblk, nkv, HEAD_DIM), lambda b, p, pt: (b, p, 0, 0)),
    )
    return pl.pallas_call(
        functools.partial(_moba_kmean_kernel, nblk=nblk, ppb=ppb),
        grid_spec=grid_spec,
        out_shape=jax.ShapeDtypeStruct((bs, nb, nkv, HEAD_DIM), F32),
        compiler_params=_cparams("parallel", "arbitrary"),
        name="moba_sample_kmean",
    )(page_table, *([k_pool] * (nblk * ppb)))


def _moba_sample_attn_kernel(pt_ref, q_ref, km_ref, kn_ref, vn_ref, *refs, pp, nkv, page, nb, ts):
    k_refs, v_refs = refs[:pp], refs[pp:2 * pp]
    o_ref, m_ref, l_ref, acc_ref, sb_ref = refs[2 * pp:]
    p = pl.program_id(1)
    rows = nkv * GROUP * SAMPLE_TPAD
    ppb = MOBA_BLOCK // page
    blk_iota = lax.broadcasted_iota(I32, (rows, nb), 1)

    @pl.when(p == 0)
    def _():
        _sample_flash_reset(m_ref, l_ref, acc_ref)
        valid = blk_iota >= 0
        gate = lax.dot_general(q_ref[0], km_ref[0].astype(BF16), _CONTRACT_LAST, preferred_element_type=F32)
        sb_ref[...] = _topb_select_bias(gate, valid, blk_iota, nb)

    selb = sb_ref[...]
    cols = []
    for c in range(pp // ppb):
        blk = p * (pp // ppb) + c
        col = jnp.sum(jnp.where(blk_iota == blk, selb, 0.0), axis=1, keepdims=True)
        cols.append(jnp.broadcast_to(col, (rows, MOBA_BLOCK)))
    _sample_flash_update(q_ref[0], _heads_on_lanes(k_refs, nkv), _heads_on_lanes(v_refs, nkv),
                         jnp.concatenate(cols, axis=1), m_ref, l_ref, acc_ref, nkv)

    @pl.when(p == pl.num_programs(1) - 1)
    def _():
        tail_w = kn_ref.shape[1]
        row = lax.broadcasted_iota(I32, (rows, tail_w), 0) & (SAMPLE_TPAD - 1)
        col = lax.broadcasted_iota(I32, (rows, tail_w), 1)
        tb = jnp.where(jnp.logical_and(col <= row, col < ts), 0.0, NEG)
        _sample_flash_update(q_ref[0], kn_ref[0].astype(BF16), vn_ref[0].astype(BF16), tb,
                             m_ref, l_ref, acc_ref, nkv)
        o_ref[0] = acc_ref[...] / l_ref[...]


def _moba_sample_attn(page_table, qbd, kmean, k_new, v_new, k_pool, v_pool, layer, nkv, ts):
    bs, npg = page_table.shape
    page = k_pool.shape[2]
    kvw = nkv * HEAD_DIM
    pp = SAMPLE_PP
    nb = kmean.shape[1]
    rows, q_spec, tail_spec, out_spec, scratch = _sample_attn_specs(nkv, k_new.shape[1])
    grid_spec = pltpu.PrefetchScalarGridSpec(
        num_scalar_prefetch=1,
        grid=(bs, npg // pp),
        in_specs=[q_spec, pl.BlockSpec((1, nb, kvw), lambda b, p, pt: (b, 0, 0)), tail_spec, tail_spec]
        + _page_specs((page * nkv, HEAD_DIM), pp, layer) + _page_specs((page * nkv, HEAD_DIM), pp, layer),
        out_specs=out_spec,
        scratch_shapes=scratch + [pltpu.VMEM((rows, nb), F32)],
    )
    return pl.pallas_call(
        functools.partial(_moba_sample_attn_kernel, pp=pp, nkv=nkv, page=page, nb=nb, ts=ts),
        grid_spec=grid_spec,
        out_shape=jax.ShapeDtypeStruct((bs, rows, HEAD_DIM), F32),
        compiler_params=_cparams("parallel", "arbitrary"),
        name="moba_sample_attn",
    )(page_table, qbd, kmean, k_new, v_new, *([_flat_pool(k_pool)] * pp), *([_flat_pool(v_pool)] * pp))


def _heads_to_kv_major(q, bs, nkv):
    x = q.reshape(bs, SAMPLE_TPAD, nkv, GROUP, HEAD_DIM).transpose(0, 2, 3, 1, 4)
    eye = jnp.eye(nkv, dtype=q.dtype)
    bd = x[:, :, :, :, None, :] * eye[None, :, None, None, :, None]
    return bd.reshape(bs, nkv * GROUP * SAMPLE_TPAD, nkv * HEAD_DIM).astype(BF16)


def _kv_major_to_rows(o, bs, nkv):
    x = o.reshape(bs, nkv, GROUP, SAMPLE_TPAD, HEAD_DIM)
    return x.transpose(0, 3, 1, 2, 4).reshape(bs * SAMPLE_TPAD, nkv * GROUP * HEAD_DIM)


def _pad_rows(x, n):
    return jnp.pad(x, ((0, 0), (0, n - x.shape[1]), (0, 0)))


def _merge_mixers(h, mix, qm, mk, mv, bsz):
    m = h.shape[0]
    ma = _mem_attn(qm.reshape(bsz, m // bsz, -1), mk, mv, qm.dtype).reshape(m, -1)
    return mix.astype(BF16), ma.astype(BF16)


def _finish_layer(hp, hs, merged_p, merged_s, layer, w_o, g_ffn, w_up, w_down):
    hp = _matmul(merged_p, w_o, layer, F32, epilogue="residual", residual=hp)
    hs = _matmul(merged_s, w_o, layer, F32, epilogue="residual", residual=hs)
    up, us = _matmul(_rmsnorm(hp, g_ffn, BF16), w_up, layer, BF16, epilogue="relu2", a2=_rmsnorm(hs, g_ffn, BF16))
    hp = _matmul(up, w_down, layer, F32, epilogue="residual", residual=hp)
    hs = _matmul(us, w_down, layer, F32, epilogue="residual", residual=hs)
    return hp, hs


def kernel(x_prompt, x_sample, cache_dsa_k, cache_dsa_v, cache_dsa_kidx, cache_moba_k, cache_moba_v, cache_mem_k, cache_mem_v, page_table, mem_prompt, norm_mix, norm_mem, w_in_dsa, w_in_moba, w_mem_kv, w_out, norm_ffn, w_up, w_down, norm_final):
    bsz, t, d = x_prompt.shape
    bs, ts, _ = x_sample.shape
    depth = norm_mix.shape[0]
    nh = (3 * d) // (4 * HEAD_DIM)
    nkv = nh // GROUP
    qw, kvw = nh * HEAD_DIM, nkv * HEAD_DIM
    memw = d // 4
    mhd = memw // MEM_HEADS
    ih = d // 128
    npg = page_table.shape[1]
    page = cache_dsa_k.shape[2]
    past = npg * page
    mlen = mem_prompt.shape[1]
    tp = SAMPLE_TPAD

    hp = x_prompt.reshape(bsz * t, d)
    hs = jnp.pad(x_sample, ((0, 0), (0, tp - ts), (0, 0))).reshape(bs * tp, d)
    pos_p = jnp.arange(t, dtype=I32)
    pos_s = past + (jnp.arange(tp, dtype=I32) % ts)
    tabs_p = _rope_tables(pos_p)
    tabs_s = _rope_tables(jnp.tile(pos_s, bs))

    outs = {name: [] for name in ("pdk", "pdv", "pdki", "sdk", "sdv", "sdki", "pmk", "pmv", "smk", "smv", "mk", "mv")}
    mem_flat = mem_prompt.reshape(bsz * mlen, d)

    w_o, wu, wd, w_mem = w_out.astype(BF16), w_up.astype(BF16), w_down.astype(BF16), w_mem_kv.astype(BF16)
    w_dsa, w_moba = w_in_dsa.astype(BF16), w_in_moba.astype(BF16)
    c0 = qw + 2 * kvw + ih * IDX_DIM
    w_dsa_tail = jnp.concatenate(
        [w_in_dsa[:, :, c0 + ih:c0 + ih + IDX_DIM], w_in_dsa[:, :, c0 + ih + IDX_DIM:], w_in_dsa[:, :, c0:c0 + ih],
         jnp.zeros(w_in_dsa.shape[:2] + (128 - ih,), F32)], axis=2).astype(BF16)

    for i in range(depth):
        j = i // 2
        mkv = _matmul(_rmsnorm(mem_flat, norm_mem[i], BF16), w_mem, i, F32)
        mk_p = mkv[:, :memw].reshape(bsz, mlen, memw)
        mv_p = mkv[:, memw:].reshape(bsz, mlen, memw)
        outs["mk"].append(mk_p.reshape(bsz, mlen, MEM_HEADS, mhd))
        outs["mv"].append(mv_p.reshape(bsz, mlen, MEM_HEADS, mhd))
        hn_p = _rmsnorm(hp, norm_mix[i], BF16)
        hn_s = _rmsnorm(hs, norm_mix[i], BF16)
        mk_s = cache_mem_k[i].reshape(bs, mlen, memw)
        mv_s = cache_mem_v[i].reshape(bs, mlen, memw)
        if i % 2 == 0:
            pa, pa_s = _matmul(hn_p, w_dsa, j, F32, n=c0, a2=hn_s)
            pb, pb_s = _matmul(hn_p, w_dsa_tail, j, F32, a2=hn_s)
            q, kf, kb, vf, vb, qi, kif, kib, wi, qm = _dsa_split(pa, pb, tabs_p, t, nh, nkv, ih, memw, BF16)
            outs["pdk"].append(kf.reshape(bsz, t, nkv, HEAD_DIM))
            outs["pdv"].append(vf.reshape(bsz, t, nkv, HEAD_DIM))
            outs["pdki"].append(kif.reshape(bsz, t, IDX_DIM))
            mix_p = _dsa_prompt(qi, wi, kib, q, kb, vb, bsz, t, nkv, ih, min(DSA_TOPK, t // 4))
            q_s, kf, _, vf, _, qi_s, kif, _, wi_s, qm_s = _dsa_split(pa_s, pb_s, tabs_s, bs * tp, nh, nkv, ih, memw, F32)
            outs["sdk"].append(kf.reshape(bs, tp, nkv, HEAD_DIM)[:, :ts])
            outs["sdv"].append(vf.reshape(bs, tp, nkv, HEAD_DIM)[:, :ts])
            outs["sdki"].append(kif.reshape(bs, tp, IDX_DIM)[:, :ts])
            qi_ht = qi_s.reshape(bs, tp, ih, IDX_DIM).transpose(0, 2, 1, 3).reshape(bs, ih * tp, IDX_DIM)
            wi_ht = wi_s.reshape(bs, tp, 128)[:, :, :ih].transpose(0, 2, 1).reshape(bs, ih * tp, 1)
            wi_ht = jnp.broadcast_to(wi_ht, (bs, ih * tp, page))
            scores = _dsa_sample_index(page_table, qi_ht, wi_ht, cache_dsa_kidx, j, ih)
            ki_new = _pad_rows(kif.reshape(bs, tp, IDX_DIM), page)
            bias = _dsa_sample_select(scores, qi_ht, wi_ht, ki_new, ih, ts, min(DSA_TOPK, (past + ts) // 4))
            o_s = _dsa_sample_attn(page_table, _heads_to_kv_major(q_s, bs, nkv), bias,
                                   _pad_rows(kf.reshape(bs, tp, kvw), page), _pad_rows(vf.reshape(bs, tp, kvw), page),
                                   cache_dsa_k, cache_dsa_v, j, nkv)
            mix_s = _kv_major_to_rows(o_s, bs, nkv)
        else:
            p, p_s = _matmul(hn_p, w_moba, j, F32, a2=hn_s)
            q, kf, kb, vf, vb, qm, kmean = _moba_split(p, tabs_p, t, nh, nkv, memw, BF16)
            outs["pmk"].append(kf.reshape(bsz, t, nkv, HEAD_DIM))
            outs["pmv"].append(vf.reshape(bsz, t, nkv, HEAD_DIM))
            mix_p = _moba_prompt(q, kb, vb, kmean, bsz, t, nkv)
            q_s, kf, _, vf, _, qm_s, _ = _moba_split(p_s, tabs_s, bs * tp, nh, nkv, memw, F32)
            outs["smk"].append(kf.reshape(bs, tp, nkv, HEAD_DIM)[:, :ts])
            outs["smv"].append(vf.reshape(bs, tp, nkv, HEAD_DIM)[:, :ts])
            kmean_s = _moba_sample_kmean(page_table, cache_moba_k, j, nkv)
            kmean_s = kmean_s.reshape(bs, kmean_s.shape[1], kvw)
            o_s = _moba_sample_attn(page_table, _heads_to_kv_major(q_s, bs, nkv), kmean_s,
                                    _pad_rows(kf.reshape(bs, tp, kvw), page), _pad_rows(vf.reshape(bs, tp, kvw), page),
                                    cache_moba_k, cache_moba_v, j, nkv, ts)
            mix_s = _kv_major_to_rows(o_s, bs, nkv)
        hp, hs = _finish_layer(hp, hs, _merge_mixers(hp, mix_p, qm, mk_p, mv_p, bsz),
                               _merge_mixers(hs, mix_s, qm_s, mk_s, mv_s, bs), i, w_o, norm_ffn[i], wu, wd)

    y_prompt = _rmsnorm(hp, norm_final, F32).reshape(bsz, t, d)
    y_sample = _rmsnorm(hs, norm_final, F32).reshape(bs, tp, d)[:, :ts]
    st = jnp.stack
    return (y_prompt, y_sample, st(outs["pdk"]), st(outs["pdv"]), st(outs["pdki"]), st(outs["pmk"]), st(outs["pmv"]),
            st(outs["mk"]), st(outs["mv"]), st(outs["sdk"]), st(outs["sdv"]), st(outs["sdki"]),
            st(outs["smk"]), st(outs["smv"]))
```

```python
import functools
import math

import jax
import jax.numpy as jnp
from jax import lax
from jax.experimental import pallas as pl
from jax.experimental.pallas import tpu as pltpu

F32 = jnp.float32
BF16 = jnp.bfloat16
I32 = jnp.int32

HEAD_DIM = 128
GROUP = 3
ROT_DIM = 32
ROPE_THETA = 500000.0
IDX_DIM = 128
MEM_HEADS = 4
DSA_TOPK = 256
MOBA_BLOCK = 256
MOBA_TOPB = 3
EPS = 1e-6
NEG = -1e30
LOG2E = 1.4426950408889634
SOFTMAX_C2 = (HEAD_DIM ** -0.5) * LOG2E
QTILE = 256
SAMPLE_TPAD = 8
VMEM_LIMIT = 60 * 1024 * 1024

_CONTRACT_LAST = (((1,), (1,)), ((), ()))


def _cparams(*sem):
    return pltpu.CompilerParams(dimension_semantics=sem, vmem_limit_bytes=VMEM_LIMIT)


def _pick_block(n, pref, align=128):
    if n <= pref:
        return n
    b = (pref // align) * align
    while b >= align:
        if n % b == 0:
            return b
        b -= align
    return n


def _rmsnorm_kernel(x_ref, g_ref, o_ref):
    x = x_ref[...]
    ms = jnp.mean(x * x, axis=-1, keepdims=True)
    o_ref[...] = ((x * lax.rsqrt(ms + EPS)) * g_ref[...]).astype(o_ref.dtype)


def _rmsnorm(x, g, out_dtype):
    m, d = x.shape
    bm = _pick_block(m, 256, 8)
    return pl.pallas_call(
        _rmsnorm_kernel,
        grid=(m // bm,),
        in_specs=[pl.BlockSpec((bm, d), lambda i: (i, 0)), pl.BlockSpec((1, d), lambda i: (0, 0))],
        out_specs=pl.BlockSpec((bm, d), lambda i: (i, 0)),
        out_shape=jax.ShapeDtypeStruct((m, d), out_dtype),
        compiler_params=_cparams("parallel"),
        name="rmsnorm",
    )(x, g.reshape(1, d).astype(F32))


def _mm_tile(a_refs, w, r_ref, o_ref, k, nk, epilogue, tab_refs=None):
    part, off = None, 0
    for a_ref in a_refs:
        ka = a_ref.shape[1]
        d = jnp.dot(a_ref[...], w[off:off + ka], preferred_element_type=F32)
        part = d if part is None else part + d
        off += ka

    def finish(acc):
        if epilogue == "rope":
            c, sm, sp = (t[...] for t in tab_refs)
            for h in range(acc.shape[1] // HEAD_DIM):
                sl = slice(h * HEAD_DIM, (h + 1) * HEAD_DIM)
                o_ref[:, sl] = _rope_head(acc[:, sl], c, sm, sp).astype(o_ref.dtype)
            return
        if epilogue == "relu2":
            r = jnp.maximum(acc, 0.0)
            acc = r * r
        elif epilogue == "residual":
            acc = r_ref[...] + acc
        o_ref[...] = acc.astype(o_ref.dtype)

    if nk == 1:
        finish(part)
        return

    @pl.when(k == 0)
    def _():
        o_ref[...] = part

    @pl.when(jnp.logical_and(k > 0, k < nk - 1))
    def _():
        o_ref[...] += part

    @pl.when(k == nk - 1)
    def _():
        finish(o_ref[...] + part)


def _mm_kernel(*refs, nk, epilogue, dual, nparts):
    refs = list(refs)
    a_refs = [refs.pop(0) for _ in range(nparts)]
    w_ref = refs.pop(0)
    r_ref = refs.pop(0) if epilogue == "residual" else None
    tab_refs = [refs.pop(0) for _ in range(3)] if epilogue == "rope" else None
    a2_ref = refs.pop(0) if dual else None
    r2_ref = refs.pop(0) if dual and epilogue == "residual" else None
    o_ref = refs.pop(0)
    o2_ref = refs.pop(0) if dual else None
    k = pl.program_id(2)
    w = w_ref[...]
    _mm_tile(a_refs, w, r_ref, o_ref, k, nk, epilogue, tab_refs)
    if dual:
        @pl.when(pl.program_id(1) == 0)
        def _():
            _mm_tile([a2_ref], w, r2_ref, o2_ref, k, nk, epilogue)


def _matmul(a, w, layer, out_dtype, epilogue=None, residual=None, n=None, n0=0, a2=None, residual2=None,
            rope=None, bm_pref=1024, bn_pref=1024, bk_pref=4096):
    a_parts = a if isinstance(a, tuple) else (a,)
    m = a_parts[0].shape[0]
    kdim = sum(p.shape[1] for p in a_parts)
    n = w.shape[2] if n is None else n
    bm = _pick_block(rope[1] if epilogue == "rope" else m, bm_pref, 8)
    bn = _pick_block(math.gcd(n, n0) if n0 else n, bn_pref)
    bk = _pick_block(kdim, bk_pref)
    nk = kdim // bk
    dual = a2 is not None
    j0 = n0 // bn
    assert nk == 1 or out_dtype == F32, "a split contraction accumulates in the f32 output block"
    assert nk == 1 or (len(a_parts) == 1 and epilogue != "rope")
    if len(a_parts) == 1:
        in_specs = [pl.BlockSpec((bm, bk), lambda j, i, k: (i, k))]
    else:
        in_specs = [pl.BlockSpec((bm, p.shape[1]), lambda j, i, k: (i, 0)) for p in a_parts]
    in_specs.append(pl.BlockSpec((None, bk, bn), lambda j, i, k: (layer, k, j + j0)))
    args = list(a_parts) + [w]
    out_specs = [pl.BlockSpec((bm, bn), lambda j, i, k: (i, j))]
    out_shape = [jax.ShapeDtypeStruct((m, n), out_dtype)]
    if epilogue == "residual":
        in_specs.append(pl.BlockSpec((bm, bn), lambda j, i, k: (i, j)))
        args.append(residual)
    if epilogue == "rope":
        nt = rope[1] // bm
        in_specs += [pl.BlockSpec((bm, HEAD_DIM), lambda j, i, k: (i % nt, 0))] * 3
        args += list(rope[0])
    if dual:
        m2 = a2.shape[0]
        in_specs.append(pl.BlockSpec((m2, bk), lambda j, i, k: (0, k)))
        args.append(a2)
        if epilogue == "residual":
            in_specs.append(pl.BlockSpec((m2, bn), lambda j, i, k: (0, j)))
            args.append(residual2)
        out_specs.append(pl.BlockSpec((m2, bn), lambda j, i, k: (0, j)))
        out_shape.append(jax.ShapeDtypeStruct((m2, n), out_dtype))
    out = pl.pallas_call(
        functools.partial(_mm_kernel, nk=nk, epilogue=epilogue, dual=dual, nparts=len(a_parts)),
        grid=(n // bn, m // bm, nk),
        in_specs=in_specs,
        out_specs=out_specs,
        out_shape=out_shape,
        compiler_params=_cparams("parallel", "arbitrary" if dual else "parallel", "arbitrary"),
        name="matmul",
    )(*args)
    return tuple(out) if dual else out[0]


def _rope_tables(pos):
    half = ROT_DIM // 2
    inv = ROPE_THETA ** (-jnp.arange(half, dtype=F32) / half)
    ang = pos.astype(F32)[:, None] * inv[None, :]
    cos, sin = jnp.cos(ang), jnp.sin(ang)
    r = pos.shape[0]
    z16 = jnp.zeros((r, half), F32)
    zrest = jnp.zeros((r, HEAD_DIM - ROT_DIM), F32)
    c = jnp.concatenate([cos, cos, jnp.ones((r, HEAD_DIM - ROT_DIM), F32)], axis=1)
    sm = jnp.concatenate([-sin, z16, zrest], axis=1)
    sp = jnp.concatenate([z16, sin, zrest], axis=1)
    return c, sm, sp


def _rope_head(x, c, sm, sp):
    half = ROT_DIM // 2
    return x * c + pltpu.roll(x, HEAD_DIM - half, 1) * sm + pltpu.roll(x, half, 1) * sp


def _dsa_split_kernel(pa_ref, pb_ref, c_ref, sm_ref, sp_ref, *out_refs, nh, nkv, ih, memw, with_q):
    c, sm, sp = c_ref[...], sm_ref[...], sp_ref[...]
    qw, kvw = nh * HEAD_DIM, nkv * HEAD_DIM
    out_refs = list(out_refs)
    kv0 = 0
    if with_q:
        q_ref, qi_ref = out_refs.pop(0), out_refs.pop(0)
        for h in range(nh):
            sl = slice(h * HEAD_DIM, (h + 1) * HEAD_DIM)
            q_ref[:, sl] = _rope_head(pa_ref[:, sl], c, sm, sp).astype(q_ref.dtype)
        base = qw + 2 * kvw
        for h in range(ih):
            sl = slice(h * IDX_DIM, (h + 1) * IDX_DIM)
            qi_ref[:, sl] = _rope_head(pa_ref[:, base + h * IDX_DIM: base + (h + 1) * IDX_DIM], c, sm, sp).astype(
                qi_ref.dtype)
        kv0 = qw
    kf_ref, kb_ref, vf_ref, vb_ref, kif_ref, kib_ref, wi_ref, qm_ref = out_refs
    for h in range(nkv):
        sl = slice(h * HEAD_DIM, (h + 1) * HEAD_DIM)
        kr = _rope_head(pa_ref[:, kv0 + h * HEAD_DIM: kv0 + (h + 1) * HEAD_DIM], c, sm, sp)
        kf_ref[:, sl] = kr
        kb_ref[:, sl] = kr.astype(kb_ref.dtype)
    v = pa_ref[:, kv0 + kvw: kv0 + 2 * kvw]
    vf_ref[...] = v
    vb_ref[...] = v.astype(vb_ref.dtype)
    ki = _rope_head(pb_ref[:, 0:IDX_DIM], c, sm, sp)
    kif_ref[...] = ki
    kib_ref[...] = ki.astype(kib_ref.dtype)
    qm_ref[...] = pb_ref[:, IDX_DIM:IDX_DIM + memw].astype(qm_ref.dtype)
    wi_ref[...] = pb_ref[:, IDX_DIM + memw:IDX_DIM + memw + 128] * ((ih * IDX_DIM) ** -0.5)


def _dsa_split(pa, pb, tabs, rows_per_seq, nh, nkv, ih, memw, act_dtype, with_q):
    m = pa.shape[0]
    bm = _pick_block(rows_per_seq, 128, 8)
    nt = rows_per_seq // bm
    qw, kvw = nh * HEAD_DIM, nkv * HEAD_DIM
    row = lambda w: pl.BlockSpec((bm, w), lambda i: (i, 0))
    tab = pl.BlockSpec((bm, HEAD_DIM), lambda i: (i % nt, 0))
    shapes = [(qw, act_dtype), (ih * IDX_DIM, act_dtype)] if with_q else []
    shapes += [(kvw, F32), (kvw, act_dtype), (kvw, F32), (kvw, act_dtype),
               (IDX_DIM, F32), (IDX_DIM, act_dtype), (128, F32), (memw, act_dtype)]
    return pl.pallas_call(
        functools.partial(_dsa_split_kernel, nh=nh, nkv=nkv, ih=ih, memw=memw, with_q=with_q),
        grid=(m // bm,),
        in_specs=[row(pa.shape[1]), row(pb.shape[1]), tab, tab, tab],
        out_specs=[row(w) for w, _ in shapes],
        out_shape=[jax.ShapeDtypeStruct((m, w), dt) for w, dt in shapes],
        compiler_params=_cparams("parallel"),
        name="dsa_split",
    )(pa, pb, *tabs)


def _moba_split_kernel(p_ref, c_ref, sm_ref, sp_ref, *out_refs, nh, nkv, memw, with_q):
    c, sm, sp = c_ref[...], sm_ref[...], sp_ref[...]
    qw, kvw = nh * HEAD_DIM, nkv * HEAD_DIM
    out_refs = list(out_refs)
    kv0 = 0
    if with_q:
        q_ref = out_refs.pop(0)
        for h in range(nh):
            sl = slice(h * HEAD_DIM, (h + 1) * HEAD_DIM)
            q_ref[:, sl] = _rope_head(p_ref[:, sl], c, sm, sp).astype(q_ref.dtype)
        kv0 = qw
    kf_ref, kb_ref, vf_ref, vb_ref, qm_ref, km_ref = out_refs
    rows = p_ref.shape[0]
    for h in range(nkv):
        sl = slice(h * HEAD_DIM, (h + 1) * HEAD_DIM)
        kr = _rope_head(p_ref[:, kv0 + h * HEAD_DIM: kv0 + (h + 1) * HEAD_DIM], c, sm, sp)
        kf_ref[:, sl] = kr
        kb_ref[:, sl] = kr.astype(kb_ref.dtype)
        km_ref[0, :, sl] = jnp.sum(kr, axis=0, keepdims=True) * (1.0 / rows)
    v = p_ref[:, kv0 + kvw: kv0 + 2 * kvw]
    vf_ref[...] = v
    vb_ref[...] = v.astype(vb_ref.dtype)
    qm_ref[...] = p_ref[:, kv0 + 2 * kvw: kv0 + 2 * kvw + memw].astype(qm_ref.dtype)


def _moba_split(p, tabs, rows_per_seq, nh, nkv, memw, act_dtype, with_q):
    m = p.shape[0]
    bm = _pick_block(rows_per_seq, MOBA_BLOCK, 8)
    nt = rows_per_seq // bm
    qw, kvw = nh * HEAD_DIM, nkv * HEAD_DIM
    row = lambda w: pl.BlockSpec((bm, w), lambda i: (i, 0))
    tab = pl.BlockSpec((bm, HEAD_DIM), lambda i: (i % nt, 0))
    shapes = [(qw, act_dtype)] if with_q else []
    shapes += [(kvw, F32), (kvw, act_dtype), (kvw, F32), (kvw, act_dtype), (memw, act_dtype)]
    return pl.pallas_call(
        functools.partial(_moba_split_kernel, nh=nh, nkv=nkv, memw=memw, with_q=with_q),
        grid=(m // bm,),
        in_specs=[row(p.shape[1]), tab, tab, tab],
        out_specs=[row(w) for w, _ in shapes] + [pl.BlockSpec((1, 1, kvw), lambda i: (i, 0, 0))],
        out_shape=[jax.ShapeDtypeStruct((m, w), dt) for w, dt in shapes]
        + [jax.ShapeDtypeStruct((m // bm, 1, kvw), F32)],
        compiler_params=_cparams("parallel"),
        name="moba_split",
    )(p, *tabs)


def _sortable_key(x):
    bits = pltpu.bitcast(x, I32)
    return jnp.where(bits < 0, bits ^ jnp.int32(0x7FFFFFFF), bits)


def _kth_largest_key(count_ge, shape, k):
    imin = jnp.int32(-2 ** 31)
    c0 = count_ge(jnp.zeros(shape, I32))
    thr = jnp.where(c0 >= k, jnp.int32(0), imin)

    def bit_body(it, thr):
        cand = thr + jnp.left_shift(jnp.int32(1), jnp.int32(30) - it)
        return jnp.where(count_ge(cand) >= k, cand, thr)

    return lax.fori_loop(0, 31, bit_body, thr)


def _lane_fold(x, acc, op):
    for w in range(x.shape[1] // 128):
        acc = op(acc, x[:, w * 128:(w + 1) * 128])
    return acc


def _chunk_loop(n, body):
    def pair(c2, _):
        body(2 * c2)
        body(2 * c2 + 1)
        return 0

    def single(c, _):
        body(c)
        return 0

    npair = n // 2
    lax.fori_loop(0, npair, pair, 0)
    lax.fori_loop(2 * npair, n, single, 0)


def _masked_attention(nheads, n_dyn, chunk_logits, load_v, mx_ref, lsum_ref, acc_ref, own=None):
    rows = mx_ref.shape[1]
    for n in range(nheads):
        mx = jnp.full((rows, 128), NEG, F32)
        if own is not None:
            mx = _lane_fold(own[0][n], mx, jnp.maximum)
        mx_ref[n] = mx

    def pass_a(c):
        of_head = chunk_logits(c)
        for n in range(nheads):
            mx_ref[n] = _lane_fold(of_head(n), mx_ref[n], jnp.maximum)

    _chunk_loop(n_dyn, pass_a)
    for n in range(nheads):
        mx_ref[n] = jnp.broadcast_to(jnp.max(mx_ref[n], axis=1, keepdims=True), (rows, 128))

    def accumulate(n, s, vb, first=False):
        mb = mx_ref[n]
        p = jnp.exp2(s - jnp.concatenate([mb] * (s.shape[1] // 128), axis=1))
        pv = jnp.dot(p.astype(BF16), vb, preferred_element_type=F32)
        if first:
            lsum_ref[n] = _lane_fold(p, jnp.zeros((rows, 128), F32), jnp.add)
            acc_ref[n] = pv
        else:
            lsum_ref[n] = _lane_fold(p, lsum_ref[n], jnp.add)
            acc_ref[n] = acc_ref[n] + pv

    for n in range(nheads):
        if own is not None:
            accumulate(n, own[0][n], own[1](n), first=True)
        else:
            lsum_ref[n] = jnp.zeros((rows, 128), F32)
            acc_ref[n] = jnp.zeros((rows, HEAD_DIM), F32)

    def pass_b(c):
        of_head = chunk_logits(c)
        for n in range(nheads):
            accumulate(n, of_head(n), load_v(c, n))

    _chunk_loop(n_dyn, pass_b)


def _write_heads(o_ref, lsum_ref, acc_ref, nkv, tq):
    for n in range(nkv):
        out = acc_ref[n] / jnp.sum(lsum_ref[n], axis=1, keepdims=True)
        for g in range(GROUP):
            o_ref[:, (n * GROUP + g) * HEAD_DIM:(n * GROUP + g + 1) * HEAD_DIM] = (
                out[g * tq:(g + 1) * tq].astype(o_ref.dtype))


def _attn_state_scratch(nkv):
    return [pltpu.VMEM((nkv, GROUP * QTILE, 128), F32)] * 3


def _dsa_prompt_kernel(qi_ref, wi_ref, ki_ref, q_ref, k_ref, v_ref, o_ref, key_ref, bias_ref, q3_ref,
                       mx_ref, lsum_ref, acc_ref, *, topk, nkv, ih):
    i = pl.program_id(1)
    tq = ck = QTILE
    key_i = lax.broadcasted_iota(I32, (ck, tq), 0)
    qry_i = lax.broadcasted_iota(I32, (ck, tq), 1)
    wi_t = wi_ref[...].T

    def causal_ok(kc):
        return key_i <= qry_i + jnp.where(kc < i, ck, 0)

    def idx_body(kc, _):
        off = pl.multiple_of(kc * ck, ck)
        kic = ki_ref[0, pl.ds(off, ck), :]
        acc = jnp.zeros((ck, tq), F32)
        for h in range(ih):
            s = lax.dot_general(kic, qi_ref[:, h * IDX_DIM:(h + 1) * IDX_DIM], _CONTRACT_LAST,
                                preferred_element_type=F32)
            acc = acc + jnp.maximum(s, 0.0) * wi_t[h:h + 1, :]
        key_ref[kc] = _sortable_key(jnp.where(causal_ok(kc), acc, -jnp.inf))
        return 0

    lax.fori_loop(0, i + 1, idx_body, 0)

    def count_ge(cand):
        def body(kc, acc):
            ge = jnp.where(key_ref[kc] >= cand, 1.0, 0.0)
            return acc + jnp.sum(ge.reshape(ck // 8, 8, tq), axis=0)
        acc = lax.fori_loop(0, i + 1, body, jnp.zeros((8, tq), F32))
        return jnp.sum(acc, axis=0, keepdims=True)

    thr = _kth_largest_key(count_ge, (1, tq), topk)
    need = topk - count_ge(thr + 1)
    incl_prefix = (lax.broadcasted_iota(I32, (ck, ck), 1) <= lax.broadcasted_iota(I32, (ck, ck), 0)).astype(BF16)

    def bias_body(kc, seen):
        key = key_ref[kc]
        tied = key == thr
        tied_f = jnp.where(tied, 1.0, 0.0)
        rank = seen + jnp.dot(incl_prefix, tied_f.astype(BF16), preferred_element_type=F32)
        sel = jnp.logical_or(key > thr, jnp.logical_and(tied, rank <= need))
        bias_ref[kc] = jnp.where(jnp.logical_and(sel, causal_ok(kc)), 0.0, NEG).T
        return seen + jnp.sum(jnp.sum(tied_f.reshape(ck // 8, 8, tq), axis=0), axis=0, keepdims=True)

    lax.fori_loop(0, i + 1, bias_body, jnp.zeros((1, tq), F32))

    for n in range(nkv):
        q3_ref[n] = jnp.concatenate(
            [q_ref[:, (n * GROUP + g) * HEAD_DIM:(n * GROUP + g + 1) * HEAD_DIM] for g in range(GROUP)], axis=0)

    def chunk(ref, kc, n):
        return ref[0, pl.ds(pl.multiple_of(kc * ck, ck), ck), n * HEAD_DIM:(n + 1) * HEAD_DIM]

    def chunk_logits(kc):
        bias = jnp.concatenate([bias_ref[kc]] * GROUP, axis=0)

        def of_head(n):
            s = lax.dot_general(q3_ref[n], chunk(k_ref, kc, n), _CONTRACT_LAST, preferred_element_type=F32)
            return s * SOFTMAX_C2 + bias
        return of_head

    _masked_attention(nkv, i + 1, chunk_logits, functools.partial(chunk, v_ref), mx_ref, lsum_ref, acc_ref)
    _write_heads(o_ref, lsum_ref, acc_ref, nkv, tq)


def _dsa_prompt(qi, wi, ki, q, k, v, bsz, t, nkv, ih, topk):
    nt = t // QTILE
    qw = nkv * GROUP * HEAD_DIM
    kvw = nkv * HEAD_DIM
    rows = lambda w: pl.BlockSpec((QTILE, w), lambda b, i: (b * nt + i, 0))
    seq = lambda w: pl.BlockSpec((1, t, w), lambda b, i: (b, 0, 0))
    return pl.pallas_call(
        functools.partial(_dsa_prompt_kernel, topk=topk, nkv=nkv, ih=ih),
        grid=(bsz, nt),
        in_specs=[rows(ih * IDX_DIM), rows(128), seq(IDX_DIM), rows(qw), seq(kvw), seq(kvw)],
        out_specs=rows(qw),
        out_shape=jax.ShapeDtypeStruct((bsz * t, qw), BF16),
        scratch_shapes=[pltpu.VMEM((nt, QTILE, QTILE), I32), pltpu.VMEM((nt, QTILE, QTILE), F32),
                        pltpu.VMEM((nkv, GROUP * QTILE, HEAD_DIM), BF16)] + _attn_state_scratch(nkv),
        compiler_params=_cparams("parallel", "arbitrary"),
        name="dsa_prompt_attn",
    )(qi, wi, ki.reshape(bsz, t, IDX_DIM), q, k.reshape(bsz, t, kvw), v.reshape(bsz, t, kvw))


def _topb_select_bias(gate, valid, blk_iota, nblk):
    g = jnp.where(valid, gate, -jnp.inf)
    rank = jnp.zeros(g.shape, I32)
    for m in range(nblk):
        gm = g[:, m:m + 1]
        beats = jnp.logical_or(gm > g, jnp.logical_and(gm == g, blk_iota > m))
        rank = rank + beats.astype(I32)
    sel = jnp.logical_and(valid, rank < MOBA_TOPB)
    return jnp.where(sel, 0.0, NEG)


def _topb_select_rows(gate_t, valid_t, nblk):
    g = jnp.where(valid_t, gate_t, -jnp.inf)
    blk = lax.broadcasted_iota(I32, g.shape, 0)
    rank = jnp.zeros(g.shape, I32)
    for m in range(nblk):
        gm = g[m:m + 1, :]
        beats = jnp.logical_or(gm > g, jnp.logical_and(gm == g, blk > m))
        rank = rank + beats.astype(I32)
    return jnp.logical_and(valid_t, rank < MOBA_TOPB)


def _moba_prompt_kernel(q_ref, k_ref, v_ref, km_ref, o_ref, qa_ref, so_ref, mx_ref, lsum_ref, acc_ref, *, nkv, nblk):
    j = pl.program_id(1)
    tq = blk = QTILE
    rows = GROUP * tq
    row = lax.broadcasted_iota(I32, (tq, blk), 0)
    col = lax.broadcasted_iota(I32, (tq, blk), 1)
    cb = jnp.where(col <= row, 0.0, NEG)
    causal_bias = jnp.concatenate([cb] * GROUP, axis=0)
    valid_t = lax.broadcasted_iota(I32, (nblk, rows), 0) < j
    lane = lax.broadcasted_iota(I32, (blk, HEAD_DIM), 1)
    km_pad = jnp.zeros((16 - nblk % 16, HEAD_DIM), F32)

    def chunk(ref, m, n):
        return ref[0, pl.ds(pl.multiple_of(m * blk, blk), blk), n * HEAD_DIM:(n + 1) * HEAD_DIM]

    for n in range(nkv):
        lanes = slice(n * HEAD_DIM, (n + 1) * HEAD_DIM)
        q3 = jnp.concatenate(
            [q_ref[:, (n * GROUP + g) * HEAD_DIM:(n * GROUP + g + 1) * HEAD_DIM] for g in range(GROUP)], axis=0)
        km = jnp.concatenate([km_ref[0, :, lanes], km_pad], axis=0).astype(BF16)
        gate_t = lax.dot_general(km, q3, _CONTRACT_LAST, preferred_element_type=F32)[:nblk]
        selb_t = jnp.where(_topb_select_rows(gate_t, valid_t, nblk), 0.0, NEG)
        selb = jnp.concatenate([selb_t, jnp.zeros((HEAD_DIM - nblk, rows), F32)], axis=0).T
        qa_ref[n] = jnp.concatenate([q3, selb.astype(BF16)], axis=1)
        own_s = lax.dot_general(q3, chunk(k_ref, j, n), _CONTRACT_LAST, preferred_element_type=F32)
        so_ref[n] = own_s * SOFTMAX_C2 + causal_bias

    def chunk_logits(m):
        onehot = jnp.where(lane == m, 1.0, 0.0).astype(BF16)

        def of_head(n):
            k_aug = jnp.concatenate([chunk(k_ref, m, n), onehot], axis=1)
            return lax.dot_general(qa_ref[n], k_aug, _CONTRACT_LAST, preferred_element_type=F32) * SOFTMAX_C2
        return of_head

    _masked_attention(nkv, j, chunk_logits, functools.partial(chunk, v_ref), mx_ref, lsum_ref, acc_ref,
                      own=(so_ref, lambda n: chunk(v_ref, j, n)))
    _write_heads(o_ref, lsum_ref, acc_ref, nkv, tq)


def _moba_prompt(q, k, v, kmean, bsz, t, nkv):
    nt = t // QTILE
    qw = nkv * GROUP * HEAD_DIM
    kvw = nkv * HEAD_DIM
    rows = lambda w: pl.BlockSpec((QTILE, w), lambda b, i: (b * nt + i, 0))
    seq = lambda w: pl.BlockSpec((1, t, w), lambda b, i: (b, 0, 0))
    return pl.pallas_call(
        functools.partial(_moba_prompt_kernel, nkv=nkv, nblk=nt),
        grid=(bsz, nt),
        in_specs=[rows(qw), seq(kvw), seq(kvw), pl.BlockSpec((1, nt, kvw), lambda b, i: (b, 0, 0))],
        out_specs=rows(qw),
        out_shape=jax.ShapeDtypeStruct((bsz * t, qw), BF16),
        scratch_shapes=[pltpu.VMEM((nkv, GROUP * QTILE, 2 * HEAD_DIM), BF16),
                        pltpu.VMEM((nkv, GROUP * QTILE, QTILE), F32)] + _attn_state_scratch(nkv),
        compiler_params=_cparams("parallel", "arbitrary"),
        name="moba_prompt_attn",
    )(q, k.reshape(bsz, t, kvw), v.reshape(bsz, t, kvw), kmean.reshape(bsz, nt, kvw))


def _mem_attn_kernel(q_ref, mk_ref, mv_ref, o_ref, *, hd):
    scale = hd ** -0.5
    for c in range(MEM_HEADS):
        lanes = slice(c * hd, (c + 1) * hd)
        q = q_ref[0, :, lanes].astype(BF16)
        mk = mk_ref[0, :, lanes].astype(BF16)
        mv = mv_ref[0, :, lanes].astype(BF16)
        s = lax.dot_general(q, mk, _CONTRACT_LAST, preferred_element_type=F32) * scale
        p = jnp.exp(s - jnp.max(s, axis=1, keepdims=True))
        l = jnp.sum(p, axis=1, keepdims=True)
        o = jnp.dot(p.astype(BF16), mv, preferred_element_type=F32) / l
        o_ref[0, :, lanes] = o.astype(o_ref.dtype)


def _mem_attn(qm, mk, mv, out_dtype):
    bsz, t, w = qm.shape
    mlen = mk.shape[1]
    tq = _pick_block(t, 512, 8)
    return pl.pallas_call(
        functools.partial(_mem_attn_kernel, hd=w // MEM_HEADS),
        grid=(bsz, t // tq),
        in_specs=[pl.BlockSpec((1, tq, w), lambda b, i: (b, i, 0)),
                  pl.BlockSpec((1, mlen, w), lambda b, i: (b, 0, 0)),
                  pl.BlockSpec((1, mlen, w), lambda b, i: (b, 0, 0))],
        out_specs=pl.BlockSpec((1, tq, w), lambda b, i: (b, i, 0)),
        out_shape=jax.ShapeDtypeStruct((bsz, t, w), out_dtype),
        compiler_params=_cparams("parallel", "parallel"),
        name="mem_attn",
    )(qm, mk, mv)


SAMPLE_PP = 16


def _page_specs(shape_tail, pp, layer):
    nd = len(shape_tail)
    return [pl.BlockSpec((None, None) + shape_tail,
                         lambda b, p, pt, c=c: (layer, pt[b, p * pp + c]) + (0,) * nd)
            for c in range(pp)]


def _head_sum(w, ih):
    acc = w[0:SAMPLE_TPAD]
    for h in range(1, ih):
        acc = acc + w[h * SAMPLE_TPAD:(h + 1) * SAMPLE_TPAD]
    return acc


def _dsa_sample_index_kernel(pt_ref, qi_ref, wi_ref, *refs, pp, ih, page):
    page_refs, o_ref = refs[:pp], refs[pp]
    qi = qi_ref[0].astype(BF16)
    wi = wi_ref[0]
    for c in range(pp):
        kp = page_refs[c][...].astype(BF16)
        s = lax.dot_general(qi, kp, _CONTRACT_LAST, preferred_element_type=F32)
        o_ref[0, :, c * page:(c + 1) * page] = _head_sum(jnp.maximum(s, 0.0) * wi, ih)


def _dsa_sample_index(page_table, qi_ht, wi_ht, kidx_pool, layer, ih):
    bs, npg = page_table.shape
    page = kidx_pool.shape[2]
    pp = SAMPLE_PP
    r = ih * SAMPLE_TPAD
    grid_spec = pltpu.PrefetchScalarGridSpec(
        num_scalar_prefetch=1,
        grid=(bs, npg // pp),
        in_specs=[pl.BlockSpec((1, r, IDX_DIM), lambda b, p, pt: (b, 0, 0)),
                  pl.BlockSpec((1, r, page), lambda b, p, pt: (b, 0, 0))]
        + _page_specs((page, IDX_DIM), pp, layer),
        out_specs=pl.BlockSpec((1, SAMPLE_TPAD, page * pp), lambda b, p, pt: (b, 0, p)),
    )
    return pl.pallas_call(
        functools.partial(_dsa_sample_index_kernel, pp=pp, ih=ih, page=page),
        grid_spec=grid_spec,
        out_shape=jax.ShapeDtypeStruct((bs, SAMPLE_TPAD, npg * page), F32),
        compiler_params=_cparams("parallel", "arbitrary"),
        name="dsa_sample_index",
    )(page_table, qi_ht, wi_ht, *([kidx_pool] * pp))


def _dsa_sample_select_kernel(sc_ref, qi_ref, wi_ref, kin_ref, o_ref, *, topk, ih, ts, past):
    tail_w = kin_ref.shape[1]
    s = lax.dot_general(qi_ref[0].astype(BF16), kin_ref[0].astype(BF16), _CONTRACT_LAST,
                        preferred_element_type=F32)
    tail = _head_sum(jnp.maximum(s, 0.0) * wi_ref[0][:, :tail_w], ih)
    row = lax.broadcasted_iota(I32, (SAMPLE_TPAD, tail_w), 0)
    col = lax.broadcasted_iota(I32, (SAMPLE_TPAD, tail_w), 1)
    tail_ok = jnp.logical_and(col <= row, col < ts)
    full = jnp.concatenate([sc_ref[0], jnp.where(tail_ok, tail, -jnp.inf)], axis=1)
    keys = _sortable_key(full)
    width = past + tail_w
    col_f = lax.broadcasted_iota(I32, (SAMPLE_TPAD, width), 1)
    row_f = lax.broadcasted_iota(I32, (SAMPLE_TPAD, width), 0)
    visible = jnp.logical_or(col_f < past, jnp.logical_and(col_f - past <= row_f, col_f - past < ts))

    def count_ge(cand):
        return jnp.sum((keys >= cand).astype(F32), axis=1, keepdims=True)

    thr = _kth_largest_key(count_ge, (SAMPLE_TPAD, 1), topk)
    need = topk - count_ge(thr + 1)
    tied = keys == thr
    incl_prefix = (lax.broadcasted_iota(I32, (128, 128), 0) <= lax.broadcasted_iota(I32, (128, 128), 1)).astype(BF16)
    seen = jnp.zeros((SAMPLE_TPAD, 1), F32)
    ranks = []
    for g in range(width // 128):
        tied_g = jnp.where(tied[:, g * 128:(g + 1) * 128], 1.0, 0.0).astype(BF16)
        pref = jnp.dot(tied_g, incl_prefix, preferred_element_type=F32)
        ranks.append(seen + pref)
        seen = seen + pref[:, 127:128]
    rank = jnp.concatenate(ranks, axis=1)
    sel = jnp.logical_or(keys > thr, jnp.logical_and(tied, rank <= need))
    o_ref[0] = jnp.where(jnp.logical_and(sel, visible), 0.0, NEG)


def _dsa_sample_select(scores, qi_ht, wi_ht, ki_new, ih, ts, topk):
    bs, _, past = scores.shape
    tail_w = ki_new.shape[1]
    r = ih * SAMPLE_TPAD
    return pl.pallas_call(
        functools.partial(_dsa_sample_select_kernel, topk=topk, ih=ih, ts=ts, past=past),
        grid=(bs,),
        in_specs=[pl.BlockSpec((1, SAMPLE_TPAD, past), lambda b: (b, 0, 0)),
                  pl.BlockSpec((1, r, IDX_DIM), lambda b: (b, 0, 0)),
                  pl.BlockSpec((1, r, wi_ht.shape[2]), lambda b: (b, 0, 0)),
                  pl.BlockSpec((1, tail_w, IDX_DIM), lambda b: (b, 0, 0))],
        out_specs=pl.BlockSpec((1, SAMPLE_TPAD, past + tail_w), lambda b: (b, 0, 0)),
        out_shape=jax.ShapeDtypeStruct((bs, SAMPLE_TPAD, past + tail_w), F32),
        compiler_params=_cparams("parallel"),
        name="dsa_sample_select",
    )(scores, qi_ht, wi_ht, ki_new)


def _sample_flash_update(qbd, kblk, vblk, bias, m_ref, l_ref, acc_ref, nkv):
    rows_per_head = GROUP * SAMPLE_TPAD
    s = lax.dot_general(qbd, kblk, _CONTRACT_LAST, preferred_element_type=F32) * (HEAD_DIM ** -0.5) + bias
    m_old = m_ref[...]
    m_new = jnp.maximum(m_old, jnp.max(s, axis=1, keepdims=True))
    alpha = jnp.exp(m_old - m_new)
    p = jnp.exp(s - m_new)
    l_ref[...] = alpha * l_ref[...] + jnp.sum(p, axis=1, keepdims=True)
    m_ref[...] = m_new
    o_full = jnp.dot(p.astype(BF16), vblk, preferred_element_type=F32)
    o_diag = jnp.concatenate(
        [o_full[n * rows_per_head:(n + 1) * rows_per_head, n * HEAD_DIM:(n + 1) * HEAD_DIM] for n in range(nkv)],
        axis=0)
    acc_ref[...] = alpha * acc_ref[...] + o_diag


def _sample_flash_reset(m_ref, l_ref, acc_ref):
    m_ref[...] = jnp.full(m_ref.shape, NEG, F32)
    l_ref[...] = jnp.zeros(l_ref.shape, F32)
    acc_ref[...] = jnp.zeros(acc_ref.shape, F32)


def _heads_on_lanes(refs, nkv):
    keys = refs[0].shape[0] // nkv
    pages = [jnp.concatenate([r[pl.ds(n, keys, stride=nkv), :] for n in range(nkv)], axis=1).astype(BF16)
             for r in refs]
    return jnp.concatenate(pages, axis=0)


def _flat_pool(pool):
    return pool.reshape(pool.shape[0], pool.shape[1], pool.shape[2] * pool.shape[3], pool.shape[4])


def _tile_query_bias(b8, reps):
    return jnp.concatenate([b8] * reps, axis=0)


def _dsa_sample_attn_kernel(pt_ref, q_ref, bias_ref, tbias_ref, kn_ref, vn_ref, *refs, pp, nkv):
    k_refs, v_refs = refs[:pp], refs[pp:2 * pp]
    o_ref, m_ref, l_ref, acc_ref = refs[2 * pp:]
    p = pl.program_id(1)
    reps = nkv * GROUP

    @pl.when(p == 0)
    def _():
        _sample_flash_reset(m_ref, l_ref, acc_ref)

    _sample_flash_update(q_ref[0], _heads_on_lanes(k_refs, nkv), _heads_on_lanes(v_refs, nkv),
                         _tile_query_bias(bias_ref[0], reps), m_ref, l_ref, acc_ref, nkv)

    @pl.when(p == pl.num_programs(1) - 1)
    def _():
        _sample_flash_update(q_ref[0], kn_ref[0].astype(BF16), vn_ref[0].astype(BF16),
                             _tile_query_bias(tbias_ref[0], reps), m_ref, l_ref, acc_ref, nkv)
        o_ref[0] = acc_ref[...] / l_ref[...]


def _sample_attn_specs(nkv, tail_rows):
    rows = nkv * GROUP * SAMPLE_TPAD
    kvw = nkv * HEAD_DIM
    q_spec = pl.BlockSpec((1, rows, kvw), lambda b, p, pt: (b, 0, 0))
    tail_spec = pl.BlockSpec((1, tail_rows, kvw), lambda b, p, pt: (b, 0, 0))
    out_spec = pl.BlockSpec((1, rows, HEAD_DIM), lambda b, p, pt: (b, 0, 0))
    scratch = [pltpu.VMEM((rows, 1), F32), pltpu.VMEM((rows, 1), F32), pltpu.VMEM((rows, HEAD_DIM), F32)]
    return rows, q_spec, tail_spec, out_spec, scratch


def _dsa_sample_attn(page_table, qbd, bias, k_new, v_new, k_pool, v_pool, layer, nkv):
    bs, npg = page_table.shape
    page = k_pool.shape[2]
    pp = SAMPLE_PP
    rows, q_spec, tail_spec, out_spec, scratch = _sample_attn_specs(nkv, k_new.shape[1])
    grid_spec = pltpu.PrefetchScalarGridSpec(
        num_scalar_prefetch=1,
        grid=(bs, npg // pp),
        in_specs=[q_spec,
                  pl.BlockSpec((1, SAMPLE_TPAD, page * pp), lambda b, p, pt: (b, 0, p)),
                  pl.BlockSpec((1, SAMPLE_TPAD, page), lambda b, p, pt: (b, 0, npg)),
                  tail_spec, tail_spec]
        + _page_specs((page * nkv, HEAD_DIM), pp, layer) + _page_specs((page * nkv, HEAD_DIM), pp, layer),
        out_specs=out_spec,
        scratch_shapes=scratch,
    )
    return pl.pallas_call(
        functools.partial(_dsa_sample_attn_kernel, pp=pp, nkv=nkv),
        grid_spec=grid_spec,
        out_shape=jax.ShapeDtypeStruct((bs, rows, HEAD_DIM), F32),
        compiler_params=_cparams("parallel", "arbitrary"),
        name="dsa_sample_attn",
    )(page_table, qbd, bias, bias, k_new, v_new, *([_flat_pool(k_pool)] * pp), *([_flat_pool(v_pool)] * pp))


KMEAN_BLOCKS = 8


def _moba_kmean_kernel(pt_ref, *refs, nblk, ppb):
    k_refs, o_ref = refs[:nblk * ppb], refs[nblk * ppb]
    for blk in range(nblk):
        acc = jnp.sum(k_refs[blk * ppb][...], axis=0)
        for c in range(1, ppb):
            acc = acc + jnp.sum(k_refs[blk * ppb + c][...], axis=0)
        o_ref[0, blk] = acc * (1.0 / MOBA_BLOCK)


def _moba_sample_kmean(page_table, k_pool, layer, nkv):
    bs, npg = page_table.shape
    page = k_pool.shape[2]
    ppb = MOBA_BLOCK // page
    nb = npg // ppb
    nblk = KMEAN_BLOCKS
    grid_spec = pltpu.PrefetchScalarGridSpec(
        num_scalar_prefetch=1,
        grid=(bs, nb // nblk),
        in_specs=_page_specs((page, nkv, HEAD_DIM), nblk * ppb, layer),
        out_specs=pl.BlockSpec((1, nblk, nkv, HEAD_DIM), lambda b, p, pt: (b, p, 0, 0)),
    )
    return pl.pallas_call(
        functools.partial(_moba_kmean_kernel, nblk=nblk, ppb=ppb),
        grid_spec=grid_spec,
        out_shape=jax.ShapeDtypeStruct((bs, nb, nkv, HEAD_DIM), F32),
        compiler_params=_cparams("parallel", "arbitrary"),
        name="moba_sample_kmean",
    )(page_table, *([k_pool] * (nblk * ppb)))


def _moba_sample_attn_kernel(pt_ref, q_ref, km_ref, kn_ref, vn_ref, *refs, pp, nkv, page, nb, ts):
    k_refs, v_refs = refs[:pp], refs[pp:2 * pp]
    o_ref, m_ref, l_ref, acc_ref, sb_ref = refs[2 * pp:]
    p = pl.program_id(1)
    rows = nkv * GROUP * SAMPLE_TPAD
    ppb = MOBA_BLOCK // page
    blk_iota = lax.broadcasted_iota(I32, (rows, nb), 1)

    @pl.when(p == 0)
    def _():
        _sample_flash_reset(m_ref, l_ref, acc_ref)
        valid = blk_iota >= 0
        gate = lax.dot_general(q_ref[0], km_ref[0].astype(BF16), _CONTRACT_LAST, preferred_element_type=F32)
        sb_ref[...] = _topb_select_bias(gate, valid, blk_iota, nb)

    selb = sb_ref[...]
    cols = []
    for c in range(pp // ppb):
        blk = p * (pp // ppb) + c
        col = jnp.sum(jnp.where(blk_iota == blk, selb, 0.0), axis=1, keepdims=True)
        cols.append(jnp.broadcast_to(col, (rows, MOBA_BLOCK)))
    _sample_flash_update(q_ref[0], _heads_on_lanes(k_refs, nkv), _heads_on_lanes(v_refs, nkv),
                         jnp.concatenate(cols, axis=1), m_ref, l_ref, acc_ref, nkv)

    @pl.when(p == pl.num_programs(1) - 1)
    def _():
        tail_w = kn_ref.shape[1]
        row = lax.broadcasted_iota(I32, (rows, tail_w), 0) & (SAMPLE_TPAD - 1)
        col = lax.broadcasted_iota(I32, (rows, tail_w), 1)
        tb = jnp.where(jnp.logical_and(col <= row, col < ts), 0.0, NEG)
        _sample_flash_update(q_ref[0], kn_ref[0].astype(BF16), vn_ref[0].astype(BF16), tb,
                             m_ref, l_ref, acc_ref, nkv)
        o_ref[0] = acc_ref[...] / l_ref[...]


def _moba_sample_attn(page_table, qbd, kmean, k_new, v_new, k_pool, v_pool, layer, nkv, ts):
    bs, npg = page_table.shape
    page = k_pool.shape[2]
    kvw = nkv * HEAD_DIM
    pp = SAMPLE_PP
    nb = kmean.shape[1]
    rows, q_spec, tail_spec, out_spec, scratch = _sample_attn_specs(nkv, k_new.shape[1])
    grid_spec = pltpu.PrefetchScalarGridSpec(
        num_scalar_prefetch=1,
        grid=(bs, npg // pp),
        in_specs=[q_spec, pl.BlockSpec((1, nb, kvw), lambda b, p, pt: (b, 0, 0)), tail_spec, tail_spec]
        + _page_specs((page * nkv, HEAD_DIM), pp, layer) + _page_specs((page * nkv, HEAD_DIM), pp, layer),
        out_specs=out_spec,
        scratch_shapes=scratch + [pltpu.VMEM((rows, nb), F32)],
    )
    return pl.pallas_call(
        functools.partial(_moba_sample_attn_kernel, pp=pp, nkv=nkv, page=page, nb=nb, ts=ts),
        grid_spec=grid_spec,
        out_shape=jax.ShapeDtypeStruct((bs, rows, HEAD_DIM), F32),
        compiler_params=_cparams("parallel", "arbitrary"),
        name="moba_sample_attn",
    )(page_table, qbd, kmean, k_new, v_new, *([_flat_pool(k_pool)] * pp), *([_flat_pool(v_pool)] * pp))


def _heads_to_kv_major(q, bs, nkv):
    x = q.reshape(bs, SAMPLE_TPAD, nkv, GROUP, HEAD_DIM).transpose(0, 2, 3, 1, 4)
    eye = jnp.eye(nkv, dtype=q.dtype)
    bd = x[:, :, :, :, None, :] * eye[None, :, None, None, :, None]
    return bd.reshape(bs, nkv * GROUP * SAMPLE_TPAD, nkv * HEAD_DIM).astype(BF16)


def _kv_major_to_rows(o, bs, nkv):
    x = o.reshape(bs, nkv, GROUP, SAMPLE_TPAD, HEAD_DIM)
    return x.transpose(0, 3, 1, 2, 4).reshape(bs * SAMPLE_TPAD, nkv * GROUP * HEAD_DIM)


def _pad_rows(x, n):
    return jnp.pad(x, ((0, 0), (0, n - x.shape[1]), (0, 0)))


def _merge_mixers(h, mix, qm, mk, mv, bsz):
    m = h.shape[0]
    ma = _mem_attn(qm.reshape(bsz, m // bsz, -1), mk, mv, qm.dtype).reshape(m, -1)
    return mix.astype(BF16), ma.astype(BF16)


def _finish_layer(hp, hs, merged_p, merged_s, layer, w_o, g_ffn, w_up, w_down):
    hp = _matmul(merged_p, w_o, layer, F32, epilogue="residual", residual=hp)
    hs = _matmul(merged_s, w_o, layer, F32, epilogue="residual", residual=hs)
    up, us = _matmul(_rmsnorm(hp, g_ffn, BF16), w_up, layer, BF16, epilogue="relu2", a2=_rmsnorm(hs, g_ffn, BF16))
    hp = _matmul(up, w_down, layer, F32, epilogue="residual", residual=hp)
    hs = _matmul(us, w_down, layer, F32, epilogue="residual", residual=hs)
    return hp, hs


def kernel(x_prompt, x_sample, cache_dsa_k, cache_dsa_v, cache_dsa_kidx, cache_moba_k, cache_moba_v, cache_mem_k, cache_mem_v, page_table, mem_prompt, norm_mix, norm_mem, w_in_dsa, w_in_moba, w_mem_kv, w_out, norm_ffn, w_up, w_down, norm_final):
    bsz, t, d = x_prompt.shape
    bs, ts, _ = x_sample.shape
    depth = norm_mix.shape[0]
    nh = (3 * d) // (4 * HEAD_DIM)
    nkv = nh // GROUP
    qw, kvw = nh * HEAD_DIM, nkv * HEAD_DIM
    memw = d // 4
    mhd = memw // MEM_HEADS
    ih = d // 128
    npg = page_table.shape[1]
    page = cache_dsa_k.shape[2]
    past = npg * page
    mlen = mem_prompt.shape[1]
    tp = SAMPLE_TPAD

    hp = x_prompt.reshape(bsz * t, d)
    hs = jnp.pad(x_sample, ((0, 0), (0, tp - ts), (0, 0))).reshape(bs * tp, d)
    pos_p = jnp.arange(t, dtype=I32)
    pos_s = past + (jnp.arange(tp, dtype=I32) % ts)
    tabs_p = _rope_tables(pos_p)
    tabs_s = _rope_tables(jnp.tile(pos_s, bs))

    outs = {name: [] for name in ("pdk", "pdv", "pdki", "sdk", "sdv", "sdki", "pmk", "pmv", "smk", "smv", "mk", "mv")}
    mem_flat = mem_prompt.reshape(bsz * mlen, d)

    w_o, wu, wd, w_mem = w_out.astype(BF16), w_up.astype(BF16), w_down.astype(BF16), w_mem_kv.astype(BF16)
    w_dsa, w_moba = w_in_dsa.astype(BF16), w_in_moba.astype(BF16)
    c0 = qw + 2 * kvw + ih * IDX_DIM
    w_dsa_tail = jnp.concatenate(
        [w_in_dsa[:, :, c0 + ih:c0 + ih + IDX_DIM], w_in_dsa[:, :, c0 + ih + IDX_DIM:], w_in_dsa[:, :, c0:c0 + ih],
         jnp.zeros(w_in_dsa.shape[:2] + (128 - ih,), F32)], axis=2).astype(BF16)

    for i in range(depth):
        j = i // 2
        mkv = _matmul(_rmsnorm(mem_flat, norm_mem[i], BF16), w_mem, i, F32)
        mk_p = mkv[:, :memw].reshape(bsz, mlen, memw)
        mv_p = mkv[:, memw:].reshape(bsz, mlen, memw)
        outs["mk"].append(mk_p.reshape(bsz, mlen, MEM_HEADS, mhd))
        outs["mv"].append(mv_p.reshape(bsz, mlen, MEM_HEADS, mhd))
        hn_p = _rmsnorm(hp, norm_mix[i], BF16)
        hn_s = _rmsnorm(hs, norm_mix[i], BF16)
        mk_s = cache_mem_k[i].reshape(bs, mlen, memw)
        mv_s = cache_mem_v[i].reshape(bs, mlen, memw)
        if i % 2 == 0:
            q = _matmul(hn_p, w_dsa, j, BF16, n=qw, epilogue="rope", rope=(tabs_p, t))
            qi = _matmul(hn_p, w_dsa, j, BF16, n=ih * IDX_DIM, n0=qw + 2 * kvw, epilogue="rope", rope=(tabs_p, t))
            pkv = _matmul(hn_p, w_dsa, j, F32, n=2 * kvw, n0=qw)
            pb, pb_s = _matmul(hn_p, w_dsa_tail, j, F32, a2=hn_s)
            kf, kb, vf, vb, kif, kib, wi, qm = _dsa_split(pkv, pb, tabs_p, t, nh, nkv, ih, memw, BF16, with_q=False)
            outs["pdk"].append(kf.reshape(bsz, t, nkv, HEAD_DIM))
            outs["pdv"].append(vf.reshape(bsz, t, nkv, HEAD_DIM))
            outs["pdki"].append(kif.reshape(bsz, t, IDX_DIM))
            mix_p = _dsa_prompt(qi, wi, kib, q, kb, vb, bsz, t, nkv, ih, min(DSA_TOPK, t // 4))
            pa_s = _matmul(hn_s, w_dsa, j, F32, n=c0)
            q_s, qi_s, kf, _, vf, _, kif, _, wi_s, qm_s = _dsa_split(pa_s, pb_s, tabs_s, bs * tp, nh, nkv, ih, memw, F32,
                                                                     with_q=True)
            outs["sdk"].append(kf.reshape(bs, tp, nkv, HEAD_DIM)[:, :ts])
            outs["sdv"].append(vf.reshape(bs, tp, nkv, HEAD_DIM)[:, :ts])
            outs["sdki"].append(kif.reshape(bs, tp, IDX_DIM)[:, :ts])
            qi_ht = qi_s.reshape(bs, tp, ih, IDX_DIM).transpose(0, 2, 1, 3).reshape(bs, ih * tp, IDX_DIM)
            wi_ht = wi_s.reshape(bs, tp, 128)[:, :, :ih].transpose(0, 2, 1).reshape(bs, ih * tp, 1)
            wi_ht = jnp.broadcast_to(wi_ht, (bs, ih * tp, page))
            scores = _dsa_sample_index(page_table, qi_ht, wi_ht, cache_dsa_kidx, j, ih)
            ki_new = _pad_rows(kif.reshape(bs, tp, IDX_DIM), page)
            bias = _dsa_sample_select(scores, qi_ht, wi_ht, ki_new, ih, ts, min(DSA_TOPK, (past + ts) // 4))
            o_s = _dsa_sample_attn(page_table, _heads_to_kv_major(q_s, bs, nkv), bias,
                                   _pad_rows(kf.reshape(bs, tp, kvw), page), _pad_rows(vf.reshape(bs, tp, kvw), page),
                                   cache_dsa_k, cache_dsa_v, j, nkv)
            mix_s = _kv_major_to_rows(o_s, bs, nkv)
        else:
            q = _matmul(hn_p, w_moba, j, BF16, n=qw, epilogue="rope", rope=(tabs_p, t))
            p_rest = _matmul(hn_p, w_moba, j, F32, n=2 * kvw + memw, n0=qw)
            kf, kb, vf, vb, qm, kmean = _moba_split(p_rest, tabs_p, t, nh, nkv, memw, BF16, with_q=False)
            outs["pmk"].append(kf.reshape(bsz, t, nkv, HEAD_DIM))
            outs["pmv"].append(vf.reshape(bsz, t, nkv, HEAD_DIM))
            mix_p = _moba_prompt(q, kb, vb, kmean, bsz, t, nkv)
            p_s = _matmul(hn_s, w_moba, j, F32)
            q_s, kf, _, vf, _, qm_s, _ = _moba_split(p_s, tabs_s, bs * tp, nh, nkv, memw, F32, with_q=True)
            outs["smk"].append(kf.reshape(bs, tp, nkv, HEAD_DIM)[:, :ts])
            outs["smv"].append(vf.reshape(bs, tp, nkv, HEAD_DIM)[:, :ts])
            kmean_s = _moba_sample_kmean(page_table, cache_moba_k, j, nkv)
            kmean_s = kmean_s.reshape(bs, kmean_s.shape[1], kvw)
            o_s = _moba_sample_attn(page_table, _heads_to_kv_major(q_s, bs, nkv), kmean_s,
                                    _pad_rows(kf.reshape(bs, tp, kvw), page), _pad_rows(vf.reshape(bs, tp, kvw), page),
                                    cache_moba_k, cache_moba_v, j, nkv, ts)
            mix_s = _kv_major_to_rows(o_s, bs, nkv)
        hp, hs = _finish_layer(hp, hs, _merge_mixers(hp, mix_p, qm, mk_p, mv_p, bsz),
                               _merge_mixers(hs, mix_s, qm_s, mk_s, mv_s, bs), i, w_o, norm_ffn[i], wu, wd)

    y_prompt = _rmsnorm(hp, norm_final, F32).reshape(bsz, t, d)
    y_sample = _rmsnorm(hs, norm_final, F32).reshape(bs, tp, d)[:, :ts]
    st = jnp.stack
    return (y_prompt, y_sample, st(outs["pdk"]), st(outs["pdv"]), st(outs["pdki"]), st(outs["pmk"]), st(outs["pmv"]),
            st(outs["mk"]), st(outs["mv"]), st(outs["sdk"]), st(outs["sdv"]), st(outs["sdki"]),
            st(outs["smk"]), st(outs["smv"]))
```

```python
import functools
import math

import jax
import jax.numpy as jnp
from jax import lax
from jax.experimental import pallas as pl
from jax.experimental.pallas import tpu as pltpu

F32 = jnp.float32
BF16 = jnp.bfloat16
I32 = jnp.int32

HEAD_DIM = 128
GROUP = 3
ROT_DIM = 32
ROPE_THETA = 500000.0
IDX_DIM = 128
MEM_HEADS = 4
DSA_TOPK = 256
MOBA_BLOCK = 256
MOBA_TOPB = 3
EPS = 1e-6
NEG = -1e30
LOG2E = 1.4426950408889634
SOFTMAX_C2 = (HEAD_DIM ** -0.5) * LOG2E
QTILE = 256
SAMPLE_TPAD = 8
VMEM_LIMIT = 60 * 1024 * 1024

_CONTRACT_LAST = (((1,), (1,)), ((), ()))


def _cparams(*sem):
    return pltpu.CompilerParams(dimension_semantics=sem, vmem_limit_bytes=VMEM_LIMIT)


def _pick_block(n, pref, align=128):
    if n <= pref:
        return n
    b = (pref // align) * align
    while b >= align:
        if n % b == 0:
            return b
        b -= align
    return n


def _rmsnorm_kernel(x_ref, g_ref, o_ref):
    x = x_ref[...]
    ms = jnp.mean(x * x, axis=-1, keepdims=True)
    o_ref[...] = ((x * lax.rsqrt(ms + EPS)) * g_ref[...]).astype(o_ref.dtype)


def _rmsnorm(x, g, out_dtype):
    m, d = x.shape
    bm = _pick_block(m, 256, 8)
    return pl.pallas_call(
        _rmsnorm_kernel,
        grid=(m // bm,),
        in_specs=[pl.BlockSpec((bm, d), lambda i: (i, 0)), pl.BlockSpec((1, d), lambda i: (0, 0))],
        out_specs=pl.BlockSpec((bm, d), lambda i: (i, 0)),
        out_shape=jax.ShapeDtypeStruct((m, d), out_dtype),
        compiler_params=_cparams("parallel"),
        name="rmsnorm",
    )(x, g.reshape(1, d).astype(F32))


def _mm_tile(a_refs, w, r_ref, o_ref, k, nk, epilogue, tab_refs=None):
    part, off = None, 0
    for a_ref in a_refs:
        ka = a_ref.shape[1]
        d = jnp.dot(a_ref[...], w[off:off + ka], preferred_element_type=F32)
        part = d if part is None else part + d
        off += ka

    def finish(acc):
        if epilogue == "rope":
            c, sm, sp = (t[...] for t in tab_refs)
            for h in range(acc.shape[1] // HEAD_DIM):
                sl = slice(h * HEAD_DIM, (h + 1) * HEAD_DIM)
                o_ref[:, sl] = _rope_head(acc[:, sl], c, sm, sp).astype(o_ref.dtype)
            return
        if epilogue == "relu2":
            r = jnp.maximum(acc, 0.0)
            acc = r * r
        elif epilogue == "residual":
            acc = r_ref[...] + acc
        o_ref[...] = acc.astype(o_ref.dtype)

    if nk == 1:
        finish(part)
        return

    @pl.when(k == 0)
    def _():
        o_ref[...] = part

    @pl.when(jnp.logical_and(k > 0, k < nk - 1))
    def _():
        o_ref[...] += part

    @pl.when(k == nk - 1)
    def _():
        finish(o_ref[...] + part)


def _mm_kernel(*refs, nk, epilogue, dual, nparts):
    refs = list(refs)
    a_refs = [refs.pop(0) for _ in range(nparts)]
    w_ref = refs.pop(0)
    r_ref = refs.pop(0) if epilogue == "residual" else None
    tab_refs = [refs.pop(0) for _ in range(3)] if epilogue == "rope" else None
    a2_ref = refs.pop(0) if dual else None
    r2_ref = refs.pop(0) if dual and epilogue == "residual" else None
    o_ref = refs.pop(0)
    o2_ref = refs.pop(0) if dual else None
    k = pl.program_id(2)
    w = w_ref[...]
    _mm_tile(a_refs, w, r_ref, o_ref, k, nk, epilogue, tab_refs)
    if dual:
        @pl.when(pl.program_id(1) == 0)
        def _():
            _mm_tile([a2_ref], w, r2_ref, o2_ref, k, nk, epilogue)


def _matmul(a, w, layer, out_dtype, epilogue=None, residual=None, n=None, n0=0, a2=None, residual2=None,
            rope=None, bm_pref=1024, bn_pref=1024, bk_pref=4096):
    a_parts = a if isinstance(a, tuple) else (a,)
    m = a_parts[0].shape[0]
    kdim = sum(p.shape[1] for p in a_parts)
    n = w.shape[2] if n is None else n
    bm = _pick_block(rope[1] if epilogue == "rope" else m, bm_pref, 8)
    bn = _pick_block(math.gcd(n, n0) if n0 else n, bn_pref)
    bk = _pick_block(kdim, bk_pref)
    nk = kdim // bk
    dual = a2 is not None
    j0 = n0 // bn
    assert nk == 1 or out_dtype == F32, "a split contraction accumulates in the f32 output block"
    assert nk == 1 or (len(a_parts) == 1 and epilogue != "rope")
    if len(a_parts) == 1:
        in_specs = [pl.BlockSpec((bm, bk), lambda j, i, k: (i, k))]
    else:
        in_specs = [pl.BlockSpec((bm, p.shape[1]), lambda j, i, k: (i, 0)) for p in a_parts]
    in_specs.append(pl.BlockSpec((None, bk, bn), lambda j, i, k: (layer, k, j + j0)))
    args = list(a_parts) + [w]
    out_specs = [pl.BlockSpec((bm, bn), lambda j, i, k: (i, j))]
    out_shape = [jax.ShapeDtypeStruct((m, n), out_dtype)]
    if epilogue == "residual":
        in_specs.append(pl.BlockSpec((bm, bn), lambda j, i, k: (i, j)))
        args.append(residual)
    if epilogue == "rope":
        nt = rope[1] // bm
        in_specs += [pl.BlockSpec((bm, HEAD_DIM), lambda j, i, k: (i % nt, 0))] * 3
        args += list(rope[0])
    if dual:
        m2 = a2.shape[0]
        in_specs.append(pl.BlockSpec((m2, bk), lambda j, i, k: (0, k)))
        args.append(a2)
        if epilogue == "residual":
            in_specs.append(pl.BlockSpec((m2, bn), lambda j, i, k: (0, j)))
            args.append(residual2)
        out_specs.append(pl.BlockSpec((m2, bn), lambda j, i, k: (0, j)))
        out_shape.append(jax.ShapeDtypeStruct((m2, n), out_dtype))
    out = pl.pallas_call(
        functools.partial(_mm_kernel, nk=nk, epilogue=epilogue, dual=dual, nparts=len(a_parts)),
        grid=(n // bn, m // bm, nk),
        in_specs=in_specs,
        out_specs=out_specs,
        out_shape=out_shape,
        compiler_params=_cparams("parallel", "arbitrary" if dual else "parallel", "arbitrary"),
        name="matmul",
    )(*args)
    return tuple(out) if dual else out[0]


def _rope_tables(pos):
    half = ROT_DIM // 2
    inv = ROPE_THETA ** (-jnp.arange(half, dtype=F32) / half)
    ang = pos.astype(F32)[:, None] * inv[None, :]
    cos, sin = jnp.cos(ang), jnp.sin(ang)
    r = pos.shape[0]
    z16 = jnp.zeros((r, half), F32)
    zrest = jnp.zeros((r, HEAD_DIM - ROT_DIM), F32)
    c = jnp.concatenate([cos, cos, jnp.ones((r, HEAD_DIM - ROT_DIM), F32)], axis=1)
    sm = jnp.concatenate([-sin, z16, zrest], axis=1)
    sp = jnp.concatenate([z16, sin, zrest], axis=1)
    return c, sm, sp


def _rope_head(x, c, sm, sp):
    half = ROT_DIM // 2
    return x * c + pltpu.roll(x, HEAD_DIM - half, 1) * sm + pltpu.roll(x, half, 1) * sp


def _dsa_split_kernel(pa_ref, pb_ref, c_ref, sm_ref, sp_ref, *out_refs, nh, nkv, ih, memw, with_q):
    c, sm, sp = c_ref[...], sm_ref[...], sp_ref[...]
    qw, kvw = nh * HEAD_DIM, nkv * HEAD_DIM
    out_refs = list(out_refs)
    kv0 = 0
    if with_q:
        q_ref, qi_ref = out_refs.pop(0), out_refs.pop(0)
        for h in range(nh):
            sl = slice(h * HEAD_DIM, (h + 1) * HEAD_DIM)
            q_ref[:, sl] = _rope_head(pa_ref[:, sl], c, sm, sp).astype(q_ref.dtype)
        base = qw + 2 * kvw
        for h in range(ih):
            sl = slice(h * IDX_DIM, (h + 1) * IDX_DIM)
            qi_ref[:, sl] = _rope_head(pa_ref[:, base + h * IDX_DIM: base + (h + 1) * IDX_DIM], c, sm, sp).astype(
                qi_ref.dtype)
        kv0 = qw
    kf_ref, kb_ref, vf_ref, vb_ref, kif_ref, kib_ref, wi_ref, qm_ref = out_refs
    for h in range(nkv):
        sl = slice(h * HEAD_DIM, (h + 1) * HEAD_DIM)
        kr = _rope_head(pa_ref[:, kv0 + h * HEAD_DIM: kv0 + (h + 1) * HEAD_DIM], c, sm, sp)
        kf_ref[:, sl] = kr
        kb_ref[:, sl] = kr.astype(kb_ref.dtype)
    v = pa_ref[:, kv0 + kvw: kv0 + 2 * kvw]
    vf_ref[...] = v
    vb_ref[...] = v.astype(vb_ref.dtype)
    ki = _rope_head(pb_ref[:, 0:IDX_DIM], c, sm, sp)
    kif_ref[...] = ki
    kib_ref[...] = ki.astype(kib_ref.dtype)
    qm_ref[...] = pb_ref[:, IDX_DIM:IDX_DIM + memw].astype(qm_ref.dtype)
    wi_ref[...] = pb_ref[:, IDX_DIM + memw:IDX_DIM + memw + 128] * ((ih * IDX_DIM) ** -0.5)


def _dsa_split(pa, pb, tabs, rows_per_seq, nh, nkv, ih, memw, act_dtype, with_q):
    m = pa.shape[0]
    bm = _pick_block(rows_per_seq, 128, 8)
    nt = rows_per_seq // bm
    qw, kvw = nh * HEAD_DIM, nkv * HEAD_DIM
    row = lambda w: pl.BlockSpec((bm, w), lambda i: (i, 0))
    tab = pl.BlockSpec((bm, HEAD_DIM), lambda i: (i % nt, 0))
    shapes = [(qw, act_dtype), (ih * IDX_DIM, act_dtype)] if with_q else []
    shapes += [(kvw, F32), (kvw, act_dtype), (kvw, F32), (kvw, act_dtype),
               (IDX_DIM, F32), (IDX_DIM, act_dtype), (128, F32), (memw, act_dtype)]
    return pl.pallas_call(
        functools.partial(_dsa_split_kernel, nh=nh, nkv=nkv, ih=ih, memw=memw, with_q=with_q),
        grid=(m // bm,),
        in_specs=[row(pa.shape[1]), row(pb.shape[1]), tab, tab, tab],
        out_specs=[row(w) for w, _ in shapes],
        out_shape=[jax.ShapeDtypeStruct((m, w), dt) for w, dt in shapes],
        compiler_params=_cparams("parallel"),
        name="dsa_split",
    )(pa, pb, *tabs)


def _moba_split_kernel(p_ref, c_ref, sm_ref, sp_ref, *out_refs, nh, nkv, memw, with_q):
    c, sm, sp = c_ref[...], sm_ref[...], sp_ref[...]
    qw, kvw = nh * HEAD_DIM, nkv * HEAD_DIM
    out_refs = list(out_refs)
    kv0 = 0
    if with_q:
        q_ref = out_refs.pop(0)
        for h in range(nh):
            sl = slice(h * HEAD_DIM, (h + 1) * HEAD_DIM)
            q_ref[:, sl] = _rope_head(p_ref[:, sl], c, sm, sp).astype(q_ref.dtype)
        kv0 = qw
    kf_ref, kb_ref, vf_ref, vb_ref, qm_ref, km_ref = out_refs
    rows = p_ref.shape[0]
    for h in range(nkv):
        sl = slice(h * HEAD_DIM, (h + 1) * HEAD_DIM)
        kr = _rope_head(p_ref[:, kv0 + h * HEAD_DIM: kv0 + (h + 1) * HEAD_DIM], c, sm, sp)
        kf_ref[:, sl] = kr
        kb_ref[:, sl] = kr.astype(kb_ref.dtype)
        km_ref[0, :, sl] = jnp.sum(kr, axis=0, keepdims=True) * (1.0 / rows)
    v = p_ref[:, kv0 + kvw: kv0 + 2 * kvw]
    vf_ref[...] = v
    vb_ref[...] = v.astype(vb_ref.dtype)
    qm_ref[...] = p_ref[:, kv0 + 2 * kvw: kv0 + 2 * kvw + memw].astype(qm_ref.dtype)


def _moba_split(p, tabs, rows_per_seq, nh, nkv, memw, act_dtype, with_q):
    m = p.shape[0]
    bm = _pick_block(rows_per_seq, MOBA_BLOCK, 8)
    nt = rows_per_seq // bm
    qw, kvw = nh * HEAD_DIM, nkv * HEAD_DIM
    row = lambda w: pl.BlockSpec((bm, w), lambda i: (i, 0))
    tab = pl.BlockSpec((bm, HEAD_DIM), lambda i: (i % nt, 0))
    shapes = [(qw, act_dtype)] if with_q else []
    shapes += [(kvw, F32), (kvw, act_dtype), (kvw, F32), (kvw, act_dtype), (memw, act_dtype)]
    return pl.pallas_call(
        functools.partial(_moba_split_kernel, nh=nh, nkv=nkv, memw=memw, with_q=with_q),
        grid=(m // bm,),
        in_specs=[row(p.shape[1]), tab, tab, tab],
        out_specs=[row(w) for w, _ in shapes] + [pl.BlockSpec((1, 1, kvw), lambda i: (i, 0, 0))],
        out_shape=[jax.ShapeDtypeStruct((m, w), dt) for w, dt in shapes]
        + [jax.ShapeDtypeStruct((m // bm, 1, kvw), F32)],
        compiler_params=_cparams("parallel"),
        name="moba_split",
    )(p, *tabs)


def _sortable_key(x):
    bits = pltpu.bitcast(x, I32)
    return jnp.where(bits < 0, bits ^ jnp.int32(0x7FFFFFFF), bits)


def _kth_largest_key(count_ge, shape, k):
    imin = jnp.int32(-2 ** 31)
    c0 = count_ge(jnp.zeros(shape, I32))
    thr = jnp.where(c0 >= k, jnp.int32(0), imin)

    def bit_body(it, thr):
        cand = thr + jnp.left_shift(jnp.int32(1), jnp.int32(30) - it)
        return jnp.where(count_ge(cand) >= k, cand, thr)

    return lax.fori_loop(0, 31, bit_body, thr)


def _lane_fold(x, acc, op):
    for w in range(x.shape[1] // 128):
        acc = op(acc, x[:, w * 128:(w + 1) * 128])
    return acc


def _chunk_loop(n, body):
    def pair(c2, _):
        body(2 * c2)
        body(2 * c2 + 1)
        return 0

    def single(c, _):
        body(c)
        return 0

    npair = n // 2
    lax.fori_loop(0, npair, pair, 0)
    lax.fori_loop(2 * npair, n, single, 0)


def _masked_attention(nheads, n_dyn, chunk_logits, load_v, mx_ref, lsum_ref, acc_ref, own=None):
    rows = mx_ref.shape[1]
    for n in range(nheads):
        mx = jnp.full((rows, 128), NEG, F32)
        if own is not None:
            mx = _lane_fold(own[0][n], mx, jnp.maximum)
        mx_ref[n] = mx

    def pass_a(c):
        of_head = chunk_logits(c)
        for n in range(nheads):
            mx_ref[n] = _lane_fold(of_head(n), mx_ref[n], jnp.maximum)

    _chunk_loop(n_dyn, pass_a)
    for n in range(nheads):
        mx_ref[n] = jnp.broadcast_to(jnp.max(mx_ref[n], axis=1, keepdims=True), (rows, 128))

    def accumulate(n, s, vb, first=False):
        mb = mx_ref[n]
        p = jnp.exp2(s - jnp.concatenate([mb] * (s.shape[1] // 128), axis=1))
        pv = jnp.dot(p.astype(BF16), vb, preferred_element_type=F32)
        if first:
            lsum_ref[n] = _lane_fold(p, jnp.zeros((rows, 128), F32), jnp.add)
            acc_ref[n] = pv
        else:
            lsum_ref[n] = _lane_fold(p, lsum_ref[n], jnp.add)
            acc_ref[n] = acc_ref[n] + pv

    for n in range(nheads):
        if own is not None:
            accumulate(n, own[0][n], own[1](n), first=True)
        else:
            lsum_ref[n] = jnp.zeros((rows, 128), F32)
            acc_ref[n] = jnp.zeros((rows, HEAD_DIM), F32)

    def pass_b(c):
        of_head = chunk_logits(c)
        for n in range(nheads):
            accumulate(n, of_head(n), load_v(c, n))

    _chunk_loop(n_dyn, pass_b)


def _write_heads(o_ref, lsum_ref, acc_ref, nkv, tq):
    for n in range(nkv):
        out = acc_ref[n] / jnp.sum(lsum_ref[n], axis=1, keepdims=True)
        for g in range(GROUP):
            o_ref[:, (n * GROUP + g) * HEAD_DIM:(n * GROUP + g + 1) * HEAD_DIM] = (
                out[g * tq:(g + 1) * tq].astype(o_ref.dtype))


def _attn_state_scratch(nkv):
    return [pltpu.VMEM((nkv, GROUP * QTILE, 128), F32)] * 3


def _dsa_prompt_kernel(qi_ref, wi_ref, ki_ref, q_ref, k_ref, v_ref, o_ref, key_ref, bias_ref, q3_ref,
                       mx_ref, lsum_ref, acc_ref, *, topk, nkv, ih):
    i = pl.program_id(1)
    tq = ck = QTILE
    key_i = lax.broadcasted_iota(I32, (ck, tq), 0)
    qry_i = lax.broadcasted_iota(I32, (ck, tq), 1)
    wi_t = wi_ref[...].T

    def causal_ok(kc):
        return key_i <= qry_i + jnp.where(kc < i, ck, 0)

    def idx_body(kc, _):
        off = pl.multiple_of(kc * ck, ck)
        kic = ki_ref[0, pl.ds(off, ck), :]
        acc = jnp.zeros((ck, tq), F32)
        for h in range(ih):
            s = lax.dot_general(kic, qi_ref[:, h * IDX_DIM:(h + 1) * IDX_DIM], _CONTRACT_LAST,
                                preferred_element_type=F32)
            acc = acc + jnp.maximum(s, 0.0) * wi_t[h:h + 1, :]
        key_ref[kc] = _sortable_key(jnp.where(causal_ok(kc), acc, -jnp.inf))
        return 0

    lax.fori_loop(0, i + 1, idx_body, 0)

    def count_ge(cand):
        def body(kc, acc):
            ge = jnp.where(key_ref[kc] >= cand, 1.0, 0.0)
            return acc + jnp.sum(ge.reshape(ck // 8, 8, tq), axis=0)
        acc = lax.fori_loop(0, i + 1, body, jnp.zeros((8, tq), F32))
        return jnp.sum(acc, axis=0, keepdims=True)

    thr = _kth_largest_key(count_ge, (1, tq), topk)
    need = topk - count_ge(thr + 1)
    incl_prefix = (lax.broadcasted_iota(I32, (ck, ck), 1) <= lax.broadcasted_iota(I32, (ck, ck), 0)).astype(BF16)

    def bias_body(kc, seen):
        key = key_ref[kc]
        tied = key == thr
        tied_f = jnp.where(tied, 1.0, 0.0)
        rank = seen + jnp.dot(incl_prefix, tied_f.astype(BF16), preferred_element_type=F32)
        sel = jnp.logical_or(key > thr, jnp.logical_and(tied, rank <= need))
        bias_ref[kc] = jnp.where(jnp.logical_and(sel, causal_ok(kc)), 0.0, NEG).T
        return seen + jnp.sum(jnp.sum(tied_f.reshape(ck // 8, 8, tq), axis=0), axis=0, keepdims=True)

    lax.fori_loop(0, i + 1, bias_body, jnp.zeros((1, tq), F32))

    for n in range(nkv):
        q3_ref[n] = jnp.concatenate(
            [q_ref[:, (n * GROUP + g) * HEAD_DIM:(n * GROUP + g + 1) * HEAD_DIM] for g in range(GROUP)], axis=0)

    def chunk(ref, kc, n):
        return ref[0, pl.ds(pl.multiple_of(kc * ck, ck), ck), n * HEAD_DIM:(n + 1) * HEAD_DIM]

    def chunk_logits(kc):
        bias = jnp.concatenate([bias_ref[kc]] * GROUP, axis=0)

        def of_head(n):
            s = lax.dot_general(q3_ref[n], chunk(k_ref, kc, n), _CONTRACT_LAST, preferred_element_type=F32)
            return s * SOFTMAX_C2 + bias
        return of_head

    _masked_attention(nkv, i + 1, chunk_logits, functools.partial(chunk, v_ref), mx_ref, lsum_ref, acc_ref)
    _write_heads(o_ref, lsum_ref, acc_ref, nkv, tq)


def _dsa_prompt(qi, wi, ki, q, k, v, bsz, t, nkv, ih, topk):
    nt = t // QTILE
    qw = nkv * GROUP * HEAD_DIM
    kvw = nkv * HEAD_DIM
    rows = lambda w: pl.BlockSpec((QTILE, w), lambda b, i: (b * nt + i, 0))
    seq = lambda w: pl.BlockSpec((1, t, w), lambda b, i: (b, 0, 0))
    return pl.pallas_call(
        functools.partial(_dsa_prompt_kernel, topk=topk, nkv=nkv, ih=ih),
        grid=(bsz, nt),
        in_specs=[rows(ih * IDX_DIM), rows(128), seq(IDX_DIM), rows(qw), seq(kvw), seq(kvw)],
        out_specs=rows(qw),
        out_shape=jax.ShapeDtypeStruct((bsz * t, qw), BF16),
        scratch_shapes=[pltpu.VMEM((nt, QTILE, QTILE), I32), pltpu.VMEM((nt, QTILE, QTILE), F32),
                        pltpu.VMEM((nkv, GROUP * QTILE, HEAD_DIM), BF16)] + _attn_state_scratch(nkv),
        compiler_params=_cparams("parallel", "arbitrary"),
        name="dsa_prompt_attn",
    )(qi, wi, ki.reshape(bsz, t, IDX_DIM), q, k.reshape(bsz, t, kvw), v.reshape(bsz, t, kvw))


def _topb_select_bias(gate, valid, blk_iota, nblk):
    g = jnp.where(valid, gate, -jnp.inf)
    rank = jnp.zeros(g.shape, I32)
    for m in range(nblk):
        gm = g[:, m:m + 1]
        beats = jnp.logical_or(gm > g, jnp.logical_and(gm == g, blk_iota > m))
        rank = rank + beats.astype(I32)
    sel = jnp.logical_and(valid, rank < MOBA_TOPB)
    return jnp.where(sel, 0.0, NEG)


def _topb_select_rows(gate_t, valid_t, nblk):
    g = jnp.where(valid_t, gate_t, -jnp.inf)
    blk = lax.broadcasted_iota(I32, g.shape, 0)
    rank = jnp.zeros(g.shape, I32)
    for m in range(nblk):
        gm = g[m:m + 1, :]
        beats = jnp.logical_or(gm > g, jnp.logical_and(gm == g, blk > m))
        rank = rank + beats.astype(I32)
    return jnp.logical_and(valid_t, rank < MOBA_TOPB)


def _moba_prompt_kernel(q_ref, k_ref, v_ref, km_ref, o_ref, qa_ref, so_ref, mx_ref, lsum_ref, acc_ref, *, nkv, nblk):
    j = pl.program_id(1)
    tq = blk = QTILE
    rows = GROUP * tq
    row = lax.broadcasted_iota(I32, (tq, blk), 0)
    col = lax.broadcasted_iota(I32, (tq, blk), 1)
    cb = jnp.where(col <= row, 0.0, NEG)
    causal_bias = jnp.concatenate([cb] * GROUP, axis=0)
    valid_t = lax.broadcasted_iota(I32, (nblk, rows), 0) < j
    lane = lax.broadcasted_iota(I32, (blk, HEAD_DIM), 1)
    km_pad = jnp.zeros((16 - nblk % 16, HEAD_DIM), F32)

    def chunk(ref, m, n):
        return ref[0, pl.ds(pl.multiple_of(m * blk, blk), blk), n * HEAD_DIM:(n + 1) * HEAD_DIM]

    for n in range(nkv):
        lanes = slice(n * HEAD_DIM, (n + 1) * HEAD_DIM)
        q3 = jnp.concatenate(
            [q_ref[:, (n * GROUP + g) * HEAD_DIM:(n * GROUP + g + 1) * HEAD_DIM] for g in range(GROUP)], axis=0)
        km = jnp.concatenate([km_ref[0, :, lanes], km_pad], axis=0).astype(BF16)
        gate_t = lax.dot_general(km, q3, _CONTRACT_LAST, preferred_element_type=F32)[:nblk]
        selb_t = jnp.where(_topb_select_rows(gate_t, valid_t, nblk), 0.0, NEG)
        selb = jnp.concatenate([selb_t, jnp.zeros((HEAD_DIM - nblk, rows), F32)], axis=0).T
        qa_ref[n] = jnp.concatenate([q3, selb.astype(BF16)], axis=1)
        own_s = lax.dot_general(q3, chunk(k_ref, j, n), _CONTRACT_LAST, preferred_element_type=F32)
        so_ref[n] = own_s * SOFTMAX_C2 + causal_bias

    def chunk_logits(m):
        onehot = jnp.where(lane == m, 1.0, 0.0).astype(BF16)

        def of_head(n):
            k_aug = jnp.concatenate([chunk(k_ref, m, n), onehot], axis=1)
            return lax.dot_general(qa_ref[n], k_aug, _CONTRACT_LAST, preferred_element_type=F32) * SOFTMAX_C2
        return of_head

    _masked_attention(nkv, j, chunk_logits, functools.partial(chunk, v_ref), mx_ref, lsum_ref, acc_ref,
                      own=(so_ref, lambda n: chunk(v_ref, j, n)))
    _write_heads(o_ref, lsum_ref, acc_ref, nkv, tq)


def _moba_prompt(q, k, v, kmean, bsz, t, nkv):
    nt = t // QTILE
    qw = nkv * GROUP * HEAD_DIM
    kvw = nkv * HEAD_DIM
    rows = lambda w: pl.BlockSpec((QTILE, w), lambda b, i: (b * nt + i, 0))
    seq = lambda w: pl.BlockSpec((1, t, w), lambda b, i: (b, 0, 0))
    return pl.pallas_call(
        functools.partial(_moba_prompt_kernel, nkv=nkv, nblk=nt),
        grid=(bsz, nt),
        in_specs=[rows(qw), seq(kvw), seq(kvw), pl.BlockSpec((1, nt, kvw), lambda b, i: (b, 0, 0))],
        out_specs=rows(qw),
        out_shape=jax.ShapeDtypeStruct((bsz * t, qw), BF16),
        scratch_shapes=[pltpu.VMEM((nkv, GROUP * QTILE, 2 * HEAD_DIM), BF16),
                        pltpu.VMEM((nkv, GROUP * QTILE, QTILE), F32)] + _attn_state_scratch(nkv),
        compiler_params=_cparams("parallel", "arbitrary"),
        name="moba_prompt_attn",
    )(q, k.reshape(bsz, t, kvw), v.reshape(bsz, t, kvw), kmean.reshape(bsz, nt, kvw))


def _mem_attn_kernel(q_ref, mk_ref, mv_ref, o_ref, *, hd):
    scale = hd ** -0.5
    for c in range(MEM_HEADS):
        lanes = slice(c * hd, (c + 1) * hd)
        q = q_ref[0, :, lanes].astype(BF16)
        mk = mk_ref[0, :, lanes].astype(BF16)
        mv = mv_ref[0, :, lanes].astype(BF16)
        s = lax.dot_general(q, mk, _CONTRACT_LAST, preferred_element_type=F32) * scale
        p = jnp.exp(s - jnp.max(s, axis=1, keepdims=True))
        l = jnp.sum(p, axis=1, keepdims=True)
        o = jnp.dot(p.astype(BF16), mv, preferred_element_type=F32) / l
        o_ref[0, :, lanes] = o.astype(o_ref.dtype)


def _mem_attn(qm, mk, mv, out_dtype):
    bsz, t, w = qm.shape
    mlen = mk.shape[1]
    tq = _pick_block(t, 512, 8)
    return pl.pallas_call(
        functools.partial(_mem_attn_kernel, hd=w // MEM_HEADS),
        grid=(bsz, t // tq),
        in_specs=[pl.BlockSpec((1, tq, w), lambda b, i: (b, i, 0)),
                  pl.BlockSpec((1, mlen, w), lambda b, i: (b, 0, 0)),
                  pl.BlockSpec((1, mlen, w), lambda b, i: (b, 0, 0))],
        out_specs=pl.BlockSpec((1, tq, w), lambda b, i: (b, i, 0)),
        out_shape=jax.ShapeDtypeStruct((bsz, t, w), out_dtype),
        compiler_params=_cparams("parallel", "parallel"),
        name="mem_attn",
    )(qm, mk, mv)


SAMPLE_PP = 16


def _page_specs(shape_tail, pp, layer):
    nd = len(shape_tail)
    return [pl.BlockSpec((None, None) + shape_tail,
                         lambda b, p, pt, c=c: (layer, pt[b, p * pp + c]) + (0,) * nd)
            for c in range(pp)]


def _head_sum(w, ih):
    acc = w[0:SAMPLE_TPAD]
    for h in range(1, ih):
        acc = acc + w[h * SAMPLE_TPAD:(h + 1) * SAMPLE_TPAD]
    return acc


def _dsa_sample_index_kernel(pt_ref, qi_ref, wi_ref, *refs, pp, ih, page):
    page_refs, o_ref = refs[:pp], refs[pp]
    qi = qi_ref[0].astype(BF16)
    wi = wi_ref[0]
    for c in range(pp):
        kp = page_refs[c][...].astype(BF16)
        s = lax.dot_general(qi, kp, _CONTRACT_LAST, preferred_element_type=F32)
        o_ref[0, :, c * page:(c + 1) * page] = _head_sum(jnp.maximum(s, 0.0) * wi, ih)


def _dsa_sample_index(page_table, qi_ht, wi_ht, kidx_pool, layer, ih):
    bs, npg = page_table.shape
    page = kidx_pool.shape[2]
    pp = SAMPLE_PP
    r = ih * SAMPLE_TPAD
    grid_spec = pltpu.PrefetchScalarGridSpec(
        num_scalar_prefetch=1,
        grid=(bs, npg // pp),
        in_specs=[pl.BlockSpec((1, r, IDX_DIM), lambda b, p, pt: (b, 0, 0)),
                  pl.BlockSpec((1, r, page), lambda b, p, pt: (b, 0, 0))]
        + _page_specs((page, IDX_DIM), pp, layer),
        out_specs=pl.BlockSpec((1, SAMPLE_TPAD, page * pp), lambda b, p, pt: (b, 0, p)),
    )
    return pl.pallas_call(
        functools.partial(_dsa_sample_index_kernel, pp=pp, ih=ih, page=page),
        grid_spec=grid_spec,
        out_shape=jax.ShapeDtypeStruct((bs, SAMPLE_TPAD, npg * page), F32),
        compiler_params=_cparams("parallel", "arbitrary"),
        name="dsa_sample_index",
    )(page_table, qi_ht, wi_ht, *([kidx_pool] * pp))


def _dsa_sample_select_kernel(sc_ref, qi_ref, wi_ref, kin_ref, o_ref, *, topk, ih, ts, past):
    tail_w = kin_ref.shape[1]
    s = lax.dot_general(qi_ref[0].astype(BF16), kin_ref[0].astype(BF16), _CONTRACT_LAST,
                        preferred_element_type=F32)
    tail = _head_sum(jnp.maximum(s, 0.0) * wi_ref[0][:, :tail_w], ih)
    row = lax.broadcasted_iota(I32, (SAMPLE_TPAD, tail_w), 0)
    col = lax.broadcasted_iota(I32, (SAMPLE_TPAD, tail_w), 1)
    tail_ok = jnp.logical_and(col <= row, col < ts)
    full = jnp.concatenate([sc_ref[0], jnp.where(tail_ok, tail, -jnp.inf)], axis=1)
    keys = _sortable_key(full)
    width = past + tail_w
    col_f = lax.broadcasted_iota(I32, (SAMPLE_TPAD, width), 1)
    row_f = lax.broadcasted_iota(I32, (SAMPLE_TPAD, width), 0)
    visible = jnp.logical_or(col_f < past, jnp.logical_and(col_f - past <= row_f, col_f - past < ts))

    def count_ge(cand):
        return jnp.sum((keys >= cand).astype(F32), axis=1, keepdims=True)

    thr = _kth_largest_key(count_ge, (SAMPLE_TPAD, 1), topk)
    need = topk - count_ge(thr + 1)
    tied = keys == thr
    incl_prefix = (lax.broadcasted_iota(I32, (128, 128), 0) <= lax.broadcasted_iota(I32, (128, 128), 1)).astype(BF16)
    seen = jnp.zeros((SAMPLE_TPAD, 1), F32)
    ranks = []
    for g in range(width // 128):
        tied_g = jnp.where(tied[:, g * 128:(g + 1) * 128], 1.0, 0.0).astype(BF16)
        pref = jnp.dot(tied_g, incl_prefix, preferred_element_type=F32)
        ranks.append(seen + pref)
        seen = seen + pref[:, 127:128]
    rank = jnp.concatenate(ranks, axis=1)
    sel = jnp.logical_or(keys > thr, jnp.logical_and(tied, rank <= need))
    o_ref[0] = jnp.where(jnp.logical_and(sel, visible), 0.0, NEG)


def _dsa_sample_select(scores, qi_ht, wi_ht, ki_new, ih, ts, topk):
    bs, _, past = scores.shape
    tail_w = ki_new.shape[1]
    r = ih * SAMPLE_TPAD
    return pl.pallas_call(
        functools.partial(_dsa_sample_select_kernel, topk=topk, ih=ih, ts=ts, past=past),
        grid=(bs,),
        in_specs=[pl.BlockSpec((1, SAMPLE_TPAD, past), lambda b: (b, 0, 0)),
                  pl.BlockSpec((1, r, IDX_DIM), lambda b: (b, 0, 0)),
                  pl.BlockSpec((1, r, wi_ht.shape[2]), lambda b: (b, 0, 0)),
                  pl.BlockSpec((1, tail_w, IDX_DIM), lambda b: (b, 0, 0))],
        out_specs=pl.BlockSpec((1, SAMPLE_TPAD, past + tail_w), lambda b: (b, 0, 0)),
        out_shape=jax.ShapeDtypeStruct((bs, SAMPLE_TPAD, past + tail_w), F32),
        compiler_params=_cparams("parallel"),
        name="dsa_sample_select",
    )(scores, qi_ht, wi_ht, ki_new)


def _sample_flash_update(qbd, kblk, vblk, bias, m_ref, l_ref, acc_ref, nkv):
    rows_per_head = GROUP * SAMPLE_TPAD
    s = lax.dot_general(qbd, kblk, _CONTRACT_LAST, preferred_element_type=F32) * (HEAD_DIM ** -0.5) + bias
    m_old = m_ref[...]
    m_new = jnp.maximum(m_old, jnp.max(s, axis=1, keepdims=True))
    alpha = jnp.exp(m_old - m_new)
    p = jnp.exp(s - m_new)
    l_ref[...] = alpha * l_ref[...] + jnp.sum(p, axis=1, keepdims=True)
    m_ref[...] = m_new
    o_full = jnp.dot(p.astype(BF16), vblk, preferred_element_type=F32)
    o_diag = jnp.concatenate(
        [o_full[n * rows_per_head:(n + 1) * rows_per_head, n * HEAD_DIM:(n + 1) * HEAD_DIM] for n in range(nkv)],
        axis=0)
    acc_ref[...] = alpha * acc_ref[...] + o_diag


def _sample_flash_reset(m_ref, l_ref, acc_ref):
    m_ref[...] = jnp.full(m_ref.shape, NEG, F32)
    l_ref[...] = jnp.zeros(l_ref.shape, F32)
    acc_ref[...] = jnp.zeros(acc_ref.shape, F32)


def _heads_on_lanes(refs, nkv):
    keys = refs[0].shape[0] // nkv
    pages = [jnp.concatenate([r[pl.ds(n, keys, stride=nkv), :] for n in range(nkv)], axis=1).astype(BF16)
             for r in refs]
    return jnp.concatenate(pages, axis=0)


def _flat_pool(pool):
    return pool.reshape(pool.shape[0], pool.shape[1], pool.shape[2] * pool.shape[3], pool.shape[4])


def _tile_query_bias(b8, reps):
    return jnp.concatenate([b8] * reps, axis=0)


def _dsa_sample_attn_kernel(pt_ref, q_ref, bias_ref, tbias_ref, kn_ref, vn_ref, *refs, pp, nkv):
    k_refs, v_refs = refs[:pp], refs[pp:2 * pp]
    o_ref, m_ref, l_ref, acc_ref = refs[2 * pp:]
    p = pl.program_id(1)
    reps = nkv * GROUP

    @pl.when(p == 0)
    def _():
        _sample_flash_reset(m_ref, l_ref, acc_ref)

    _sample_flash_update(q_ref[0], _heads_on_lanes(k_refs, nkv), _heads_on_lanes(v_refs, nkv),
                         _tile_query_bias(bias_ref[0], reps), m_ref, l_ref, acc_ref, nkv)

    @pl.when(p == pl.num_programs(1) - 1)
    def _():
        _sample_flash_update(q_ref[0], kn_ref[0].astype(BF16), vn_ref[0].astype(BF16),
                             _tile_query_bias(tbias_ref[0], reps), m_ref, l_ref, acc_ref, nkv)
        o_ref[0] = acc_ref[...] / l_ref[...]


def _sample_attn_specs(nkv, tail_rows):
    rows = nkv * GROUP * SAMPLE_TPAD
    kvw = nkv * HEAD_DIM
    q_spec = pl.BlockSpec((1, rows, kvw), lambda b, p, pt: (b, 0, 0))
    tail_spec = pl.BlockSpec((1, tail_rows, kvw), lambda b, p, pt: (b, 0, 0))
    out_spec = pl.BlockSpec((1, rows, HEAD_DIM), lambda b, p, pt: (b, 0, 0))
    scratch = [pltpu.VMEM((rows, 1), F32), pltpu.VMEM((rows, 1), F32), pltpu.VMEM((rows, HEAD_DIM), F32)]
    return rows, q_spec, tail_spec, out_spec, scratch


def _dsa_sample_attn(page_table, qbd, bias, k_new, v_new, k_pool, v_pool, layer, nkv):
    bs, npg = page_table.shape
    page = k_pool.shape[2]
    pp = SAMPLE_PP
    rows, q_spec, tail_spec, out_spec, scratch = _sample_attn_specs(nkv, k_new.shape[1])
    grid_spec = pltpu.PrefetchScalarGridSpec(
        num_scalar_prefetch=1,
        grid=(bs, npg // pp),
        in_specs=[q_spec,
                  pl.BlockSpec((1, SAMPLE_TPAD, page * pp), lambda b, p, pt: (b, 0, p)),
                  pl.BlockSpec((1, SAMPLE_TPAD, page), lambda b, p, pt: (b, 0, npg)),
                  tail_spec, tail_spec]
        + _page_specs((page * nkv, HEAD_DIM), pp, layer) + _page_specs((page * nkv, HEAD_DIM), pp, layer),
        out_specs=out_spec,
        scratch_shapes=scratch,
    )
    return pl.pallas_call(
        functools.partial(_dsa_sample_attn_kernel, pp=pp, nkv=nkv),
        grid_spec=grid_spec,
        out_shape=jax.ShapeDtypeStruct((bs, rows, HEAD_DIM), F32),
        compiler_params=_cparams("parallel", "arbitrary"),
        name="dsa_sample_attn",
    )(page_table, qbd, bias, bias, k_new, v_new, *([_flat_pool(k_pool)] * pp), *([_flat_pool(v_pool)] * pp))


KMEAN_BLOCKS = 8


def _moba_kmean_kernel(pt_ref, *refs, nblk, ppb):
    k_refs, o_ref = refs[:nblk * ppb], refs[nblk * ppb]
    for blk in range(nblk):
        acc = jnp.sum(k_refs[blk * ppb][...], axis=0)
        for c in range(1, ppb):
            acc = acc + jnp.sum(k_refs[blk * ppb + c][...], axis=0)
        o_ref[0, blk] = acc * (1.0 / MOBA_BLOCK)


def _moba_sample_kmean(page_table, k_pool, layer, nkv):
    bs, npg = page_table.shape
    page = k_pool.shape[2]
    ppb = MOBA_BLOCK // page
    nb = npg // ppb
    nblk = KMEAN_BLOCKS
    grid_spec = pltpu.PrefetchScalarGridSpec(
        num_scalar_prefetch=1,
        grid=(bs, nb // nblk),
        in_specs=_page_specs((page, nkv, HEAD_DIM), nblk * ppb, layer),
        out_specs=pl.BlockSpec((1, nblk, nkv, HEAD_DIM), lambda b, p, pt: (b, p, 0, 0)),
    )
    return pl.pallas_call(
        functools.partial(_moba_kmean_kernel, nblk=nblk, ppb=ppb),
        grid_spec=grid_spec,
        out_shape=jax.ShapeDtypeStruct((bs, nb, nkv, HEAD_DIM), F32),
        compiler_params=_cparams("parallel", "arbitrary"),
        name="moba_sample_kmean",
    )(page_table, *([k_pool] * (nblk * ppb)))


def _moba_sample_attn_kernel(pt_ref, q_ref, km_ref, kn_ref, vn_ref, *refs, pp, nkv, page, nb, ts):
    k_refs, v_refs = refs[:pp], refs[pp:2 * pp]
    o_ref, m_ref, l_ref, acc_ref, sb_ref = refs[2 * pp:]
    p = pl.program_id(1)
    rows = nkv * GROUP * SAMPLE_TPAD
    ppb = MOBA_BLOCK // page
    blk_iota = lax.broadcasted_iota(I32, (rows, nb), 1)

    @pl.when(p == 0)
    def _():
        _sample_flash_reset(m_ref, l_ref, acc_ref)
        valid = blk_iota >= 0
        gate = lax.dot_general(q_ref[0], km_ref[0].astype(BF16), _CONTRACT_LAST, preferred_element_type=F32)
        sb_ref[...] = _topb_select_bias(gate, valid, blk_iota, nb)

    selb = sb_ref[...]
    cols = []
    for c in range(pp // ppb):
        blk = p * (pp // ppb) + c
        col = jnp.sum(jnp.where(blk_iota == blk, selb, 0.0), axis=1, keepdims=True)
        cols.append(jnp.broadcast_to(col, (rows, MOBA_BLOCK)))
    _sample_flash_update(q_ref[0], _heads_on_lanes(k_refs, nkv), _heads_on_lanes(v_refs, nkv),
                         jnp.concatenate(cols, axis=1), m_ref, l_ref, acc_ref, nkv)

    @pl.when(p == pl.num_programs(1) - 1)
    def _():
        tail_w = kn_ref.shape[1]
        row = lax.broadcasted_iota(I32, (rows, tail_w), 0) & (SAMPLE_TPAD - 1)
        col = lax.broadcasted_iota(I32, (rows, tail_w), 1)
        tb = jnp.where(jnp.logical_and(col <= row, col < ts), 0.0, NEG)
        _sample_flash_update(q_ref[0], kn_ref[0].astype(BF16), vn_ref[0].astype(BF16), tb,
                             m_ref, l_ref, acc_ref, nkv)
        o_ref[0] = acc_ref[...] / l_ref[...]


def _moba_sample_attn(page_table, qbd, kmean, k_new, v_new, k_pool, v_pool, layer, nkv, ts):
    bs, npg = page_table.shape
    page = k_pool.shape[2]
    kvw = nkv * HEAD_DIM
    pp = SAMPLE_PP
    nb = kmean.shape[1]
    rows, q_spec, tail_spec, out_spec, scratch = _sample_attn_specs(nkv, k_new.shape[1])
    grid_spec = pltpu.PrefetchScalarGridSpec(
        num_scalar_prefetch=1,
        grid=(bs, npg // pp),
        in_specs=[q_spec, pl.BlockSpec((1, nb, kvw), lambda b, p, pt: (b, 0, 0)), tail_spec, tail_spec]
        + _page_specs((page * nkv, HEAD_DIM), pp, layer) + _page_specs((page * nkv, HEAD_DIM), pp, layer),
        out_specs=out_spec,
        scratch_shapes=scratch + [pltpu.VMEM((rows, nb), F32)],
    )
    return pl.pallas_call(
        functools.partial(_moba_sample_attn_kernel, pp=pp, nkv=nkv, page=page, nb=nb, ts=ts),
        grid_spec=grid_spec,
        out_shape=jax.ShapeDtypeStruct((bs, rows, HEAD_DIM), F32),
        compiler_params=_cparams("parallel", "arbitrary"),
        name="moba_sample_attn",
    )(page_table, qbd, kmean, k_new, v_new, *([_flat_pool(k_pool)] * pp), *([_flat_pool(v_pool)] * pp))


def _heads_to_kv_major(q, bs, nkv):
    x = q.reshape(bs, SAMPLE_TPAD, nkv, GROUP, HEAD_DIM).transpose(0, 2, 3, 1, 4)
    eye = jnp.eye(nkv, dtype=q.dtype)
    bd = x[:, :, :, :, None, :] * eye[None, :, None, None, :, None]
    return bd.reshape(bs, nkv * GROUP * SAMPLE_TPAD, nkv * HEAD_DIM).astype(BF16)


def _kv_major_to_rows(o, bs, nkv):
    x = o.reshape(bs, nkv, GROUP, SAMPLE_TPAD, HEAD_DIM)
    return x.transpose(0, 3, 1, 2, 4).reshape(bs * SAMPLE_TPAD, nkv * GROUP * HEAD_DIM)


def _pad_rows(x, n):
    return jnp.pad(x, ((0, 0), (0, n - x.shape[1]), (0, 0)))


def _merge_mixers(h, mix, qm, mk, mv, bsz):
    m = h.shape[0]
    ma = _mem_attn(qm.reshape(bsz, m // bsz, -1), mk, mv, qm.dtype).reshape(m, -1)
    return mix.astype(BF16), ma.astype(BF16)


def _finish_layer(hp, hs, merged_p, merged_s, layer, w_o, g_ffn, w_up, w_down):
    hp = _matmul(merged_p, w_o, layer, F32, epilogue="residual", residual=hp)
    hs = _matmul(merged_s, w_o, layer, F32, epilogue="residual", residual=hs)
    up, us = _matmul(_rmsnorm(hp, g_ffn, BF16), w_up, layer, BF16, epilogue="relu2", a2=_rmsnorm(hs, g_ffn, BF16))
    hp = _matmul(up, w_down, layer, F32, epilogue="residual", residual=hp)
    hs = _matmul(us, w_down, layer, F32, epilogue="residual", residual=hs)
    return hp, hs


def kernel(x_prompt, x_sample, cache_dsa_k, cache_dsa_v, cache_dsa_kidx, cache_moba_k, cache_moba_v, cache_mem_k, cache_mem_v, page_table, mem_prompt, norm_mix, norm_mem, w_in_dsa, w_in_moba, w_mem_kv, w_out, norm_ffn, w_up, w_down, norm_final):
    bsz, t, d = x_prompt.shape
    bs, ts, _ = x_sample.shape
    depth = norm_mix.shape[0]
    nh = (3 * d) // (4 * HEAD_DIM)
    nkv = nh // GROUP
    qw, kvw = nh * HEAD_DIM, nkv * HEAD_DIM
    memw = d // 4
    mhd = memw // MEM_HEADS
    ih = d // 128
    npg = page_table.shape[1]
    page = cache_dsa_k.shape[2]
    past = npg * page
    mlen = mem_prompt.shape[1]
    tp = SAMPLE_TPAD

    hp = x_prompt.reshape(bsz * t, d)
    hs = jnp.pad(x_sample, ((0, 0), (0, tp - ts), (0, 0))).reshape(bs * tp, d)
    pos_p = jnp.arange(t, dtype=I32)
    pos_s = past + (jnp.arange(tp, dtype=I32) % ts)
    tabs_p = _rope_tables(pos_p)
    tabs_s = _rope_tables(jnp.tile(pos_s, bs))

    outs = {name: [] for name in ("pdk", "pdv", "pdki", "sdk", "sdv", "sdki", "pmk", "pmv", "smk", "smv", "mk", "mv")}
    mem_flat = mem_prompt.reshape(bsz * mlen, d)

    w_o, wu, wd, w_mem = w_out.astype(BF16), w_up.astype(BF16), w_down.astype(BF16), w_mem_kv.astype(BF16)
    c0 = qw + 2 * kvw + ih * IDX_DIM
    w_dsa, w_moba = w_in_dsa[:, :, :c0].astype(BF16), w_in_moba.astype(BF16)
    w_dsa_tail = jnp.concatenate(
        [w_in_dsa[:, :, c0 + ih:c0 + ih + IDX_DIM], w_in_dsa[:, :, c0 + ih + IDX_DIM:], w_in_dsa[:, :, c0:c0 + ih],
         jnp.zeros(w_in_dsa.shape[:2] + (128 - ih,), F32)], axis=2).astype(BF16)

    for i in range(depth):
        j = i // 2
        mkv = _matmul(_rmsnorm(mem_flat, norm_mem[i], BF16), w_mem, i, F32)
        mk_p = mkv[:, :memw].reshape(bsz, mlen, memw)
        mv_p = mkv[:, memw:].reshape(bsz, mlen, memw)
        outs["mk"].append(mk_p.reshape(bsz, mlen, MEM_HEADS, mhd))
        outs["mv"].append(mv_p.reshape(bsz, mlen, MEM_HEADS, mhd))
        hn_p = _rmsnorm(hp, norm_mix[i], BF16)
        hn_s = _rmsnorm(hs, norm_mix[i], BF16)
        mk_s = cache_mem_k[i].reshape(bs, mlen, memw)
        mv_s = cache_mem_v[i].reshape(bs, mlen, memw)
        if i % 2 == 0:
            q = _matmul(hn_p, w_dsa, j, BF16, n=qw, epilogue="rope", rope=(tabs_p, t))
            qi = _matmul(hn_p, w_dsa, j, BF16, n=ih * IDX_DIM, n0=qw + 2 * kvw, epilogue="rope", rope=(tabs_p, t))
            pkv = _matmul(hn_p, w_dsa, j, F32, n=2 * kvw, n0=qw)
            pb, pb_s = _matmul(hn_p, w_dsa_tail, j, F32, a2=hn_s)
            kf, kb, vf, vb, kif, kib, wi, qm = _dsa_split(pkv, pb, tabs_p, t, nh, nkv, ih, memw, BF16, with_q=False)
            outs["pdk"].append(kf.reshape(bsz, t, nkv, HEAD_DIM))
            outs["pdv"].append(vf.reshape(bsz, t, nkv, HEAD_DIM))
            outs["pdki"].append(kif.reshape(bsz, t, IDX_DIM))
            mix_p = _dsa_prompt(qi, wi, kib, q, kb, vb, bsz, t, nkv, ih, min(DSA_TOPK, t // 4))
            pa_s = _matmul(hn_s, w_dsa, j, F32, n=c0)
            q_s, qi_s, kf, _, vf, _, kif, _, wi_s, qm_s = _dsa_split(pa_s, pb_s, tabs_s, bs * tp, nh, nkv, ih, memw, F32,
                                                                     with_q=True)
            outs["sdk"].append(kf.reshape(bs, tp, nkv, HEAD_DIM)[:, :ts])
            outs["sdv"].append(vf.reshape(bs, tp, nkv, HEAD_DIM)[:, :ts])
            outs["sdki"].append(kif.reshape(bs, tp, IDX_DIM)[:, :ts])
            qi_ht = qi_s.reshape(bs, tp, ih, IDX_DIM).transpose(0, 2, 1, 3).reshape(bs, ih * tp, IDX_DIM)
            wi_ht = wi_s.reshape(bs, tp, 128)[:, :, :ih].transpose(0, 2, 1).reshape(bs, ih * tp, 1)
            wi_ht = jnp.broadcast_to(wi_ht, (bs, ih * tp, page))
            scores = _dsa_sample_index(page_table, qi_ht, wi_ht, cache_dsa_kidx, j, ih)
            ki_new = _pad_rows(kif.reshape(bs, tp, IDX_DIM), page)
            bias = _dsa_sample_select(scores, qi_ht, wi_ht, ki_new, ih, ts, min(DSA_TOPK, (past + ts) // 4))
            o_s = _dsa_sample_attn(page_table, _heads_to_kv_major(q_s, bs, nkv), bias,
                                   _pad_rows(kf.reshape(bs, tp, kvw), page), _pad_rows(vf.reshape(bs, tp, kvw), page),
                                   cache_dsa_k, cache_dsa_v, j, nkv)
            mix_s = _kv_major_to_rows(o_s, bs, nkv)
        else:
            q = _matmul(hn_p, w_moba, j, BF16, n=qw, epilogue="rope", rope=(tabs_p, t))
            p_rest = _matmul(hn_p, w_moba, j, F32, n=2 * kvw + memw, n0=qw)
            kf, kb, vf, vb, qm, kmean = _moba_split(p_rest, tabs_p, t, nh, nkv, memw, BF16, with_q=False)
            outs["pmk"].append(kf.reshape(bsz, t, nkv, HEAD_DIM))
            outs["pmv"].append(vf.reshape(bsz, t, nkv, HEAD_DIM))
            mix_p = _moba_prompt(q, kb, vb, kmean, bsz, t, nkv)
            p_s = _matmul(hn_s, w_moba, j, F32)
            q_s, kf, _, vf, _, qm_s, _ = _moba_split(p_s, tabs_s, bs * tp, nh, nkv, memw, F32, with_q=True)
            outs["smk"].append(kf.reshape(bs, tp, nkv, HEAD_DIM)[:, :ts])
            outs["smv"].append(vf.reshape(bs, tp, nkv, HEAD_DIM)[:, :ts])
            kmean_s = _moba_sample_kmean(page_table, cache_moba_k, j, nkv)
            kmean_s = kmean_s.reshape(bs, kmean_s.shape[1], kvw)
            o_s = _moba_sample_attn(page_table, _heads_to_kv_major(q_s, bs, nkv), kmean_s,
                                    _pad_rows(kf.reshape(bs, tp, kvw), page), _pad_rows(vf.reshape(bs, tp, kvw), page),
                                    cache_moba_k, cache_moba_v, j, nkv, ts)
            mix_s = _kv_major_to_rows(o_s, bs, nkv)
        hp, hs = _finish_layer(hp, hs, _merge_mixers(hp, mix_p, qm, mk_p, mv_p, bsz),
                               _merge_mixers(hs, mix_s, qm_s, mk_s, mv_s, bs), i, w_o, norm_ffn[i], wu, wd)

    y_prompt = _rmsnorm(hp, norm_final, F32).reshape(bsz, t, d)
    y_sample = _rmsnorm(hs, norm_final, F32).reshape(bs, tp, d)[:, :ts]
    st = jnp.stack
    return (y_prompt, y_sample, st(outs["pdk"]), st(outs["pdv"]), st(outs["pdki"]), st(outs["pmk"]), st(outs["pmv"]),
            st(outs["mk"]), st(outs["mv"]), st(outs["sdk"]), st(outs["sdv"]), st(outs["sdki"]),
            st(outs["smk"]), st(outs["smv"]))
```

```python
import functools
import math

import jax
import jax.numpy as jnp
from jax import lax
from jax.experimental import pallas as pl
from jax.experimental.pallas import tpu as pltpu

F32 = jnp.float32
BF16 = jnp.bfloat16
I32 = jnp.int32

HEAD_DIM = 128
GROUP = 3
ROT_DIM = 32
ROPE_THETA = 500000.0
IDX_DIM = 128
MEM_HEADS = 4
DSA_TOPK = 256
MOBA_BLOCK = 256
MOBA_TOPB = 3
EPS = 1e-6
NEG = -1e30
LOG2E = 1.4426950408889634
SOFTMAX_C2 = (HEAD_DIM ** -0.5) * LOG2E
QTILE = 256
SAMPLE_TPAD = 8
VMEM_LIMIT = 60 * 1024 * 1024

_CONTRACT_LAST = (((1,), (1,)), ((), ()))


def _cparams(*sem):
    return pltpu.CompilerParams(dimension_semantics=sem, vmem_limit_bytes=VMEM_LIMIT)


def _pick_block(n, pref, align=128):
    if n <= pref:
        return n
    b = (pref // align) * align
    while b >= align:
        if n % b == 0:
            return b
        b -= align
    return n


def _rmsnorm_kernel(x_ref, g_ref, o_ref):
    x = x_ref[...]
    ms = jnp.mean(x * x, axis=-1, keepdims=True)
    o_ref[...] = ((x * lax.rsqrt(ms + EPS)) * g_ref[...]).astype(o_ref.dtype)


def _rmsnorm(x, g, out_dtype):
    m, d = x.shape
    bm = _pick_block(m, 256, 8)
    return pl.pallas_call(
        _rmsnorm_kernel,
        grid=(m // bm,),
        in_specs=[pl.BlockSpec((bm, d), lambda i: (i, 0)), pl.BlockSpec((1, d), lambda i: (0, 0))],
        out_specs=pl.BlockSpec((bm, d), lambda i: (i, 0)),
        out_shape=jax.ShapeDtypeStruct((m, d), out_dtype),
        compiler_params=_cparams("parallel"),
        name="rmsnorm",
    )(x, g.reshape(1, d).astype(F32))


def _mm_tile(a_refs, w, r_ref, o_ref, k, nk, epilogue, tab_refs=None):
    part, off = None, 0
    for a_ref in a_refs:
        ka = a_ref.shape[1]
        d = jnp.dot(a_ref[...], w[off:off + ka], preferred_element_type=F32)
        part = d if part is None else part + d
        off += ka

    def finish(acc):
        if epilogue == "rope":
            c, sm, sp = (t[...] for t in tab_refs)
            for h in range(acc.shape[1] // HEAD_DIM):
                sl = slice(h * HEAD_DIM, (h + 1) * HEAD_DIM)
                o_ref[:, sl] = _rope_head(acc[:, sl], c, sm, sp).astype(o_ref.dtype)
            return
        if epilogue == "relu2":
            r = jnp.maximum(acc, 0.0)
            acc = r * r
        elif epilogue == "residual":
            acc = r_ref[...] + acc
        o_ref[...] = acc.astype(o_ref.dtype)

    if nk == 1:
        finish(part)
        return

    @pl.when(k == 0)
    def _():
        o_ref[...] = part

    @pl.when(jnp.logical_and(k > 0, k < nk - 1))
    def _():
        o_ref[...] += part

    @pl.when(k == nk - 1)
    def _():
        finish(o_ref[...] + part)


def _mm_kernel(*refs, nk, epilogue, dual, nparts):
    refs = list(refs)
    a_refs = [refs.pop(0) for _ in range(nparts)]
    w_ref = refs.pop(0)
    r_ref = refs.pop(0) if epilogue == "residual" else None
    tab_refs = [refs.pop(0) for _ in range(3)] if epilogue == "rope" else None
    a2_ref = refs.pop(0) if dual else None
    r2_ref = refs.pop(0) if dual and epilogue == "residual" else None
    o_ref = refs.pop(0)
    o2_ref = refs.pop(0) if dual else None
    k = pl.program_id(2)
    w = w_ref[...]
    _mm_tile(a_refs, w, r_ref, o_ref, k, nk, epilogue, tab_refs)
    if dual:
        @pl.when(pl.program_id(1) == 0)
        def _():
            _mm_tile([a2_ref], w, r2_ref, o2_ref, k, nk, epilogue)


def _matmul(a, w, layer, out_dtype, epilogue=None, residual=None, n=None, n0=0, a2=None, residual2=None,
            rope=None, bm_pref=1024, bn_pref=1024, bk_pref=4096):
    a_parts = a if isinstance(a, tuple) else (a,)
    m = a_parts[0].shape[0]
    kdim = sum(p.shape[1] for p in a_parts)
    n = w.shape[2] if n is None else n
    bm = _pick_block(rope[1] if epilogue == "rope" else m, bm_pref, 8)
    bn = _pick_block(math.gcd(n, n0) if n0 else n, bn_pref)
    bk = _pick_block(kdim, bk_pref)
    nk = kdim // bk
    dual = a2 is not None
    j0 = n0 // bn
    assert nk == 1 or out_dtype == F32, "a split contraction accumulates in the f32 output block"
    assert nk == 1 or (len(a_parts) == 1 and epilogue != "rope")
    if len(a_parts) == 1:
        in_specs = [pl.BlockSpec((bm, bk), lambda j, i, k: (i, k))]
    else:
        in_specs = [pl.BlockSpec((bm, p.shape[1]), lambda j, i, k: (i, 0)) for p in a_parts]
    in_specs.append(pl.BlockSpec((None, bk, bn), lambda j, i, k: (layer, k, j + j0)))
    args = list(a_parts) + [w]
    out_specs = [pl.BlockSpec((bm, bn), lambda j, i, k: (i, j))]
    out_shape = [jax.ShapeDtypeStruct((m, n), out_dtype)]
    if epilogue == "residual":
        in_specs.append(pl.BlockSpec((bm, bn), lambda j, i, k: (i, j)))
        args.append(residual)
    if epilogue == "rope":
        nt = rope[1] // bm
        in_specs += [pl.BlockSpec((bm, HEAD_DIM), lambda j, i, k: (i % nt, 0))] * 3
        args += list(rope[0])
    if dual:
        m2 = a2.shape[0]
        in_specs.append(pl.BlockSpec((m2, bk), lambda j, i, k: (0, k)))
        args.append(a2)
        if epilogue == "residual":
            in_specs.append(pl.BlockSpec((m2, bn), lambda j, i, k: (0, j)))
            args.append(residual2)
        out_specs.append(pl.BlockSpec((m2, bn), lambda j, i, k: (0, j)))
        out_shape.append(jax.ShapeDtypeStruct((m2, n), out_dtype))
    out = pl.pallas_call(
        functools.partial(_mm_kernel, nk=nk, epilogue=epilogue, dual=dual, nparts=len(a_parts)),
        grid=(n // bn, m // bm, nk),
        in_specs=in_specs,
        out_specs=out_specs,
        out_shape=out_shape,
        compiler_params=_cparams("parallel", "arbitrary" if dual else "parallel", "arbitrary"),
        name="matmul",
    )(*args)
    return tuple(out) if dual else out[0]


def _rope_tables(pos):
    half = ROT_DIM // 2
    inv = ROPE_THETA ** (-jnp.arange(half, dtype=F32) / half)
    ang = pos.astype(F32)[:, None] * inv[None, :]
    cos, sin = jnp.cos(ang), jnp.sin(ang)
    r = pos.shape[0]
    z16 = jnp.zeros((r, half), F32)
    zrest = jnp.zeros((r, HEAD_DIM - ROT_DIM), F32)
    c = jnp.concatenate([cos, cos, jnp.ones((r, HEAD_DIM - ROT_DIM), F32)], axis=1)
    sm = jnp.concatenate([-sin, z16, zrest], axis=1)
    sp = jnp.concatenate([z16, sin, zrest], axis=1)
    return c, sm, sp


def _rope_head(x, c, sm, sp):
    half = ROT_DIM // 2
    return x * c + pltpu.roll(x, HEAD_DIM - half, 1) * sm + pltpu.roll(x, half, 1) * sp


def _dsa_split_kernel(pa_ref, pb_ref, c_ref, sm_ref, sp_ref, *out_refs, nh, nkv, ih, memw, with_q):
    c, sm, sp = c_ref[...], sm_ref[...], sp_ref[...]
    qw, kvw = nh * HEAD_DIM, nkv * HEAD_DIM
    out_refs = list(out_refs)
    kv0 = 0
    if with_q:
        q_ref, qi_ref = out_refs.pop(0), out_refs.pop(0)
        for h in range(nh):
            sl = slice(h * HEAD_DIM, (h + 1) * HEAD_DIM)
            q_ref[:, sl] = _rope_head(pa_ref[:, sl], c, sm, sp).astype(q_ref.dtype)
        base = qw + 2 * kvw
        for h in range(ih):
            sl = slice(h * IDX_DIM, (h + 1) * IDX_DIM)
            qi_ref[:, sl] = _rope_head(pa_ref[:, base + h * IDX_DIM: base + (h + 1) * IDX_DIM], c, sm, sp).astype(
                qi_ref.dtype)
        kv0 = qw
    kf_ref, kb_ref, vf_ref, vb_ref, kif_ref, kib_ref, wi_ref, qm_ref = out_refs
    for h in range(nkv):
        sl = slice(h * HEAD_DIM, (h + 1) * HEAD_DIM)
        kr = _rope_head(pa_ref[:, kv0 + h * HEAD_DIM: kv0 + (h + 1) * HEAD_DIM], c, sm, sp)
        kf_ref[:, sl] = kr
        kb_ref[:, sl] = kr.astype(kb_ref.dtype)
    v = pa_ref[:, kv0 + kvw: kv0 + 2 * kvw]
    vf_ref[...] = v
    vb_ref[...] = v.astype(vb_ref.dtype)
    ki = _rope_head(pb_ref[:, 0:IDX_DIM], c, sm, sp)
    kif_ref[...] = ki
    kib_ref[...] = ki.astype(kib_ref.dtype)
    qm_ref[...] = pb_ref[:, IDX_DIM:IDX_DIM + memw].astype(qm_ref.dtype)
    wi_ref[...] = pb_ref[:, IDX_DIM + memw:IDX_DIM + memw + 128] * ((ih * IDX_DIM) ** -0.5)


def _dsa_split(pa, pb, tabs, rows_per_seq, nh, nkv, ih, memw, act_dtype, with_q):
    m = pa.shape[0]
    bm = _pick_block(rows_per_seq, 128, 8)
    nt = rows_per_seq // bm
    qw, kvw = nh * HEAD_DIM, nkv * HEAD_DIM
    row = lambda w: pl.BlockSpec((bm, w), lambda i: (i, 0))
    tab = pl.BlockSpec((bm, HEAD_DIM), lambda i: (i % nt, 0))
    shapes = [(qw, act_dtype), (ih * IDX_DIM, act_dtype)] if with_q else []
    shapes += [(kvw, F32), (kvw, act_dtype), (kvw, F32), (kvw, act_dtype),
               (IDX_DIM, F32), (IDX_DIM, act_dtype), (128, F32), (memw, act_dtype)]
    return pl.pallas_call(
        functools.partial(_dsa_split_kernel, nh=nh, nkv=nkv, ih=ih, memw=memw, with_q=with_q),
        grid=(m // bm,),
        in_specs=[row(pa.shape[1]), row(pb.shape[1]), tab, tab, tab],
        out_specs=[row(w) for w, _ in shapes],
        out_shape=[jax.ShapeDtypeStruct((m, w), dt) for w, dt in shapes],
        compiler_params=_cparams("parallel"),
        name="dsa_split",
    )(pa, pb, *tabs)


def _moba_split_kernel(p_ref, c_ref, sm_ref, sp_ref, *out_refs, nh, nkv, memw, with_q):
    c, sm, sp = c_ref[...], sm_ref[...], sp_ref[...]
    qw, kvw = nh * HEAD_DIM, nkv * HEAD_DIM
    out_refs = list(out_refs)
    kv0 = 0
    if with_q:
        q_ref = out_refs.pop(0)
        for h in range(nh):
            sl = slice(h * HEAD_DIM, (h + 1) * HEAD_DIM)
            q_ref[:, sl] = _rope_head(p_ref[:, sl], c, sm, sp).astype(q_ref.dtype)
        kv0 = qw
    kf_ref, kb_ref, vf_ref, vb_ref, qm_ref, km_ref = out_refs
    rows = p_ref.shape[0]
    for h in range(nkv):
        sl = slice(h * HEAD_DIM, (h + 1) * HEAD_DIM)
        kr = _rope_head(p_ref[:, kv0 + h * HEAD_DIM: kv0 + (h + 1) * HEAD_DIM], c, sm, sp)
        kf_ref[:, sl] = kr
        kb_ref[:, sl] = kr.astype(kb_ref.dtype)
        km_ref[0, :, sl] = jnp.sum(kr, axis=0, keepdims=True) * (1.0 / rows)
    v = p_ref[:, kv0 + kvw: kv0 + 2 * kvw]
    vf_ref[...] = v
    vb_ref[...] = v.astype(vb_ref.dtype)
    qm_ref[...] = p_ref[:, kv0 + 2 * kvw: kv0 + 2 * kvw + memw].astype(qm_ref.dtype)


def _moba_split(p, tabs, rows_per_seq, nh, nkv, memw, act_dtype, with_q):
    m = p.shape[0]
    bm = _pick_block(rows_per_seq, MOBA_BLOCK, 8)
    nt = rows_per_seq // bm
    qw, kvw = nh * HEAD_DIM, nkv * HEAD_DIM
    row = lambda w: pl.BlockSpec((bm, w), lambda i: (i, 0))
    tab = pl.BlockSpec((bm, HEAD_DIM), lambda i: (i % nt, 0))
    shapes = [(qw, act_dtype)] if with_q else []
    shapes += [(kvw, F32), (kvw, act_dtype), (kvw, F32), (kvw, act_dtype), (memw, act_dtype)]
    return pl.pallas_call(
        functools.partial(_moba_split_kernel, nh=nh, nkv=nkv, memw=memw, with_q=with_q),
        grid=(m // bm,),
        in_specs=[row(p.shape[1]), tab, tab, tab],
        out_specs=[row(w) for w, _ in shapes] + [pl.BlockSpec((1, 1, kvw), lambda i: (i, 0, 0))],
        out_shape=[jax.ShapeDtypeStruct((m, w), dt) for w, dt in shapes]
        + [jax.ShapeDtypeStruct((m // bm, 1, kvw), F32)],
        compiler_params=_cparams("parallel"),
        name="moba_split",
    )(p, *tabs)


def _sortable_key(x):
    bits = pltpu.bitcast(x, I32)
    return jnp.where(bits < 0, bits ^ jnp.int32(0x7FFFFFFF), bits)


def _kth_largest_key(count_ge, shape, k):
    imin = jnp.int32(-2 ** 31)
    c0 = count_ge(jnp.zeros(shape, I32))
    thr = jnp.where(c0 >= k, jnp.int32(0), imin)

    def bit_body(it, thr):
        cand = thr + jnp.left_shift(jnp.int32(1), jnp.int32(30) - it)
        return jnp.where(count_ge(cand) >= k, cand, thr)

    return lax.fori_loop(0, 31, bit_body, thr)


def _lane_fold(x, acc, op):
    for w in range(x.shape[1] // 128):
        acc = op(acc, x[:, w * 128:(w + 1) * 128])
    return acc


def _chunk_loop(n, body):
    def pair(c2, _):
        body(2 * c2)
        body(2 * c2 + 1)
        return 0

    def single(c, _):
        body(c)
        return 0

    npair = n // 2
    lax.fori_loop(0, npair, pair, 0)
    lax.fori_loop(2 * npair, n, single, 0)


def _masked_attention(nheads, n_dyn, chunk_logits, load_v, mx_ref, lsum_ref, acc_ref, own=None):
    rows = mx_ref.shape[1]
    for n in range(nheads):
        mx = jnp.full((rows, 128), NEG, F32)
        if own is not None:
            mx = _lane_fold(own[0][n], mx, jnp.maximum)
        mx_ref[n] = mx

    def pass_a(c):
        of_head = chunk_logits(c)
        for n in range(nheads):
            mx_ref[n] = _lane_fold(of_head(n), mx_ref[n], jnp.maximum)

    _chunk_loop(n_dyn, pass_a)
    for n in range(nheads):
        mx_ref[n] = jnp.broadcast_to(jnp.max(mx_ref[n], axis=1, keepdims=True), (rows, 128))

    def accumulate(n, s, vb, first=False):
        mb = mx_ref[n]
        p = jnp.exp2(s - jnp.concatenate([mb] * (s.shape[1] // 128), axis=1))
        pv = jnp.dot(p.astype(BF16), vb, preferred_element_type=F32)
        if first:
            lsum_ref[n] = _lane_fold(p, jnp.zeros((rows, 128), F32), jnp.add)
            acc_ref[n] = pv
        else:
            lsum_ref[n] = _lane_fold(p, lsum_ref[n], jnp.add)
            acc_ref[n] = acc_ref[n] + pv

    for n in range(nheads):
        if own is not None:
            accumulate(n, own[0][n], own[1](n), first=True)
        else:
            lsum_ref[n] = jnp.zeros((rows, 128), F32)
            acc_ref[n] = jnp.zeros((rows, HEAD_DIM), F32)

    def pass_b(c):
        of_head = chunk_logits(c)
        for n in range(nheads):
            accumulate(n, of_head(n), load_v(c, n))

    _chunk_loop(n_dyn, pass_b)


def _write_heads(o_ref, lsum_ref, acc_ref, nkv, tq):
    for n in range(nkv):
        out = acc_ref[n] / jnp.sum(lsum_ref[n], axis=1, keepdims=True)
        for g in range(GROUP):
            o_ref[:, (n * GROUP + g) * HEAD_DIM:(n * GROUP + g + 1) * HEAD_DIM] = (
                out[g * tq:(g + 1) * tq].astype(o_ref.dtype))


def _attn_state_scratch(nkv):
    return [pltpu.VMEM((nkv, GROUP * QTILE, 128), F32)] * 3


def _dsa_prompt_kernel(qi_ref, wi_ref, ki_ref, q_ref, k_ref, v_ref, o_ref, key_ref, bias_ref, q3_ref,
                       mx_ref, lsum_ref, acc_ref, *, topk, nkv, ih):
    i = pl.program_id(1)
    tq = ck = QTILE
    key_i = lax.broadcasted_iota(I32, (ck, tq), 0)
    qry_i = lax.broadcasted_iota(I32, (ck, tq), 1)
    wi_t = wi_ref[...].T

    def causal_ok(kc):
        return key_i <= qry_i + jnp.where(kc < i, ck, 0)

    def idx_body(kc, _):
        off = pl.multiple_of(kc * ck, ck)
        kic = ki_ref[0, pl.ds(off, ck), :]
        acc = jnp.zeros((ck, tq), F32)
        for h in range(ih):
            s = lax.dot_general(kic, qi_ref[:, h * IDX_DIM:(h + 1) * IDX_DIM], _CONTRACT_LAST,
                                preferred_element_type=F32)
            acc = acc + jnp.maximum(s, 0.0) * wi_t[h:h + 1, :]
        key_ref[kc] = _sortable_key(jnp.where(causal_ok(kc), acc, -jnp.inf))
        return 0

    lax.fori_loop(0, i + 1, idx_body, 0)

    def count_ge(cand):
        def body(kc, acc):
            ge = jnp.where(key_ref[kc] >= cand, 1.0, 0.0)
            return acc + jnp.sum(ge.reshape(ck // 8, 8, tq), axis=0)
        acc = lax.fori_loop(0, i + 1, body, jnp.zeros((8, tq), F32))
        return jnp.sum(acc, axis=0, keepdims=True)

    thr = _kth_largest_key(count_ge, (1, tq), topk)
    need = topk - count_ge(thr + 1)
    incl_prefix = (lax.broadcasted_iota(I32, (ck, ck), 1) <= lax.broadcasted_iota(I32, (ck, ck), 0)).astype(BF16)

    def bias_body(kc, seen):
        key = key_ref[kc]
        tied = key == thr
        tied_f = jnp.where(tied, 1.0, 0.0)
        rank = seen + jnp.dot(incl_prefix, tied_f.astype(BF16), preferred_element_type=F32)
        sel = jnp.logical_or(key > thr, jnp.logical_and(tied, rank <= need))
        bias_ref[kc] = jnp.where(jnp.logical_and(sel, causal_ok(kc)), 0.0, NEG).T
        return seen + jnp.sum(jnp.sum(tied_f.reshape(ck // 8, 8, tq), axis=0), axis=0, keepdims=True)

    lax.fori_loop(0, i + 1, bias_body, jnp.zeros((1, tq), F32))

    for n in range(nkv):
        q3_ref[n] = jnp.concatenate(
            [q_ref[:, (n * GROUP + g) * HEAD_DIM:(n * GROUP + g + 1) * HEAD_DIM] for g in range(GROUP)], axis=0)

    def chunk(ref, kc, n):
        return ref[0, pl.ds(pl.multiple_of(kc * ck, ck), ck), n * HEAD_DIM:(n + 1) * HEAD_DIM]

    def chunk_logits(kc):
        bias = jnp.concatenate([bias_ref[kc]] * GROUP, axis=0)

        def of_head(n):
            s = lax.dot_general(q3_ref[n], chunk(k_ref, kc, n), _CONTRACT_LAST, preferred_element_type=F32)
            return s * SOFTMAX_C2 + bias
        return of_head

    _masked_attention(nkv, i + 1, chunk_logits, functools.partial(chunk, v_ref), mx_ref, lsum_ref, acc_ref)
    _write_heads(o_ref, lsum_ref, acc_ref, nkv, tq)


def _dsa_prompt(qi, wi, ki, q, k, v, bsz, t, nkv, ih, topk):
    nt = t // QTILE
    qw = nkv * GROUP * HEAD_DIM
    kvw = nkv * HEAD_DIM
    rows = lambda w: pl.BlockSpec((QTILE, w), lambda b, i: (b * nt + i, 0))
    seq = lambda w: pl.BlockSpec((1, t, w), lambda b, i: (b, 0, 0))
    return pl.pallas_call(
        functools.partial(_dsa_prompt_kernel, topk=topk, nkv=nkv, ih=ih),
        grid=(bsz, nt),
        in_specs=[rows(ih * IDX_DIM), rows(128), seq(IDX_DIM), rows(qw), seq(kvw), seq(kvw)],
        out_specs=rows(qw),
        out_shape=jax.ShapeDtypeStruct((bsz * t, qw), BF16),
        scratch_shapes=[pltpu.VMEM((nt, QTILE, QTILE), I32), pltpu.VMEM((nt, QTILE, QTILE), F32),
                        pltpu.VMEM((nkv, GROUP * QTILE, HEAD_DIM), BF16)] + _attn_state_scratch(nkv),
        compiler_params=_cparams("parallel", "arbitrary"),
        name="dsa_prompt_attn",
    )(qi, wi, ki.reshape(bsz, t, IDX_DIM), q, k.reshape(bsz, t, kvw), v.reshape(bsz, t, kvw))


def _topb_select_bias(gate, valid, blk_iota, nblk):
    g = jnp.where(valid, gate, -jnp.inf)
    rank = jnp.zeros(g.shape, I32)
    for m in range(nblk):
        gm = g[:, m:m + 1]
        beats = jnp.logical_or(gm > g, jnp.logical_and(gm == g, blk_iota > m))
        rank = rank + beats.astype(I32)
    sel = jnp.logical_and(valid, rank < MOBA_TOPB)
    return jnp.where(sel, 0.0, NEG)


def _topb_select_rows(gate_t, valid_t, nblk):
    g = jnp.where(valid_t, gate_t, -jnp.inf)
    blk = lax.broadcasted_iota(I32, g.shape, 0)
    rank = jnp.zeros(g.shape, I32)
    for m in range(nblk):
        gm = g[m:m + 1, :]
        beats = jnp.logical_or(gm > g, jnp.logical_and(gm == g, blk > m))
        rank = rank + beats.astype(I32)
    return jnp.logical_and(valid_t, rank < MOBA_TOPB)


def _moba_prompt_kernel(q_ref, k_ref, v_ref, km_ref, o_ref, qa_ref, so_ref, mx_ref, lsum_ref, acc_ref, *, nkv, nblk):
    j = pl.program_id(1)
    tq = blk = QTILE
    rows = GROUP * tq
    row = lax.broadcasted_iota(I32, (tq, blk), 0)
    col = lax.broadcasted_iota(I32, (tq, blk), 1)
    cb = jnp.where(col <= row, 0.0, NEG)
    causal_bias = jnp.concatenate([cb] * GROUP, axis=0)
    valid_t = lax.broadcasted_iota(I32, (nblk, rows), 0) < j
    lane = lax.broadcasted_iota(I32, (blk, HEAD_DIM), 1)
    km_pad = jnp.zeros((16 - nblk % 16, HEAD_DIM), F32)

    def chunk(ref, m, n):
        return ref[0, pl.ds(pl.multiple_of(m * blk, blk), blk), n * HEAD_DIM:(n + 1) * HEAD_DIM]

    for n in range(nkv):
        lanes = slice(n * HEAD_DIM, (n + 1) * HEAD_DIM)
        q3 = jnp.concatenate(
            [q_ref[:, (n * GROUP + g) * HEAD_DIM:(n * GROUP + g + 1) * HEAD_DIM] for g in range(GROUP)], axis=0)
        km = jnp.concatenate([km_ref[0, :, lanes], km_pad], axis=0).astype(BF16)
        gate_t = lax.dot_general(km, q3, _CONTRACT_LAST, preferred_element_type=F32)[:nblk]
        selb_t = jnp.where(_topb_select_rows(gate_t, valid_t, nblk), 0.0, NEG)
        selb = jnp.concatenate([selb_t, jnp.zeros((HEAD_DIM - nblk, rows), F32)], axis=0).T
        qa_ref[n] = jnp.concatenate([q3, selb.astype(BF16)], axis=1)
        own_s = lax.dot_general(q3, chunk(k_ref, j, n), _CONTRACT_LAST, preferred_element_type=F32)
        so_ref[n] = own_s * SOFTMAX_C2 + causal_bias

    def chunk_logits(m):
        onehot = jnp.where(lane == m, 1.0, 0.0).astype(BF16)

        def of_head(n):
            k_aug = jnp.concatenate([chunk(k_ref, m, n), onehot], axis=1)
            return lax.dot_general(qa_ref[n], k_aug, _CONTRACT_LAST, preferred_element_type=F32) * SOFTMAX_C2
        return of_head

    _masked_attention(nkv, j, chunk_logits, functools.partial(chunk, v_ref), mx_ref, lsum_ref, acc_ref,
                      own=(so_ref, lambda n: chunk(v_ref, j, n)))
    _write_heads(o_ref, lsum_ref, acc_ref, nkv, tq)


def _moba_prompt(q, k, v, kmean, bsz, t, nkv):
    nt = t // QTILE
    qw = nkv * GROUP * HEAD_DIM
    kvw = nkv * HEAD_DIM
    rows = lambda w: pl.BlockSpec((QTILE, w), lambda b, i: (b * nt + i, 0))
    seq = lambda w: pl.BlockSpec((1, t, w), lambda b, i: (b, 0, 0))
    return pl.pallas_call(
        functools.partial(_moba_prompt_kernel, nkv=nkv, nblk=nt),
        grid=(bsz, nt),
        in_specs=[rows(qw), seq(kvw), seq(kvw), pl.BlockSpec((1, nt, kvw), lambda b, i: (b, 0, 0))],
        out_specs=rows(qw),
        out_shape=jax.ShapeDtypeStruct((bsz * t, qw), BF16),
        scratch_shapes=[pltpu.VMEM((nkv, GROUP * QTILE, 2 * HEAD_DIM), BF16),
                        pltpu.VMEM((nkv, GROUP * QTILE, QTILE), F32)] + _attn_state_scratch(nkv),
        compiler_params=_cparams("parallel", "arbitrary"),
        name="moba_prompt_attn",
    )(q, k.reshape(bsz, t, kvw), v.reshape(bsz, t, kvw), kmean.reshape(bsz, nt, kvw))


def _mem_attn_kernel(q_ref, mk_ref, mv_ref, o_ref, *, hd):
    scale = hd ** -0.5
    for c in range(MEM_HEADS):
        lanes = slice(c * hd, (c + 1) * hd)
        q = q_ref[0, :, lanes].astype(BF16)
        mk = mk_ref[0, :, lanes].astype(BF16)
        mv = mv_ref[0, :, lanes].astype(BF16)
        s = lax.dot_general(q, mk, _CONTRACT_LAST, preferred_element_type=F32) * scale
        p = jnp.exp(s - jnp.max(s, axis=1, keepdims=True))
        l = jnp.sum(p, axis=1, keepdims=True)
        o = jnp.dot(p.astype(BF16), mv, preferred_element_type=F32) / l
        o_ref[0, :, lanes] = o.astype(o_ref.dtype)


def _mem_attn(qm, mk, mv, out_dtype):
    bsz, t, w = qm.shape
    mlen = mk.shape[1]
    tq = _pick_block(t, 512, 8)
    return pl.pallas_call(
        functools.partial(_mem_attn_kernel, hd=w // MEM_HEADS),
        grid=(bsz, t // tq),
        in_specs=[pl.BlockSpec((1, tq, w), lambda b, i: (b, i, 0)),
                  pl.BlockSpec((1, mlen, w), lambda b, i: (b, 0, 0)),
                  pl.BlockSpec((1, mlen, w), lambda b, i: (b, 0, 0))],
        out_specs=pl.BlockSpec((1, tq, w), lambda b, i: (b, i, 0)),
        out_shape=jax.ShapeDtypeStruct((bsz, t, w), out_dtype),
        compiler_params=_cparams("parallel", "parallel"),
        name="mem_attn",
    )(qm, mk, mv)


SAMPLE_PP = 16


def _page_specs(shape_tail, pp, layer):
    nd = len(shape_tail)
    return [pl.BlockSpec((None, None) + shape_tail,
                         lambda b, p, pt, c=c: (layer, pt[b, p * pp + c]) + (0,) * nd)
            for c in range(pp)]


def _head_sum(w, ih):
    acc = w[0:SAMPLE_TPAD]
    for h in range(1, ih):
        acc = acc + w[h * SAMPLE_TPAD:(h + 1) * SAMPLE_TPAD]
    return acc


def _dsa_sample_select_kernel(pt_ref, qi_ref, wi_ref, kin_ref, *refs, pp, page, topk, ih, ts, past):
    page_refs, o_ref, sc_ref = refs[:pp], refs[pp], refs[pp + 1]
    p = pl.program_id(1)
    nsteps = past // (pp * page)
    qi = qi_ref[0].astype(BF16)
    wi = wi_ref[0]
    for c in range(pp):
        kp = page_refs[c][...].astype(BF16)
        s = lax.dot_general(qi, kp, _CONTRACT_LAST, preferred_element_type=F32)
        sc_ref[p, :, c * page:(c + 1) * page] = _head_sum(jnp.maximum(s, 0.0) * wi, ih)

    @pl.when(p == nsteps - 1)
    def _():
        _dsa_sample_select_tail(sc_ref, qi_ref, wi_ref, kin_ref, o_ref, nsteps, topk, ih, ts, past)


def _dsa_sample_select_tail(sc_ref, qi_ref, wi_ref, kin_ref, o_ref, nsteps, topk, ih, ts, past):
    tail_w = kin_ref.shape[1]
    s = lax.dot_general(qi_ref[0].astype(BF16), kin_ref[0].astype(BF16), _CONTRACT_LAST,
                        preferred_element_type=F32)
    tail = _head_sum(jnp.maximum(s, 0.0) * wi_ref[0][:, :tail_w], ih)
    row = lax.broadcasted_iota(I32, (SAMPLE_TPAD, tail_w), 0)
    col = lax.broadcasted_iota(I32, (SAMPLE_TPAD, tail_w), 1)
    tail_ok = jnp.logical_and(col <= row, col < ts)
    full = jnp.concatenate([sc_ref[g] for g in range(nsteps)] + [jnp.where(tail_ok, tail, -jnp.inf)], axis=1)
    keys = _sortable_key(full)
    width = past + tail_w
    col_f = lax.broadcasted_iota(I32, (SAMPLE_TPAD, width), 1)
    row_f = lax.broadcasted_iota(I32, (SAMPLE_TPAD, width), 0)
    visible = jnp.logical_or(col_f < past, jnp.logical_and(col_f - past <= row_f, col_f - past < ts))

    def count_ge(cand):
        return jnp.sum((keys >= cand).astype(F32), axis=1, keepdims=True)

    thr = _kth_largest_key(count_ge, (SAMPLE_TPAD, 1), topk)
    need = topk - count_ge(thr + 1)
    tied = keys == thr
    incl_prefix = (lax.broadcasted_iota(I32, (128, 128), 0) <= lax.broadcasted_iota(I32, (128, 128), 1)).astype(BF16)
    seen = jnp.zeros((SAMPLE_TPAD, 1), F32)
    ranks = []
    for g in range(width // 128):
        tied_g = jnp.where(tied[:, g * 128:(g + 1) * 128], 1.0, 0.0).astype(BF16)
        pref = jnp.dot(tied_g, incl_prefix, preferred_element_type=F32)
        ranks.append(seen + pref)
        seen = seen + pref[:, 127:128]
    rank = jnp.concatenate(ranks, axis=1)
    sel = jnp.logical_or(keys > thr, jnp.logical_and(tied, rank <= need))
    o_ref[0] = jnp.where(jnp.logical_and(sel, visible), 0.0, NEG)


def _dsa_sample_select(page_table, qi_ht, wi_ht, ki_new, kidx_pool, layer, ih, ts, topk):
    bs, npg = page_table.shape
    page = kidx_pool.shape[2]
    past = npg * page
    pp = SAMPLE_PP
    tail_w = ki_new.shape[1]
    r = ih * SAMPLE_TPAD
    grid_spec = pltpu.PrefetchScalarGridSpec(
        num_scalar_prefetch=1,
        grid=(bs, npg // pp),
        in_specs=[pl.BlockSpec((1, r, IDX_DIM), lambda b, p, pt: (b, 0, 0)),
                  pl.BlockSpec((1, r, wi_ht.shape[2]), lambda b, p, pt: (b, 0, 0)),
                  pl.BlockSpec((1, tail_w, IDX_DIM), lambda b, p, pt: (b, 0, 0))]
        + _page_specs((page, IDX_DIM), pp, layer),
        out_specs=pl.BlockSpec((1, SAMPLE_TPAD, past + tail_w), lambda b, p, pt: (b, 0, 0)),
        scratch_shapes=[pltpu.VMEM((npg // pp, SAMPLE_TPAD, pp * page), F32)],
    )
    return pl.pallas_call(
        functools.partial(_dsa_sample_select_kernel, pp=pp, page=page, topk=topk, ih=ih, ts=ts, past=past),
        grid_spec=grid_spec,
        out_shape=jax.ShapeDtypeStruct((bs, SAMPLE_TPAD, past + tail_w), F32),
        compiler_params=_cparams("parallel", "arbitrary"),
        name="dsa_sample_select",
    )(page_table, qi_ht, wi_ht, ki_new, *([kidx_pool] * pp))


def _sample_flash_update(qbd, kblk, vblk, bias, m_ref, l_ref, acc_ref, nkv):
    rows_per_head = GROUP * SAMPLE_TPAD
    s = lax.dot_general(qbd, kblk, _CONTRACT_LAST, preferred_element_type=F32) * (HEAD_DIM ** -0.5) + bias
    m_old = m_ref[...]
    m_new = jnp.maximum(m_old, jnp.max(s, axis=1, keepdims=True))
    alpha = jnp.exp(m_old - m_new)
    p = jnp.exp(s - m_new)
    l_ref[...] = alpha * l_ref[...] + jnp.sum(p, axis=1, keepdims=True)
    m_ref[...] = m_new
    o_full = jnp.dot(p.astype(BF16), vblk, preferred_element_type=F32)
    o_diag = jnp.concatenate(
        [o_full[n * rows_per_head:(n + 1) * rows_per_head, n * HEAD_DIM:(n + 1) * HEAD_DIM] for n in range(nkv)],
        axis=0)
    acc_ref[...] = alpha * acc_ref[...] + o_diag


def _sample_flash_reset(m_ref, l_ref, acc_ref):
    m_ref[...] = jnp.full(m_ref.shape, NEG, F32)
    l_ref[...] = jnp.zeros(l_ref.shape, F32)
    acc_ref[...] = jnp.zeros(acc_ref.shape, F32)


def _heads_on_lanes(refs, nkv):
    keys = refs[0].shape[0] // nkv
    pages = [jnp.concatenate([r[pl.ds(n, keys, stride=nkv), :] for n in range(nkv)], axis=1).astype(BF16)
             for r in refs]
    return jnp.concatenate(pages, axis=0)


def _flat_pool(pool):
    return pool.reshape(pool.shape[0], pool.shape[1], pool.shape[2] * pool.shape[3], pool.shape[4])


def _tile_query_bias(b8, reps):
    return jnp.concatenate([b8] * reps, axis=0)


def _dsa_sample_attn_kernel(pt_ref, q_ref, bias_ref, tbias_ref, kn_ref, vn_ref, *refs, pp, nkv):
    k_refs, v_refs = refs[:pp], refs[pp:2 * pp]
    o_ref, m_ref, l_ref, acc_ref = refs[2 * pp:]
    p = pl.program_id(1)
    reps = nkv * GROUP

    @pl.when(p == 0)
    def _():
        _sample_flash_reset(m_ref, l_ref, acc_ref)

    _sample_flash_update(q_ref[0], _heads_on_lanes(k_refs, nkv), _heads_on_lanes(v_refs, nkv),
                         _tile_query_bias(bias_ref[0], reps), m_ref, l_ref, acc_ref, nkv)

    @pl.when(p == pl.num_programs(1) - 1)
    def _():
        _sample_flash_update(q_ref[0], kn_ref[0].astype(BF16), vn_ref[0].astype(BF16),
                             _tile_query_bias(tbias_ref[0], reps), m_ref, l_ref, acc_ref, nkv)
        o_ref[0] = acc_ref[...] / l_ref[...]


def _sample_attn_specs(nkv, tail_rows):
    rows = nkv * GROUP * SAMPLE_TPAD
    kvw = nkv * HEAD_DIM
    q_spec = pl.BlockSpec((1, rows, kvw), lambda b, p, pt: (b, 0, 0))
    tail_spec = pl.BlockSpec((1, tail_rows, kvw), lambda b, p, pt: (b, 0, 0))
    out_spec = pl.BlockSpec((1, rows, HEAD_DIM), lambda b, p, pt: (b, 0, 0))
    scratch = [pltpu.VMEM((rows, 1), F32), pltpu.VMEM((rows, 1), F32), pltpu.VMEM((rows, HEAD_DIM), F32)]
    return rows, q_spec, tail_spec, out_spec, scratch


def _dsa_sample_attn(page_table, qbd, bias, k_new, v_new, k_pool, v_pool, layer, nkv):
    bs, npg = page_table.shape
    page = k_pool.shape[2]
    pp = SAMPLE_PP
    rows, q_spec, tail_spec, out_spec, scratch = _sample_attn_specs(nkv, k_new.shape[1])
    grid_spec = pltpu.PrefetchScalarGridSpec(
        num_scalar_prefetch=1,
        grid=(bs, npg // pp),
        in_specs=[q_spec,
                  pl.BlockSpec((1, SAMPLE_TPAD, page * pp), lambda b, p, pt: (b, 0, p)),
                  pl.BlockSpec((1, SAMPLE_TPAD, page), lambda b, p, pt: (b, 0, npg)),
                  tail_spec, tail_spec]
        + _page_specs((page * nkv, HEAD_DIM), pp, layer) + _page_specs((page * nkv, HEAD_DIM), pp, layer),
        out_specs=out_spec,
        scratch_shapes=scratch,
    )
    return pl.pallas_call(
        functools.partial(_dsa_sample_attn_kernel, pp=pp, nkv=nkv),
        grid_spec=grid_spec,
        out_shape=jax.ShapeDtypeStruct((bs, rows, HEAD_DIM), F32),
        compiler_params=_cparams("parallel", "arbitrary"),
        name="dsa_sample_attn",
    )(page_table, qbd, bias, bias, k_new, v_new, *([_flat_pool(k_pool)] * pp), *([_flat_pool(v_pool)] * pp))


KMEAN_BLOCKS = 8


def _moba_kmean_kernel(pt_ref, *refs, nblk, ppb):
    k_refs, o_ref = refs[:nblk * ppb], refs[nblk * ppb]
    for blk in range(nblk):
        acc = jnp.sum(k_refs[blk * ppb][...], axis=0)
        for c in range(1, ppb):
            acc = acc + jnp.sum(k_refs[blk * ppb + c][...], axis=0)
        o_ref[0, blk] = acc * (1.0 / MOBA_BLOCK)


def _moba_sample_kmean(page_table, k_pool, layer, nkv):
    bs, npg = page_table.shape
    page = k_pool.shape[2]
    ppb = MOBA_BLOCK // page
    nb = npg // ppb
    nblk = KMEAN_BLOCKS
    grid_spec = pltpu.PrefetchScalarGridSpec(
        num_scalar_prefetch=1,
        grid=(bs, nb // nblk),
        in_specs=_page_specs((page, nkv, HEAD_DIM), nblk * ppb, layer),
        out_specs=pl.BlockSpec((1, nblk, nkv, HEAD_DIM), lambda b, p, pt: (b, p, 0, 0)),
    )
    return pl.pallas_call(
        functools.partial(_moba_kmean_kernel, nblk=nblk, ppb=ppb),
        grid_spec=grid_spec,
        out_shape=jax.ShapeDtypeStruct((bs, nb, nkv, HEAD_DIM), F32),
        compiler_params=_cparams("parallel", "arbitrary"),
        name="moba_sample_kmean",
    )(page_table, *([k_pool] * (nblk * ppb)))


def _moba_sample_attn_kernel(pt_ref, q_ref, km_ref, kn_ref, vn_ref, *refs, pp, nkv, page, nb, ts):
    k_refs, v_refs = refs[:pp], refs[pp:2 * pp]
    o_ref, m_ref, l_ref, acc_ref, sb_ref = refs[2 * pp:]
    p = pl.program_id(1)
    rows = nkv * GROUP * SAMPLE_TPAD
    ppb = MOBA_BLOCK // page
    blk_iota = lax.broadcasted_iota(I32, (rows, nb), 1)

    @pl.when(p == 0)
    def _():
        _sample_flash_reset(m_ref, l_ref, acc_ref)
        valid = blk_iota >= 0
        gate = lax.dot_general(q_ref[0], km_ref[0].astype(BF16), _CONTRACT_LAST, preferred_element_type=F32)
        sb_ref[...] = _topb_select_bias(gate, valid, blk_iota, nb)

    selb = sb_ref[...]
    cols = []
    for c in range(pp // ppb):
        blk = p * (pp // ppb) + c
        col = jnp.sum(jnp.where(blk_iota == blk, selb, 0.0), axis=1, keepdims=True)
        cols.append(jnp.broadcast_to(col, (rows, MOBA_BLOCK)))
    _sample_flash_update(q_ref[0], _heads_on_lanes(k_refs, nkv), _heads_on_lanes(v_refs, nkv),
                         jnp.concatenate(cols, axis=1), m_ref, l_ref, acc_ref, nkv)

    @pl.when(p == pl.num_programs(1) - 1)
    def _():
        tail_w = kn_ref.shape[1]
        row = lax.broadcasted_iota(I32, (rows, tail_w), 0) & (SAMPLE_TPAD - 1)
        col = lax.broadcasted_iota(I32, (rows, tail_w), 1)
        tb = jnp.where(jnp.logical_and(col <= row, col < ts), 0.0, NEG)
        _sample_flash_update(q_ref[0], kn_ref[0].astype(BF16), vn_ref[0].astype(BF16), tb,
                             m_ref, l_ref, acc_ref, nkv)
        o_ref[0] = acc_ref[...] / l_ref[...]


def _moba_sample_attn(page_table, qbd, kmean, k_new, v_new, k_pool, v_pool, layer, nkv, ts):
    bs, npg = page_table.shape
    page = k_pool.shape[2]
    kvw = nkv * HEAD_DIM
    pp = SAMPLE_PP
    nb = kmean.shape[1]
    rows, q_spec, tail_spec, out_spec, scratch = _sample_attn_specs(nkv, k_new.shape[1])
    grid_spec = pltpu.PrefetchScalarGridSpec(
        num_scalar_prefetch=1,
        grid=(bs, npg // pp),
        in_specs=[q_spec, pl.BlockSpec((1, nb, kvw), lambda b, p, pt: (b, 0, 0)), tail_spec, tail_spec]
        + _page_specs((page * nkv, HEAD_DIM), pp, layer) + _page_specs((page * nkv, HEAD_DIM), pp, layer),
        out_specs=out_spec,
        scratch_shapes=scratch + [pltpu.VMEM((rows, nb), F32)],
    )
    return pl.pallas_call(
        functools.partial(_moba_sample_attn_kernel, pp=pp, nkv=nkv, page=page, nb=nb, ts=ts),
        grid_spec=grid_spec,
        out_shape=jax.ShapeDtypeStruct((bs, rows, HEAD_DIM), F32),
        compiler_params=_cparams("parallel", "arbitrary"),
        name="moba_sample_attn",
    )(page_table, qbd, kmean, k_new, v_new, *([_flat_pool(k_pool)] * pp), *([_flat_pool(v_pool)] * pp))


def _heads_to_kv_major(q, bs, nkv):
    x = q.reshape(bs, SAMPLE_TPAD, nkv, GROUP, HEAD_DIM).transpose(0, 2, 3, 1, 4)
    eye = jnp.eye(nkv, dtype=q.dtype)
    bd = x[:, :, :, :, None, :] * eye[None, :, None, None, :, None]
    return bd.reshape(bs, nkv * GROUP * SAMPLE_TPAD, nkv * HEAD_DIM).astype(BF16)


def _kv_major_to_rows(o, bs, nkv):
    x = o.reshape(bs, nkv, GROUP, SAMPLE_TPAD, HEAD_DIM)
    return x.transpose(0, 3, 1, 2, 4).reshape(bs * SAMPLE_TPAD, nkv * GROUP * HEAD_DIM)


def _pad_rows(x, n):
    return jnp.pad(x, ((0, 0), (0, n - x.shape[1]), (0, 0)))


def _merge_mixers(h, mix, qm, mk, mv, bsz):
    m = h.shape[0]
    ma = _mem_attn(qm.reshape(bsz, m // bsz, -1), mk, mv, qm.dtype).reshape(m, -1)
    return mix.astype(BF16), ma.astype(BF16)


def _finish_layer(hp, hs, merged_p, merged_s, layer, w_o, g_ffn, w_up, w_down):
    hp = _matmul(merged_p, w_o, layer, F32, epilogue="residual", residual=hp)
    hs = _matmul(merged_s, w_o, layer, F32, epilogue="residual", residual=hs)
    up, us = _matmul(_rmsnorm(hp, g_ffn, BF16), w_up, layer, BF16, epilogue="relu2", a2=_rmsnorm(hs, g_ffn, BF16))
    hp = _matmul(up, w_down, layer, F32, epilogue="residual", residual=hp)
    hs = _matmul(us, w_down, layer, F32, epilogue="residual", residual=hs)
    return hp, hs


def kernel(x_prompt, x_sample, cache_dsa_k, cache_dsa_v, cache_dsa_kidx, cache_moba_k, cache_moba_v, cache_mem_k, cache_mem_v, page_table, mem_prompt, norm_mix, norm_mem, w_in_dsa, w_in_moba, w_mem_kv, w_out, norm_ffn, w_up, w_down, norm_final):
    bsz, t, d = x_prompt.shape
    bs, ts, _ = x_sample.shape
    depth = norm_mix.shape[0]
    nh = (3 * d) // (4 * HEAD_DIM)
    nkv = nh // GROUP
    qw, kvw = nh * HEAD_DIM, nkv * HEAD_DIM
    memw = d // 4
    mhd = memw // MEM_HEADS
    ih = d // 128
    npg = page_table.shape[1]
    page = cache_dsa_k.shape[2]
    past = npg * page
    mlen = mem_prompt.shape[1]
    tp = SAMPLE_TPAD

    hp = x_prompt.reshape(bsz * t, d)
    hs = jnp.pad(x_sample, ((0, 0), (0, tp - ts), (0, 0))).reshape(bs * tp, d)
    pos_p = jnp.arange(t, dtype=I32)
    pos_s = past + (jnp.arange(tp, dtype=I32) % ts)
    tabs_p = _rope_tables(pos_p)
    tabs_s = _rope_tables(jnp.tile(pos_s, bs))

    outs = {name: [] for name in ("pdk", "pdv", "pdki", "sdk", "sdv", "sdki", "pmk", "pmv", "smk", "smv", "mk", "mv")}
    mem_flat = mem_prompt.reshape(bsz * mlen, d)

    w_o, wu, wd, w_mem = w_out.astype(BF16), w_up.astype(BF16), w_down.astype(BF16), w_mem_kv.astype(BF16)
    w_dsa, w_moba = w_in_dsa.astype(BF16), w_in_moba.astype(BF16)
    c0 = qw + 2 * kvw + ih * IDX_DIM
    w_dsa_tail = jnp.concatenate(
        [w_in_dsa[:, :, c0 + ih:c0 + ih + IDX_DIM], w_in_dsa[:, :, c0 + ih + IDX_DIM:], w_in_dsa[:, :, c0:c0 + ih],
         jnp.zeros(w_in_dsa.shape[:2] + (128 - ih,), F32)], axis=2).astype(BF16)

    for i in range(depth):
        j = i // 2
        mkv = _matmul(_rmsnorm(mem_flat, norm_mem[i], BF16), w_mem, i, F32)
        mk_p = mkv[:, :memw].reshape(bsz, mlen, memw)
        mv_p = mkv[:, memw:].reshape(bsz, mlen, memw)
        outs["mk"].append(mk_p.reshape(bsz, mlen, MEM_HEADS, mhd))
        outs["mv"].append(mv_p.reshape(bsz, mlen, MEM_HEADS, mhd))
        hn_p = _rmsnorm(hp, norm_mix[i], BF16)
        hn_s = _rmsnorm(hs, norm_mix[i], BF16)
        mk_s = cache_mem_k[i].reshape(bs, mlen, memw)
        mv_s = cache_mem_v[i].reshape(bs, mlen, memw)
        if i % 2 == 0:
            q = _matmul(hn_p, w_dsa, j, BF16, n=qw, epilogue="rope", rope=(tabs_p, t))
            qi = _matmul(hn_p, w_dsa, j, BF16, n=ih * IDX_DIM, n0=qw + 2 * kvw, epilogue="rope", rope=(tabs_p, t))
            pkv = _matmul(hn_p, w_dsa, j, F32, n=2 * kvw, n0=qw)
            pb, pb_s = _matmul(hn_p, w_dsa_tail, j, F32, a2=hn_s)
            kf, kb, vf, vb, kif, kib, wi, qm = _dsa_split(pkv, pb, tabs_p, t, nh, nkv, ih, memw, BF16, with_q=False)
            outs["pdk"].append(kf.reshape(bsz, t, nkv, HEAD_DIM))
            outs["pdv"].append(vf.reshape(bsz, t, nkv, HEAD_DIM))
            outs["pdki"].append(kif.reshape(bsz, t, IDX_DIM))
            mix_p = _dsa_prompt(qi, wi, kib, q, kb, vb, bsz, t, nkv, ih, min(DSA_TOPK, t // 4))
            pa_s = _matmul(hn_s, w_dsa, j, F32, n=c0)
            q_s, qi_s, kf, _, vf, _, kif, _, wi_s, qm_s = _dsa_split(pa_s, pb_s, tabs_s, bs * tp, nh, nkv, ih, memw, F32,
                                                                     with_q=True)
            outs["sdk"].append(kf.reshape(bs, tp, nkv, HEAD_DIM)[:, :ts])
            outs["sdv"].append(vf.reshape(bs, tp, nkv, HEAD_DIM)[:, :ts])
            outs["sdki"].append(kif.reshape(bs, tp, IDX_DIM)[:, :ts])
            qi_ht = qi_s.reshape(bs, tp, ih, IDX_DIM).transpose(0, 2, 1, 3).reshape(bs, ih * tp, IDX_DIM)
            wi_ht = wi_s.reshape(bs, tp, 128)[:, :, :ih].transpose(0, 2, 1).reshape(bs, ih * tp, 1)
            wi_ht = jnp.broadcast_to(wi_ht, (bs, ih * tp, page))
            ki_new = _pad_rows(kif.reshape(bs, tp, IDX_DIM), page)
            bias = _dsa_sample_select(page_table, qi_ht, wi_ht, ki_new, cache_dsa_kidx, j, ih, ts,
                                      min(DSA_TOPK, (past + ts) // 4))
            o_s = _dsa_sample_attn(page_table, _heads_to_kv_major(q_s, bs, nkv), bias,
                                   _pad_rows(kf.reshape(bs, tp, kvw), page), _pad_rows(vf.reshape(bs, tp, kvw), page),
                                   cache_dsa_k, cache_dsa_v, j, nkv)
            mix_s = _kv_major_to_rows(o_s, bs, nkv)
        else:
            q = _matmul(hn_p, w_moba, j, BF16, n=qw, epilogue="rope", rope=(tabs_p, t))
            p_rest = _matmul(hn_p, w_moba, j, F32, n=2 * kvw + memw, n0=qw)
            kf, kb, vf, vb, qm, kmean = _moba_split(p_rest, tabs_p, t, nh, nkv, memw, BF16, with_q=False)
            outs["pmk"].append(kf.reshape(bsz, t, nkv, HEAD_DIM))
            outs["pmv"].append(vf.reshape(bsz, t, nkv, HEAD_DIM))
            mix_p = _moba_prompt(q, kb, vb, kmean, bsz, t, nkv)
            p_s = _matmul(hn_s, w_moba, j, F32)
            q_s, kf, _, vf, _, qm_s, _ = _moba_split(p_s, tabs_s, bs * tp, nh, nkv, memw, F32, with_q=True)
            outs["smk"].append(kf.reshape(bs, tp, nkv, HEAD_DIM)[:, :ts])
            outs["smv"].append(vf.reshape(bs, tp, nkv, HEAD_DIM)[:, :ts])
            kmean_s = _moba_sample_kmean(page_table, cache_moba_k, j, nkv)
            kmean_s = kmean_s.reshape(bs, kmean_s.shape[1], kvw)
            o_s = _moba_sample_attn(page_table, _heads_to_kv_major(q_s, bs, nkv), kmean_s,
                                    _pad_rows(kf.reshape(bs, tp, kvw), page), _pad_rows(vf.reshape(bs, tp, kvw), page),
                                    cache_moba_k, cache_moba_v, j, nkv, ts)
            mix_s = _kv_major_to_rows(o_s, bs, nkv)
        hp, hs = _finish_layer(hp, hs, _merge_mixers(hp, mix_p, qm, mk_p, mv_p, bsz),
                               _merge_mixers(hs, mix_s, qm_s, mk_s, mv_s, bs), i, w_o, norm_ffn[i], wu, wd)

    y_prompt = _rmsnorm(hp, norm_final, F32).reshape(bsz, t, d)
    y_sample = _rmsnorm(hs, norm_final, F32).reshape(bs, tp, d)[:, :ts]
    st = jnp.stack
    return (y_prompt, y_sample, st(outs["pdk"]), st(outs["pdv"]), st(outs["pdki"]), st(outs["pmk"]), st(outs["pmv"]),
            st(outs["mk"]), st(outs["mv"]), st(outs["sdk"]), st(outs["sdv"]), st(outs["sdki"]),
            st(outs["smk"]), st(outs["smv"]))
```
